```python
import jax, jax.numpy as jnp
from jax import lax
import numpy as np

D_MODEL = 4096
BATCH = 2
SEQ = 8192
DEPTH = 2

N_MIXERS = 2
NORM_EPS = 1e-6
INNER_NORM_EPS = 1e-5
D_FF = -(-8 * D_MODEL // (3 * 256)) * 256

SSD_EXPAND = 2
SSD_D_INNER = SSD_EXPAND * D_MODEL
SSD_HEAD_DIM = 64
SSD_N_HEADS = SSD_D_INNER // SSD_HEAD_DIM
SSD_N_GROUPS = 8
SSD_HEADS_PER_GROUP = SSD_N_HEADS // SSD_N_GROUPS
SSD_D_STATE = 128
SSD_D_CONV = 4
SSD_CHUNK = 256
SSD_BC_DIM = SSD_N_GROUPS * SSD_D_STATE
SSD_CONV_DIM = SSD_D_INNER + 2 * SSD_BC_DIM
SSD_IN_DIM = SSD_D_INNER + SSD_CONV_DIM + SSD_N_HEADS

GLA_KEY_DIM = D_MODEL // 2
GLA_VALUE_DIM = D_MODEL
GLA_HEAD_K = 256
GLA_N_HEADS = GLA_KEY_DIM // GLA_HEAD_K
GLA_HEAD_V = GLA_VALUE_DIM // GLA_N_HEADS
GLA_GATE_RANK = 16
GLA_GATE_NORMALIZER = 16.0
GLA_CHUNK = 64
GLA_IN_DIM = 2 * GLA_KEY_DIM + 2 * GLA_VALUE_DIM + GLA_GATE_RANK

kernel_name = 'hybrid_ssd_gla_trunk'


def rms_norm(x, w, eps):
    xf = x.astype(jnp.float32)
    y = xf * lax.rsqrt(jnp.mean(xf * xf, axis=-1, keepdims=True) + eps)
    return (y * w.astype(jnp.float32)).astype(x.dtype)


def to_chunks(t, chunk):
    bsz, seqlen = t.shape[:2]
    n_pad = (-seqlen) % chunk
    t = jnp.pad(t, [(0, 0), (0, n_pad)] + [(0, 0)] * (t.ndim - 2))
    t = t.reshape(bsz, (seqlen + n_pad) // chunk, chunk, *t.shape[2:])
    return jnp.moveaxis(t, 1, 0)


def from_chunks(t, seqlen):
    t = jnp.moveaxis(t, 0, 1)
    t = t.reshape(t.shape[0], t.shape[1] * t.shape[2], *t.shape[3:])
    return t[:, :seqlen]


def causal_depthwise_conv(u, w, b):
    seqlen = u.shape[1]
    k = w.shape[1]
    up = jnp.pad(u, ((0, 0), (k - 1, 0), (0, 0)))
    out = up[:, 0:seqlen] * w[:, 0]
    for j in range(1, k):
        out = out + up[:, j:j + seqlen] * w[:, j]
    return out + b


def ssd_chunked_scan(x, dt, a, b_in, c_in):
    bsz, seqlen = x.shape[:2]
    log_a = dt * a
    xdt = x * dt[..., None]
    xs_c = to_chunks(xdt, SSD_CHUNK)
    la_c = to_chunks(log_a, SSD_CHUNK)
    b_c = to_chunks(b_in, SSD_CHUNK)
    c_c = to_chunks(c_in, SSD_CHUNK)
    causal = jnp.tril(jnp.ones((SSD_CHUNK, SSD_CHUNK), dtype=bool))[None, :, :, None, None]

    def step(state, inp):
        xc, lac, bc, cc = inp
        cs = jnp.cumsum(lac, axis=1)
        seg = cs[:, :, None] - cs[:, None, :]
        decay = jnp.exp(jnp.where(causal, seg, -jnp.inf))
        cb = jnp.einsum('blgn,bsgn->blsg', cc, bc)
        y_diag = jnp.einsum('blsg,blsgr,bsgrp->blgrp', cb, decay, xc)
        y_off = jnp.einsum('blgn,bgrpn,blgr->blgrp', cc, state, jnp.exp(cs))
        cs_end = cs[:, -1]
        decay_end = jnp.exp(cs_end[:, None] - cs)
        new_state = state * jnp.exp(cs_end)[..., None, None] + jnp.einsum(
            'blgn,blgr,blgrp->bgrpn', bc, decay_end, xc)
        return new_state, y_diag + y_off

    state0 = jnp.zeros((bsz, SSD_N_GROUPS, SSD_HEADS_PER_GROUP, SSD_HEAD_DIM, SSD_D_STATE), jnp.float32)
    _, y_c = lax.scan(step, state0, (xs_c, la_c, b_c, c_c))
    return from_chunks(y_c, seqlen)


def ssd_mixer(h, in_proj, conv_w, conv_b, dt_bias, a_log, d_skip, norm_w, out_proj):
    bsz, seqlen, _ = h.shape
    f32 = jnp.float32
    zxbcdt = h @ in_proj
    z, xbc, dt_raw = jnp.split(zxbcdt, [SSD_D_INNER, SSD_D_INNER + SSD_CONV_DIM], axis=-1)
    xbc = jax.nn.silu(causal_depthwise_conv(xbc, conv_w, conv_b))
    xs, b_in, c_in = jnp.split(xbc, [SSD_D_INNER, SSD_D_INNER + SSD_BC_DIM], axis=-1)
    xs = xs.astype(f32).reshape(bsz, seqlen, SSD_N_GROUPS, SSD_HEADS_PER_GROUP, SSD_HEAD_DIM)
    b_in = b_in.astype(f32).reshape(bsz, seqlen, SSD_N_GROUPS, SSD_D_STATE)
    c_in = c_in.astype(f32).reshape(bsz, seqlen, SSD_N_GROUPS, SSD_D_STATE)
    dt = jax.nn.softplus(dt_raw.astype(f32) + dt_bias.astype(f32))
    dt = dt.reshape(bsz, seqlen, SSD_N_GROUPS, SSD_HEADS_PER_GROUP)
    a = -jnp.exp(a_log.astype(f32)).reshape(SSD_N_GROUPS, SSD_HEADS_PER_GROUP)
    y = ssd_chunked_scan(xs, dt, a, b_in, c_in)
    y = y + xs * d_skip.astype(f32).reshape(SSD_N_GROUPS, SSD_HEADS_PER_GROUP, 1)
    y = y.reshape(bsz, seqlen, SSD_D_INNER) * jax.nn.silu(z.astype(f32))
    y = rms_norm(y.reshape(bsz, seqlen, SSD_N_GROUPS, SSD_D_INNER // SSD_N_GROUPS),
                 norm_w.reshape(SSD_N_GROUPS, SSD_D_INNER // SSD_N_GROUPS), INNER_NORM_EPS)
    y = y.reshape(bsz, seqlen, SSD_D_INNER)
    return y.astype(h.dtype) @ out_proj


def gla_chunked_scan(q, k, v, log_g):
    bsz, seqlen = q.shape[:2]
    q_c = to_chunks(q, GLA_CHUNK)
    k_c = to_chunks(k, GLA_CHUNK)
    v_c = to_chunks(v, GLA_CHUNK)
    g_c = to_chunks(log_g, GLA_CHUNK)
    causal = jnp.tril(jnp.ones((GLA_CHUNK, GLA_CHUNK), dtype=bool))[None, None]

    def step(state, inp):
        qc, kc, vc, gc = inp
        cs = jnp.cumsum(gc, axis=1)
        q_dec = qc * jnp.exp(cs)
        k_inv = kc * jnp.exp(-cs)
        scores = jnp.where(causal, jnp.einsum('blhd,bshd->bhls', q_dec, k_inv), 0.0)
        o = jnp.einsum('bhls,bshv->blhv', scores, vc) + jnp.einsum('blhd,bhdv->blhv', q_dec, state)
        cs_end = cs[:, -1]
        k_end = kc * jnp.exp(cs_end[:, None] - cs)
        new_state = state * jnp.exp(cs_end)[..., None] + jnp.einsum('blhd,blhv->bhdv', k_end, vc)
        return new_state, o

    state0 = jnp.zeros((bsz, GLA_N_HEADS, GLA_HEAD_K, GLA_HEAD_V), jnp.float32)
    _, o_c = lax.scan(step, state0, (q_c, k_c, v_c, g_c))
    return from_chunks(o_c, seqlen)


def gla_mixer(h, in_proj, gk_up, gk_bias, norm_w, out_proj):
    bsz, seqlen, _ = h.shape
    f32 = jnp.float32
    proj = h @ in_proj
    q, k, v, g, gk_low = jnp.split(
        proj, [GLA_KEY_DIM, 2 * GLA_KEY_DIM, 2 * GLA_KEY_DIM + GLA_VALUE_DIM,
               2 * GLA_KEY_DIM + 2 * GLA_VALUE_DIM], axis=-1)
    log_g = jax.nn.log_sigmoid((gk_low @ gk_up + gk_bias).astype(f32)) / GLA_GATE_NORMALIZER
    q = q.astype(f32).reshape(bsz, seqlen, GLA_N_HEADS, GLA_HEAD_K) * (GLA_HEAD_K ** -0.5)
    k = k.astype(f32).reshape(bsz, seqlen, GLA_N_HEADS, GLA_HEAD_K)
    v = v.astype(f32).reshape(bsz, seqlen, GLA_N_HEADS, GLA_HEAD_V)
    log_g = log_g.reshape(bsz, seqlen, GLA_N_HEADS, GLA_HEAD_K)
    o = gla_chunked_scan(q, k, v, log_g)
    o = rms_norm(o, norm_w, INNER_NORM_EPS)
    o = o * jax.nn.silu(g.astype(f32).reshape(bsz, seqlen, GLA_N_HEADS, GLA_HEAD_V))
    return o.reshape(bsz, seqlen, GLA_VALUE_DIM).astype(h.dtype) @ out_proj


def swiglu_ffn(h, w_gate, w_up, w_down):
    return (jax.nn.silu(h @ w_gate) * (h @ w_up)) @ w_down


def setup_inputs(seed: int = 0) -> dict:
    key = jax.random.key(seed)
    ks = jax.random.split(key, 32)
    f32 = jnp.float32

    def nrm(k, shape, scale):
        return jax.random.normal(k, shape, f32) * scale

    def gain(k, n):
        return 1.0 + 0.01 * jax.random.normal(k, (n,), f32)

    dt0 = jnp.exp(jax.random.uniform(ks[5], (SSD_N_HEADS,), f32,
                                     minval=float(np.log(1e-3)), maxval=float(np.log(1e-1))))
    return {
        'x': jax.random.normal(ks[0], (BATCH, SEQ, D_MODEL), f32),
        'l0_mixer_norm': gain(ks[1], D_MODEL),
        'l0_ssd_in_proj': nrm(ks[2], (D_MODEL, SSD_IN_DIM), D_MODEL ** -0.5),
        'l0_ssd_conv_w': nrm(ks[3], (SSD_CONV_DIM, SSD_D_CONV), SSD_D_CONV ** -0.5),
        'l0_ssd_conv_b': nrm(ks[4], (SSD_CONV_DIM,), 0.01),
        'l0_ssd_dt_bias': dt0 + jnp.log(-jnp.expm1(-dt0)),
        'l0_ssd_a_log': jnp.log(jax.random.uniform(ks[6], (SSD_N_HEADS,), f32, minval=1.0, maxval=16.0)),
        'l0_ssd_d': 1.0 + 0.1 * jax.random.normal(ks[7], (SSD_N_HEADS,), f32),
        'l0_ssd_norm_w': gain(ks[8], SSD_D_INNER),
        'l0_ssd_out_proj': nrm(ks[9], (SSD_D_INNER, D_MODEL), SSD_D_INNER ** -0.5),
        'l0_ffn_norm': gain(ks[10], D_MODEL),
        'l0_ffn_w_gate': nrm(ks[11], (D_MODEL, D_FF), D_MODEL ** -0.5),
        'l0_ffn_w_up': nrm(ks[12], (D_MODEL, D_FF), D_MODEL ** -0.5),
        'l0_ffn_w_down': nrm(ks[13], (D_FF, D_MODEL), D_FF ** -0.5),
        'l1_mixer_norm': gain(ks[14], D_MODEL),
        'l1_gla_in_proj': nrm(ks[15], (D_MODEL, GLA_IN_DIM), D_MODEL ** -0.5),
        'l1_gla_gk_up': nrm(ks[16], (GLA_GATE_RANK, GLA_KEY_DIM), GLA_GATE_RANK ** -0.5),
        'l1_gla_gk_bias': nrm(ks[17], (GLA_KEY_DIM,), 0.01),
        'l1_gla_norm_w': gain(ks[18], GLA_HEAD_V),
        'l1_gla_out_proj': nrm(ks[19], (GLA_VALUE_DIM, D_MODEL), GLA_VALUE_DIM ** -0.5),
        'l1_ffn_norm': gain(ks[20], D_MODEL),
        'l1_ffn_w_gate': nrm(ks[21], (D_MODEL, D_FF), D_MODEL ** -0.5),
        'l1_ffn_w_up': nrm(ks[22], (D_MODEL, D_FF), D_MODEL ** -0.5),
        'l1_ffn_w_down': nrm(ks[23], (D_FF, D_MODEL), D_FF ** -0.5),
        'final_norm': gain(ks[24], D_MODEL),
    }


def reference(x,
              l0_mixer_norm, l0_ssd_in_proj, l0_ssd_conv_w, l0_ssd_conv_b, l0_ssd_dt_bias,
              l0_ssd_a_log, l0_ssd_d, l0_ssd_norm_w, l0_ssd_out_proj,
              l0_ffn_norm, l0_ffn_w_gate, l0_ffn_w_up, l0_ffn_w_down,
              l1_mixer_norm, l1_gla_in_proj, l1_gla_gk_up, l1_gla_gk_bias, l1_gla_norm_w,
              l1_gla_out_proj,
              l1_ffn_norm, l1_ffn_w_gate, l1_ffn_w_up, l1_ffn_w_down,
              final_norm):
    layers = (
        (l0_mixer_norm,
         (l0_ssd_in_proj, l0_ssd_conv_w, l0_ssd_conv_b, l0_ssd_dt_bias, l0_ssd_a_log,
          l0_ssd_d, l0_ssd_norm_w, l0_ssd_out_proj),
         l0_ffn_norm, (l0_ffn_w_gate, l0_ffn_w_up, l0_ffn_w_down)),
        (l1_mixer_norm,
         (l1_gla_in_proj, l1_gla_gk_up, l1_gla_gk_bias, l1_gla_norm_w, l1_gla_out_proj),
         l1_ffn_norm, (l1_ffn_w_gate, l1_ffn_w_up, l1_ffn_w_down)),
    )
    for i in range(DEPTH):
        mixer_norm, mixer_params, ffn_norm, ffn_params = layers[i]
        h = rms_norm(x, mixer_norm, NORM_EPS)
        if i % N_MIXERS == 0:
            x = x + ssd_mixer(h, *mixer_params)
        else:
            x = x + gla_mixer(h, *mixer_params)
        x = x + swiglu_ffn(rms_norm(x, ffn_norm, NORM_EPS), *ffn_params)
    return rms_norm(x, final_norm, NORM_EPS)
```

```python
import functools

import jax
import jax.numpy as jnp
from jax import lax
from jax.experimental import pallas as pl
from jax.experimental.pallas import tpu as pltpu

F32 = jnp.float32
BF16 = jnp.bfloat16
HIGHEST = lax.Precision.HIGHEST

NORM_EPS = 1e-6
INNER_NORM_EPS = 1e-5
SSD_D_STATE = 128
SSD_CHUNK = 256
GLA_CHUNK = 64
GLA_GATE_NORMALIZER = 16.0

V7X_LANES = 128
V7X_BF16_SUBLANES = 16
V7X_VMEM_BYTES = 64 * 1024 * 1024
VMEM_LIMIT_BYTES = V7X_VMEM_BYTES - 8 * 1024 * 1024

MM_TILE_M = 1024
MM_TILE_N = 1024
MM_MAX_TILE_K = 3072
GATEUP_TILE_N = 512
ROW_TILE = 256
CONV_COL_TILE = 1024
GLA_ROW_BLOCK = 256


def _params(*semantics):
    return pltpu.CompilerParams(dimension_semantics=semantics, vmem_limit_bytes=VMEM_LIMIT_BYTES)


def _tile(dim, pref, align=V7X_LANES):
    if dim <= pref:
        return dim
    t = (pref // align) * align
    while t >= align:
        if dim % t == 0:
            return t
        t -= align
    return dim


def _silu(x):
    return x * jax.nn.sigmoid(x)


def _softplus(x):
    return jnp.maximum(x, 0.0) + jnp.log1p(jnp.exp(-jnp.abs(x)))


def _rmsnorm_kernel(x_ref, w_ref, o_ref, *, eps):
    x = x_ref[...]
    ms = jnp.mean(x * x, axis=-1, keepdims=True)
    o_ref[...] = (x * lax.rsqrt(ms + eps) * w_ref[...]).astype(o_ref.dtype)


def _rmsnorm(x, w, eps, out_dtype):
    t, d = x.shape
    tr = _tile(t, ROW_TILE, 8)
    return pl.pallas_call(
        functools.partial(_rmsnorm_kernel, eps=eps),
        grid=(t // tr,),
        in_specs=[pl.BlockSpec((tr, d), lambda i: (i, 0)), pl.BlockSpec((1, d), lambda i: (0, 0))],
        out_specs=pl.BlockSpec((tr, d), lambda i: (i, 0)),
        out_shape=jax.ShapeDtypeStruct((t, d), out_dtype),
        compiler_params=_params("parallel"),
        name="rmsnorm",
    )(x, w.reshape(1, d).astype(F32))


def _mm_kernel(a_ref, w_ref, o_ref):
    o_ref[...] = jnp.dot(a_ref[...], w_ref[...], preferred_element_type=F32).astype(o_ref.dtype)


def _matmul(a, w, out_dtype):
    m, k = a.shape
    n = w.shape[1]
    tm, tn = _tile(m, MM_TILE_M), _tile(n, MM_TILE_N)
    return pl.pallas_call(
        _mm_kernel,
        grid=(m // tm, n // tn),
        in_specs=[pl.BlockSpec((tm, k), lambda i, j: (i, 0)), pl.BlockSpec((k, tn), lambda i, j: (0, j))],
        out_specs=pl.BlockSpec((tm, tn), lambda i, j: (i, j)),
        out_shape=jax.ShapeDtypeStruct((m, n), out_dtype),
        compiler_params=_params("parallel", "parallel"),
        name="matmul",
    )(a, w)


def _mm_res_kernel(a_ref, w_ref, r_ref, o_ref):
    part = jnp.dot(a_ref[...], w_ref[...], preferred_element_type=F32)

    @pl.when(pl.program_id(2) == 0)
    def _():
        o_ref[...] = r_ref[...] + part

    @pl.when(pl.program_id(2) != 0)
    def _():
        o_ref[...] += part


def _matmul_residual(a, w, res):
    m, k = a.shape
    n = w.shape[1]
    tm, tn, tk = _tile(m, MM_TILE_M), _tile(n, MM_TILE_N), _tile(k, MM_MAX_TILE_K)
    return pl.pallas_call(
        _mm_res_kernel,
        grid=(m // tm, n // tn, k // tk),
        in_specs=[
            pl.BlockSpec((tm, tk), lambda i, j, kk: (i, kk)),
            pl.BlockSpec((tk, tn), lambda i, j, kk: (kk, j)),
            pl.BlockSpec((tm, tn), lambda i, j, kk: (i, j)),
        ],
        out_specs=pl.BlockSpec((tm, tn), lambda i, j, kk: (i, j)),
        out_shape=jax.ShapeDtypeStruct((m, n), F32),
        compiler_params=_params("parallel", "parallel", "arbitrary"),
        name="matmul_residual",
    )(a, w, res)


def _gateup_kernel(h_ref, wg_ref, wu_ref, o_ref):
    h = h_ref[...]
    g = jnp.dot(h, wg_ref[...], preferred_element_type=F32)
    u = jnp.dot(h, wu_ref[...], preferred_element_type=F32)
    o_ref[...] = (_silu(g) * u).astype(o_ref.dtype)


def _ffn_gateup(h, wg, wu):
    m, k = h.shape
    n = wg.shape[1]
    tm, tn = _tile(m, MM_TILE_M), _tile(n, GATEUP_TILE_N)
    return pl.pallas_call(
        _gateup_kernel,
        grid=(m // tm, n // tn),
        in_specs=[
            pl.BlockSpec((tm, k), lambda i, j: (i, 0)),
            pl.BlockSpec((k, tn), lambda i, j: (0, j)),
            pl.BlockSpec((k, tn), lambda i, j: (0, j)),
        ],
        out_specs=pl.BlockSpec((tm, tn), lambda i, j: (i, j)),
        out_shape=jax.ShapeDtypeStruct((m, n), BF16),
        compiler_params=_params("parallel", "parallel"),
        name="ffn_gateup",
    )(h, wg, wu)


def _ffn(x, norm_w, w_gate, w_up, w_down):
    d, d_ff = w_gate.shape
    pad = (-d_ff) % MM_TILE_N
    wg = jnp.pad(w_gate, ((0, 0), (0, pad))).astype(BF16)
    wu = jnp.pad(w_up, ((0, 0), (0, pad))).astype(BF16)
    wd = jnp.pad(w_down, ((0, pad), (0, 0))).astype(BF16)
    h = _rmsnorm(x, norm_w, NORM_EPS, BF16)
    return _matmul_residual(_ffn_gateup(h, wg, wu), wd, x)


def _ssd_dt_kernel(h_ref, w_ref, wt_ref, bias_ref, biast_ref, alog_ref, alogt_ref,
                   dt_ref, cs_ref, cst_ref, *, n_groups):
    h = h_ref[...]
    lc = h.shape[0]
    nh = w_ref.shape[1]
    r = nh // n_groups
    dt = _softplus(jnp.dot(h, w_ref[...], preferred_element_type=F32) + bias_ref[...])
    la = dt * -jnp.exp(alog_ref[...])
    dtt = _softplus(
        lax.dot_general(wt_ref[...], h, (((1,), (1,)), ((), ())), preferred_element_type=F32) + biast_ref[...])
    lat = dtt * -jnp.exp(alogt_ref[...])
    row = lax.broadcasted_iota(jnp.int32, (lc, lc), 0)
    col = lax.broadcasted_iota(jnp.int32, (lc, lc), 1)
    cs = jnp.dot((row >= col).astype(F32), la, precision=HIGHEST, preferred_element_type=F32)
    cst = jnp.dot(lat, (row <= col).astype(F32), precision=HIGHEST, preferred_element_type=F32)
    for g in range(n_groups):
        dt_ref[0, g] = dt[:, g * r:(g + 1) * r]
        cs_ref[0, g] = cs[:, g * r:(g + 1) * r]
        cst_ref[0, g] = cst[g * r:(g + 1) * r, :]


def _ssd_dt(h, w_dt, dt_bias, a_log, n_groups):
    t, d = h.shape
    nh = w_dt.shape[1]
    r = nh // n_groups
    lc = SSD_CHUNK
    nblk = t // lc
    vec = lambda shape: pl.BlockSpec(shape, lambda i: (0, 0))
    tok = pl.BlockSpec((1, n_groups, lc, r), lambda i: (i, 0, 0, 0))
    head = pl.BlockSpec((1, n_groups, r, lc), lambda i: (i, 0, 0, 0))
    return pl.pallas_call(
        functools.partial(_ssd_dt_kernel, n_groups=n_groups),
        grid=(nblk,),
        in_specs=[pl.BlockSpec((lc, d), lambda i: (i, 0)), vec((d, nh)), vec((nh, d)),
                  vec((1, nh)), vec((nh, 1)), vec((1, nh)), vec((nh, 1))],
        out_specs=[tok, tok, head],
        out_shape=[jax.ShapeDtypeStruct((nblk, n_groups, lc, r), F32),
                   jax.ShapeDtypeStruct((nblk, n_groups, lc, r), F32),
                   jax.ShapeDtypeStruct((nblk, n_groups, r, lc), F32)],
        compiler_params=_params("parallel"),
        name="ssd_dt",
    )(h, w_dt, w_dt.T, dt_bias.reshape(1, nh).astype(F32), dt_bias.reshape(nh, 1).astype(F32),
      a_log.reshape(1, nh).astype(F32), a_log.reshape(nh, 1).astype(F32))


def _conv_kernel(cur_ref, prev_ref, w_ref, b_ref, o_ref, buf_ref, *, halo):
    ts = cur_ref.shape[0]
    kw = w_ref.shape[0]
    prev = prev_ref[...].astype(F32)
    buf_ref[0:halo, :] = jnp.where(pl.program_id(1) == 0, 0.0, prev)
    buf_ref[halo:halo + ts, :] = cur_ref[...].astype(F32)
    acc = b_ref[...] + buf_ref[halo:halo + ts, :] * w_ref[kw - 1:kw, :]
    for j in range(kw - 1):
        off = halo - (kw - 1) + j
        acc = acc + buf_ref[off:off + ts, :] * w_ref[j:j + 1, :]
    o_ref[...] = _silu(acc).astype(o_ref.dtype)


def _ssd_conv(zxbc, conv_w, conv_b, batch, col0):
    t = zxbc.shape[0]
    c, kw = conv_w.shape
    seqlen = t // batch
    halo = V7X_BF16_SUBLANES
    ts = _tile(seqlen, ROW_TILE, halo)
    tc = _tile(c, CONV_COL_TILE)
    nt = seqlen // ts
    cb0 = col0 // tc
    assert col0 % tc == 0 and ts % halo == 0 and kw - 1 <= halo
    return pl.pallas_call(
        functools.partial(_conv_kernel, halo=halo),
        grid=(batch, nt, c // tc),
        in_specs=[
            pl.BlockSpec((ts, tc), lambda b, i, j: (b * nt + i, cb0 + j)),
            pl.BlockSpec((halo, tc),
                         lambda b, i, j: (jnp.maximum((b * nt + i) * (ts // halo) - 1, 0), cb0 + j)),
            pl.BlockSpec((kw, tc), lambda b, i, j: (0, j)),
            pl.BlockSpec((1, tc), lambda b, i, j: (0, j)),
        ],
        out_specs=pl.BlockSpec((ts, tc), lambda b, i, j: (b * nt + i, j)),
        out_shape=jax.ShapeDtypeStruct((t, c), BF16),
        scratch_shapes=[pltpu.VMEM((ts + halo, tc), F32)],
        compiler_params=_params("parallel", "parallel", "parallel"),
        name="ssd_conv",
    )(zxbc, zxbc, conv_w.T.astype(F32), conv_b.reshape(1, c).astype(F32))


def _ssd_scan_kernel(x_ref, b_ref, c_ref, z_ref, dt_ref, cs_ref, cst_ref, dskip_ref, nw_ref,
                     o_ref, state_ref, *, heads, head_dim, eps):
    @pl.when(pl.program_id(2) == 0)
    def _():
        state_ref[...] = jnp.zeros_like(state_ref)

    lc = x_ref.shape[0]
    gw = heads * head_dim
    pair = 2 * head_dim
    x = x_ref[...].astype(F32)
    bm = b_ref[...]
    cm = c_ref[...]
    dt = dt_ref[0, 0]
    cs = cs_ref[0, 0]
    cst = cst_ref[0, 0]

    e_row = lax.broadcasted_iota(jnp.int32, (heads, gw), 0)
    e_col = lax.broadcasted_iota(jnp.int32, (heads, gw), 1) // head_dim
    expand_m = (e_row == e_col).astype(F32)
    expand = lambda a: jnp.dot(a, expand_m, precision=HIGHEST, preferred_element_type=F32)

    cs_end = cs[lc - 1:lc, :]
    dt_e = expand(dt)
    ecs_e = expand(jnp.exp(cs))
    dtd_e = expand(dt * jnp.exp(cs_end - cs))

    xdt = (x * dt_e).astype(BF16)
    cb = lax.dot_general(cm, bm, (((1,), (1,)), ((), ())), preferred_element_type=F32)
    causal = lax.broadcasted_iota(jnp.int32, (lc, lc), 0) >= lax.broadcasted_iota(jnp.int32, (lc, lc), 1)
    first_half = lax.broadcasted_iota(jnp.int32, (lc, pair), 1) < head_dim

    def masked_cb(r):
        seg = cs[:, r:r + 1] - cst[r:r + 1, :]
        return (cb * jnp.exp(jnp.where(causal, seg, -jnp.inf))).astype(BF16)

    ys = []
    for p in range(heads // 2):
        xp = xdt[:, p * pair:(p + 1) * pair]
        xa = jnp.where(first_half, xp, jnp.zeros_like(xp))
        xb = jnp.where(first_half, jnp.zeros_like(xp), xp)
        ys.append(jnp.dot(masked_cb(2 * p), xa, preferred_element_type=F32)
                  + jnp.dot(masked_cb(2 * p + 1), xb, preferred_element_type=F32))
    y = jnp.concatenate(ys, axis=1) if len(ys) > 1 else ys[0]

    state = state_ref[...]
    y = y + jnp.dot(cm, state.astype(BF16), preferred_element_type=F32) * ecs_e
    state_ref[...] = state * ecs_e[lc - 1:lc, :] + lax.dot_general(
        bm, (x * dtd_e).astype(BF16), (((0,), (0,)), ((), ())), preferred_element_type=F32)

    y = y + x * dskip_ref[...]
    y = y * _silu(z_ref[...].astype(F32))
    ms = jnp.mean(y * y, axis=-1, keepdims=True)
    o_ref[...] = (y * lax.rsqrt(ms + eps) * nw_ref[...]).astype(o_ref.dtype)


def _ssd_scan(zxbc, xbc, dt, cs, cst, d_skip, norm_w, batch, d_inner, n_groups, heads):
    t = zxbc.shape[0]
    lc = SSD_CHUNK
    nc = t // batch // lc
    n = SSD_D_STATE
    gw = d_inner // n_groups
    head_dim = gw // heads
    assert heads % 2 == 0 and 2 * head_dim == V7X_LANES
    row = lambda b, g, c: b * nc + c
    small = lambda shape: pl.BlockSpec(shape, lambda b, g, c: (row(b, g, c), g, 0, 0))
    return pl.pallas_call(
        functools.partial(_ssd_scan_kernel, heads=heads, head_dim=head_dim, eps=INNER_NORM_EPS),
        grid=(batch, n_groups, nc),
        in_specs=[
            pl.BlockSpec((lc, gw), lambda b, g, c: (row(b, g, c), g)),
            pl.BlockSpec((lc, n), lambda b, g, c: (row(b, g, c), d_inner // n + g)),
            pl.BlockSpec((lc, n), lambda b, g, c: (row(b, g, c), d_inner // n + n_groups + g)),
            pl.BlockSpec((lc, gw), lambda b, g, c: (row(b, g, c), g)),
            small((1, 1, lc, heads)), small((1, 1, lc, heads)), small((1, 1, heads, lc)),
            pl.BlockSpec((1, gw), lambda b, g, c: (0, g)),
            pl.BlockSpec((1, gw), lambda b, g, c: (0, g)),
        ],
        out_specs=pl.BlockSpec((lc, gw), lambda b, g, c: (row(b, g, c), g)),
        out_shape=jax.ShapeDtypeStruct((t, d_inner), BF16),
        scratch_shapes=[pltpu.VMEM((n, gw), F32)],
        compiler_params=_params("parallel", "parallel", "arbitrary"),
        name="ssd_scan",
    )(xbc, xbc, xbc, zxbc, dt, cs, cst,
      jnp.repeat(d_skip.astype(F32), head_dim).reshape(1, d_inner),
      norm_w.reshape(1, d_inner).astype(F32))


def _ssd_mixer(x, batch, norm_w, in_proj, conv_w, conv_b, dt_bias, a_log, d_skip, inner_norm_w, out_proj):
    d_inner = out_proj.shape[0]
    conv_dim = conv_w.shape[0]
    n_heads = dt_bias.shape[0]
    n_groups = (conv_dim - d_inner) // 2 // SSD_D_STATE
    main = d_inner + conv_dim
    h = _rmsnorm(x, norm_w, NORM_EPS, BF16)
    zxbc = _matmul(h, in_proj[:, :main].astype(BF16), BF16)
    dt, cs, cst = _ssd_dt(h, in_proj[:, main:].astype(BF16), dt_bias, a_log, n_groups)
    xbc = _ssd_conv(zxbc, conv_w, conv_b, batch, d_inner)
    y = _ssd_scan(zxbc, xbc, dt, cs, cst, d_skip, inner_norm_w, batch, d_inner, n_groups,
                  n_heads // n_groups)
    return _matmul_residual(y, out_proj.astype(BF16), x)


def _gla_gate_kernel(h_ref, wlow_ref, up_ref, bias_ref, o_ref):
    low = jnp.dot(h_ref[...], wlow_ref[...], preferred_element_type=F32)
    x = jnp.dot(low, up_ref[...], precision=HIGHEST, preferred_element_type=F32) + bias_ref[...]
    o_ref[...] = (jnp.minimum(x, 0.0) - jnp.log1p(jnp.exp(-jnp.abs(x)))) * (1.0 / GLA_GATE_NORMALIZER)


def _gla_gate(h, w_low, gk_up, gk_bias):
    t, d = h.shape
    rank, kd = gk_up.shape
    pad = (-rank) % V7X_LANES
    w_low = jnp.pad(w_low, ((0, 0), (0, pad))).astype(BF16)
    gk_up = jnp.pad(gk_up, ((0, pad), (0, 0))).astype(F32)
    tr = _tile(t, 2 * ROW_TILE, 8)
    return pl.pallas_call(
        _gla_gate_kernel,
        grid=(t // tr,),
        in_specs=[pl.BlockSpec((tr, d), lambda i: (i, 0)),
                  pl.BlockSpec((d, rank + pad), lambda i: (0, 0)),
                  pl.BlockSpec((rank + pad, kd), lambda i: (0, 0)),
                  pl.BlockSpec((1, kd), lambda i: (0, 0))],
        out_specs=pl.BlockSpec((tr, kd), lambda i: (i, 0)),
        out_shape=jax.ShapeDtypeStruct((t, kd), F32),
        compiler_params=_params("parallel"),
        name="gla_gate",
    )(h, w_low, gk_up, gk_bias.reshape(1, kd).astype(F32))


def _gla_scan_kernel(q_ref, k_ref, v_ref, g_ref, go_ref, nw_ref, o_ref, state_ref, *, chunk, scale, eps):
    @pl.when(pl.program_id(2) == 0)
    def _():
        state_ref[...] = jnp.zeros_like(state_ref)

    rows = q_ref.shape[0]
    causal = lax.broadcasted_iota(jnp.int32, (chunk, chunk), 0) >= lax.broadcasted_iota(jnp.int32, (chunk, chunk), 1)
    tril = causal.astype(F32)
    for s in range(rows // chunk):
        sl = pl.ds(s * chunk, chunk)
        q = q_ref[sl, :].astype(F32) * scale
        k = k_ref[sl, :].astype(F32)
        v = v_ref[sl, :]
        g = g_ref[sl, :]
        cs = jnp.dot(tril, g, precision=HIGHEST, preferred_element_type=F32)
        q_dec = (q * jnp.exp(cs)).astype(BF16)
        k_inv = (k * jnp.exp(-cs)).astype(BF16)
        scores = lax.dot_general(q_dec, k_inv, (((1,), (1,)), ((), ())), preferred_element_type=F32)
        scores = jnp.where(causal, scores, 0.0).astype(BF16)
        state = state_ref[...]
        o = (jnp.dot(scores, v, preferred_element_type=F32)
             + jnp.dot(q_dec, state.astype(BF16), preferred_element_type=F32))
        k_end = (k * jnp.exp(cs[chunk - 1:chunk, :] - cs)).astype(BF16)
        decay_end = jnp.exp(jnp.sum(g.T, axis=1, keepdims=True))
        state_ref[...] = state * decay_end + lax.dot_general(
            k_end, v, (((0,), (0,)), ((), ())), preferred_element_type=F32)
        ms = jnp.mean(o * o, axis=-1, keepdims=True)
        o = o * lax.rsqrt(ms + eps) * nw_ref[...]
        o_ref[sl, :] = (o * _silu(go_ref[sl, :].astype(F32))).astype(o_ref.dtype)


def _gla_scan(qkvg, log_g, norm_w, batch, key_dim, value_dim, n_heads):
    t = qkvg.shape[0]
    hk, hv = key_dim // n_heads, value_dim // n_heads
    seqlen = t // batch
    rb = _tile(seqlen, GLA_ROW_BLOCK, GLA_CHUNK)
    nb = seqlen // rb
    assert rb % GLA_CHUNK == 0
    row = lambda b, h, i: b * nb + i
    assert (2 * key_dim) % hv == 0
    v0 = 2 * key_dim // hv
    return pl.pallas_call(
        functools.partial(_gla_scan_kernel, chunk=GLA_CHUNK, scale=hk ** -0.5, eps=INNER_NORM_EPS),
        grid=(batch, n_heads, nb),
        in_specs=[
            pl.BlockSpec((rb, hk), lambda b, h, i: (row(b, h, i), h)),
            pl.BlockSpec((rb, hk), lambda b, h, i: (row(b, h, i), n_heads + h)),
            pl.BlockSpec((rb, hv), lambda b, h, i: (row(b, h, i), v0 + h)),
            pl.BlockSpec((rb, hk), lambda b, h, i: (row(b, h, i), h)),
            pl.BlockSpec((rb, hv), lambda b, h, i: (row(b, h, i), v0 + n_heads + h)),
            pl.BlockSpec((1, hv), lambda b, h, i: (0, 0)),
        ],
        out_specs=pl.BlockSpec((rb, hv), lambda b, h, i: (row(b, h, i), h)),
        out_shape=jax.ShapeDtypeStruct((t, value_dim), BF16),
        scratch_shapes=[pltpu.VMEM((hk, hv), F32)],
        compiler_params=_params("parallel", "parallel", "arbitrary"),
        name="gla_scan",
    )(qkvg, qkvg, qkvg, log_g, qkvg, norm_w.reshape(1, hv).astype(F32))


def _gla_mixer(x, batch, norm_w, in_proj, gk_up, gk_bias, inner_norm_w, out_proj):
    key_dim = gk_up.shape[1]
    value_dim = out_proj.shape[0]
    n_heads = value_dim // inner_norm_w.shape[0]
    main = 2 * key_dim + 2 * value_dim
    h = _rmsnorm(x, norm_w, NORM_EPS, BF16)
    qkvg = _matmul(h, in_proj[:, :main].astype(BF16), BF16)
    log_g = _gla_gate(h, in_proj[:, main:], gk_up, gk_bias)
    o = _gla_scan(qkvg, log_g, inner_norm_w, batch, key_dim, value_dim, n_heads)
    return _matmul_residual(o, out_proj.astype(BF16), x)


def kernel(x, l0_mixer_norm, l0_ssd_in_proj, l0_ssd_conv_w, l0_ssd_conv_b, l0_ssd_dt_bias, l0_ssd_a_log, l0_ssd_d, l0_ssd_norm_w, l0_ssd_out_proj, l0_ffn_norm, l0_ffn_w_gate, l0_ffn_w_up, l0_ffn_w_down, l1_mixer_norm, l1_gla_in_proj, l1_gla_gk_up, l1_gla_gk_bias, l1_gla_norm_w, l1_gla_out_proj, l1_ffn_norm, l1_ffn_w_gate, l1_ffn_w_up, l1_ffn_w_down, final_norm):
    batch, seqlen, d = x.shape
    xf = x.reshape(batch * seqlen, d).astype(F32)
    xf = _ssd_mixer(xf, batch, l0_mixer_norm, l0_ssd_in_proj, l0_ssd_conv_w, l0_ssd_conv_b,
                    l0_ssd_dt_bias, l0_ssd_a_log, l0_ssd_d, l0_ssd_norm_w, l0_ssd_out_proj)
    xf = _ffn(xf, l0_ffn_norm, l0_ffn_w_gate, l0_ffn_w_up, l0_ffn_w_down)
    xf = _gla_mixer(xf, batch, l1_mixer_norm, l1_gla_in_proj, l1_gla_gk_up, l1_gla_gk_bias,
                    l1_gla_norm_w, l1_gla_out_proj)
    xf = _ffn(xf, l1_ffn_norm, l1_ffn_w_gate, l1_ffn_w_up, l1_ffn_w_down)
    return _rmsnorm(xf, final_norm, NORM_EPS, x.dtype).reshape(batch, seqlen, d)
```

```python
import functools

import jax
import jax.numpy as jnp
from jax import lax
from jax.experimental import pallas as pl
from jax.experimental.pallas import tpu as pltpu

F32 = jnp.float32
BF16 = jnp.bfloat16
HIGHEST = lax.Precision.HIGHEST

NORM_EPS = 1e-6
INNER_NORM_EPS = 1e-5
SSD_D_STATE = 128
SSD_CHUNK = 256
GLA_CHUNK = 64
GLA_GATE_NORMALIZER = 16.0

V7X_LANES = 128
V7X_BF16_SUBLANES = 16
V7X_VMEM_BYTES = 64 * 1024 * 1024
VMEM_LIMIT_BYTES = V7X_VMEM_BYTES - 8 * 1024 * 1024

MM_TILE_M = 1024
MM_TILE_N = 1024
MM_MAX_TILE_K = 3072
GATEUP_TILE_N = 512
ROW_TILE = 256
CONV_COL_TILE = 1024
GLA_ROW_BLOCK = 256


def _params(*semantics):
    return pltpu.CompilerParams(dimension_semantics=semantics, vmem_limit_bytes=VMEM_LIMIT_BYTES)


def _tile(dim, pref, align=V7X_LANES):
    if dim <= pref:
        return dim
    t = (pref // align) * align
    while t >= align:
        if dim % t == 0:
            return t
        t -= align
    return dim


def _silu(x):
    return x * jax.nn.sigmoid(x)


def _softplus(x):
    return jnp.maximum(x, 0.0) + jnp.log1p(jnp.exp(-jnp.abs(x)))


def _rmsnorm_kernel(x_ref, w_ref, o_ref, *, eps):
    x = x_ref[...]
    ms = jnp.mean(x * x, axis=-1, keepdims=True)
    o_ref[...] = (x * lax.rsqrt(ms + eps) * w_ref[...]).astype(o_ref.dtype)


def _rmsnorm(x, w, eps, out_dtype):
    t, d = x.shape
    tr = _tile(t, ROW_TILE, 8)
    return pl.pallas_call(
        functools.partial(_rmsnorm_kernel, eps=eps),
        grid=(t // tr,),
        in_specs=[pl.BlockSpec((tr, d), lambda i: (i, 0)), pl.BlockSpec((1, d), lambda i: (0, 0))],
        out_specs=pl.BlockSpec((tr, d), lambda i: (i, 0)),
        out_shape=jax.ShapeDtypeStruct((t, d), out_dtype),
        compiler_params=_params("parallel"),
        name="rmsnorm",
    )(x, w.reshape(1, d).astype(F32))


def _mm_kernel(a_ref, w_ref, o_ref):
    o_ref[...] = jnp.dot(a_ref[...], w_ref[...], preferred_element_type=F32).astype(o_ref.dtype)


def _matmul(a, w, out_dtype):
    m, k = a.shape
    n = w.shape[1]
    tm, tn = _tile(m, MM_TILE_M), _tile(n, MM_TILE_N)
    return pl.pallas_call(
        _mm_kernel,
        grid=(m // tm, n // tn),
        in_specs=[pl.BlockSpec((tm, k), lambda i, j: (i, 0)), pl.BlockSpec((k, tn), lambda i, j: (0, j))],
        out_specs=pl.BlockSpec((tm, tn), lambda i, j: (i, j)),
        out_shape=jax.ShapeDtypeStruct((m, n), out_dtype),
        compiler_params=_params("parallel", "parallel"),
        name="matmul",
    )(a, w)


def _mm_res_kernel(a_ref, w_ref, r_ref, o_ref):
    @pl.when(pl.program_id(2) == 0)
    def _():
        o_ref[...] = r_ref[...]

    o_ref[...] += jnp.dot(a_ref[...], w_ref[...], preferred_element_type=F32)


def _matmul_residual(a, w, res):
    m, k = a.shape
    n = w.shape[1]
    tm, tn, tk = _tile(m, MM_TILE_M), _tile(n, MM_TILE_N), _tile(k, MM_MAX_TILE_K)
    return pl.pallas_call(
        _mm_res_kernel,
        grid=(m // tm, n // tn, k // tk),
        in_specs=[
            pl.BlockSpec((tm, tk), lambda i, j, kk: (i, kk)),
            pl.BlockSpec((tk, tn), lambda i, j, kk: (kk, j)),
            pl.BlockSpec((tm, tn), lambda i, j, kk: (i, j)),
        ],
        out_specs=pl.BlockSpec((tm, tn), lambda i, j, kk: (i, j)),
        out_shape=jax.ShapeDtypeStruct((m, n), F32),
        compiler_params=_params("parallel", "parallel", "arbitrary"),
        name="matmul_residual",
    )(a, w, res)


def _gateup_kernel(h_ref, wg_ref, wu_ref, o_ref):
    h = h_ref[...]
    g = jnp.dot(h, wg_ref[...], preferred_element_type=F32)
    u = jnp.dot(h, wu_ref[...], preferred_element_type=F32)
    o_ref[...] = (_silu(g) * u).astype(o_ref.dtype)


def _ffn_gateup(h, wg, wu):
    m, k = h.shape
    n = wg.shape[1]
    tm, tn = _tile(m, MM_TILE_M), _tile(n, GATEUP_TILE_N)
    return pl.pallas_call(
        _gateup_kernel,
        grid=(m // tm, n // tn),
        in_specs=[
            pl.BlockSpec((tm, k), lambda i, j: (i, 0)),
            pl.BlockSpec((k, tn), lambda i, j: (0, j)),
            pl.BlockSpec((k, tn), lambda i, j: (0, j)),
        ],
        out_specs=pl.BlockSpec((tm, tn), lambda i, j: (i, j)),
        out_shape=jax.ShapeDtypeStruct((m, n), BF16),
        compiler_params=_params("parallel", "parallel"),
        name="ffn_gateup",
    )(h, wg, wu)


def _ffn(x, norm_w, w_gate, w_up, w_down):
    d, d_ff = w_gate.shape
    pad = (-d_ff) % MM_TILE_N
    wg = jnp.pad(w_gate, ((0, 0), (0, pad))).astype(BF16)
    wu = jnp.pad(w_up, ((0, 0), (0, pad))).astype(BF16)
    wd = jnp.pad(w_down, ((0, pad), (0, 0))).astype(BF16)
    h = _rmsnorm(x, norm_w, NORM_EPS, BF16)
    return _matmul_residual(_ffn_gateup(h, wg, wu), wd, x)


def _ssd_dt_kernel(h_ref, w_ref, wt_ref, bias_ref, biast_ref, alog_ref, alogt_ref,
                   dt_ref, cs_ref, cst_ref, *, n_groups):
    h = h_ref[...]
    lc = h.shape[0]
    nh = w_ref.shape[1]
    r = nh // n_groups
    dt = _softplus(jnp.dot(h, w_ref[...], preferred_element_type=F32) + bias_ref[...])
    la = dt * -jnp.exp(alog_ref[...])
    dtt = _softplus(
        lax.dot_general(wt_ref[...], h, (((1,), (1,)), ((), ())), preferred_element_type=F32) + biast_ref[...])
    lat = dtt * -jnp.exp(alogt_ref[...])
    row = lax.broadcasted_iota(jnp.int32, (lc, lc), 0)
    col = lax.broadcasted_iota(jnp.int32, (lc, lc), 1)
    cs = jnp.dot((row >= col).astype(F32), la, precision=HIGHEST, preferred_element_type=F32)
    cst = jnp.dot(lat, (row <= col).astype(F32), precision=HIGHEST, preferred_element_type=F32)
    for g in range(n_groups):
        dt_ref[0, g] = dt[:, g * r:(g + 1) * r]
        cs_ref[0, g] = cs[:, g * r:(g + 1) * r]
        cst_ref[0, g] = cst[g * r:(g + 1) * r, :]


def _ssd_dt(h, w_dt, dt_bias, a_log, n_groups):
    t, d = h.shape
    nh = w_dt.shape[1]
    r = nh // n_groups
    lc = SSD_CHUNK
    nblk = t // lc
    vec = lambda shape: pl.BlockSpec(shape, lambda i: (0, 0))
    tok = pl.BlockSpec((1, n_groups, lc, r), lambda i: (i, 0, 0, 0))
    head = pl.BlockSpec((1, n_groups, r, lc), lambda i: (i, 0, 0, 0))
    return pl.pallas_call(
        functools.partial(_ssd_dt_kernel, n_groups=n_groups),
        grid=(nblk,),
        in_specs=[pl.BlockSpec((lc, d), lambda i: (i, 0)), vec((d, nh)), vec((nh, d)),
                  vec((1, nh)), vec((nh, 1)), vec((1, nh)), vec((nh, 1))],
        out_specs=[tok, tok, head],
        out_shape=[jax.ShapeDtypeStruct((nblk, n_groups, lc, r), F32),
                   jax.ShapeDtypeStruct((nblk, n_groups, lc, r), F32),
                   jax.ShapeDtypeStruct((nblk, n_groups, r, lc), F32)],
        compiler_params=_params("parallel"),
        name="ssd_dt",
    )(h, w_dt, w_dt.T, dt_bias.reshape(1, nh).astype(F32), dt_bias.reshape(nh, 1).astype(F32),
      a_log.reshape(1, nh).astype(F32), a_log.reshape(nh, 1).astype(F32))


def _conv_kernel(cur_ref, prev_ref, w_ref, b_ref, o_ref, buf_ref, *, halo):
    ts = cur_ref.shape[0]
    kw = w_ref.shape[0]
    prev = prev_ref[...].astype(F32)
    buf_ref[0:halo, :] = jnp.where(pl.program_id(1) == 0, 0.0, prev)
    buf_ref[halo:halo + ts, :] = cur_ref[...].astype(F32)
    acc = b_ref[...] + buf_ref[halo:halo + ts, :] * w_ref[kw - 1:kw, :]
    for j in range(kw - 1):
        off = halo - (kw - 1) + j
        acc = acc + buf_ref[off:off + ts, :] * w_ref[j:j + 1, :]
    o_ref[...] = _silu(acc).astype(o_ref.dtype)


def _ssd_conv(zxbc, conv_w, conv_b, batch, col0):
    t = zxbc.shape[0]
    c, kw = conv_w.shape
    seqlen = t // batch
    halo = V7X_BF16_SUBLANES
    ts = _tile(seqlen, ROW_TILE, halo)
    tc = _tile(c, CONV_COL_TILE)
    nt = seqlen // ts
    cb0 = col0 // tc
    assert col0 % tc == 0 and ts % halo == 0 and kw - 1 <= halo
    return pl.pallas_call(
        functools.partial(_conv_kernel, halo=halo),
        grid=(batch, nt, c // tc),
        in_specs=[
            pl.BlockSpec((ts, tc), lambda b, i, j: (b * nt + i, cb0 + j)),
            pl.BlockSpec((halo, tc),
                         lambda b, i, j: (jnp.maximum((b * nt + i) * (ts // halo) - 1, 0), cb0 + j)),
            pl.BlockSpec((kw, tc), lambda b, i, j: (0, j)),
            pl.BlockSpec((1, tc), lambda b, i, j: (0, j)),
        ],
        out_specs=pl.BlockSpec((ts, tc), lambda b, i, j: (b * nt + i, j)),
        out_shape=jax.ShapeDtypeStruct((t, c), BF16),
        scratch_shapes=[pltpu.VMEM((ts + halo, tc), F32)],
        compiler_params=_params("parallel", "parallel", "parallel"),
        name="ssd_conv",
    )(zxbc, zxbc, conv_w.T.astype(F32), conv_b.reshape(1, c).astype(F32))


def _ssd_scan_kernel(x_ref, b_ref, c_ref, z_ref, dt_ref, cs_ref, cst_ref, dskip_ref, nw_ref,
                     o_ref, state_ref, *, heads, head_dim, eps):
    @pl.when(pl.program_id(2) == 0)
    def _():
        state_ref[...] = jnp.zeros_like(state_ref)

    lc = x_ref.shape[0]
    gw = heads * head_dim
    pair = 2 * head_dim
    x = x_ref[...].astype(F32)
    bm = b_ref[...]
    cm = c_ref[...]
    dt = dt_ref[0, 0]
    cs = cs_ref[0, 0]
    cst = cst_ref[0, 0]

    e_row = lax.broadcasted_iota(jnp.int32, (3 * heads, gw), 0) % heads
    e_col = lax.broadcasted_iota(jnp.int32, (3 * heads, gw), 1) // head_dim
    expand_m = (e_row == e_col).astype(BF16)

    def expand(a):
        hi = a.astype(BF16).astype(F32)
        mid = (a - hi).astype(BF16).astype(F32)
        lo = a - hi - mid
        pieces = jnp.concatenate([hi, mid, lo], axis=1).astype(BF16)
        return jnp.dot(pieces, expand_m, preferred_element_type=F32)

    cs_end = cs[lc - 1:lc, :]
    dt_e = expand(dt)
    ecs_e = expand(jnp.exp(cs))
    dtd_e = expand(dt * jnp.exp(cs_end - cs))

    xdt = (x * dt_e).astype(BF16)
    cb = lax.dot_general(cm, bm, (((1,), (1,)), ((), ())), preferred_element_type=F32)
    causal = lax.broadcasted_iota(jnp.int32, (lc, lc), 0) >= lax.broadcasted_iota(jnp.int32, (lc, lc), 1)
    first_half = lax.broadcasted_iota(jnp.int32, (lc, pair), 1) < head_dim

    def masked_cb(r):
        seg = cs[:, r:r + 1] - cst[r:r + 1, :]
        return (cb * jnp.exp(jnp.where(causal, seg, -jnp.inf))).astype(BF16)

    ys = []
    for p in range(heads // 2):
        xp = xdt[:, p * pair:(p + 1) * pair]
        xa = jnp.where(first_half, xp, jnp.zeros_like(xp))
        xb = jnp.where(first_half, jnp.zeros_like(xp), xp)
        ys.append(jnp.dot(masked_cb(2 * p), xa, preferred_element_type=F32)
                  + jnp.dot(masked_cb(2 * p + 1), xb, preferred_element_type=F32))
    y = jnp.concatenate(ys, axis=1) if len(ys) > 1 else ys[0]

    state = state_ref[...]
    y = y + jnp.dot(cm, state.astype(BF16), preferred_element_type=F32) * ecs_e
    state_ref[...] = state * ecs_e[lc - 1:lc, :] + lax.dot_general(
        bm, (x * dtd_e).astype(BF16), (((0,), (0,)), ((), ())), preferred_element_type=F32)

    y = y + x * dskip_ref[...]
    y = y * _silu(z_ref[...].astype(F32))
    ms = jnp.mean(y * y, axis=-1, keepdims=True)
    o_ref[...] = (y * lax.rsqrt(ms + eps) * nw_ref[...]).astype(o_ref.dtype)


def _ssd_scan(zxbc, xbc, dt, cs, cst, d_skip, norm_w, batch, d_inner, n_groups, heads):
    t = zxbc.shape[0]
    lc = SSD_CHUNK
    nc = t // batch // lc
    n = SSD_D_STATE
    gw = d_inner // n_groups
    head_dim = gw // heads
    assert heads % 2 == 0 and 2 * head_dim == V7X_LANES
    row = lambda b, g, c: b * nc + c
    small = lambda shape: pl.BlockSpec(shape, lambda b, g, c: (row(b, g, c), g, 0, 0))
    return pl.pallas_call(
        functools.partial(_ssd_scan_kernel, heads=heads, head_dim=head_dim, eps=INNER_NORM_EPS),
        grid=(batch, n_groups, nc),
        in_specs=[
            pl.BlockSpec((lc, gw), lambda b, g, c: (row(b, g, c), g)),
            pl.BlockSpec((lc, n), lambda b, g, c: (row(b, g, c), d_inner // n + g)),
            pl.BlockSpec((lc, n), lambda b, g, c: (row(b, g, c), d_inner // n + n_groups + g)),
            pl.BlockSpec((lc, gw), lambda b, g, c: (row(b, g, c), g)),
            small((1, 1, lc, heads)), small((1, 1, lc, heads)), small((1, 1, heads, lc)),
            pl.BlockSpec((1, gw), lambda b, g, c: (0, g)),
            pl.BlockSpec((1, gw), lambda b, g, c: (0, g)),
        ],
        out_specs=pl.BlockSpec((lc, gw), lambda b, g, c: (row(b, g, c), g)),
        out_shape=jax.ShapeDtypeStruct((t, d_inner), BF16),
        scratch_shapes=[pltpu.VMEM((n, gw), F32)],
        compiler_params=_params("parallel", "parallel", "arbitrary"),
        name="ssd_scan",
    )(xbc, xbc, xbc, zxbc, dt, cs, cst,
      jnp.repeat(d_skip.astype(F32), head_dim).reshape(1, d_inner),
      norm_w.reshape(1, d_inner).astype(F32))


def _ssd_mixer(x, batch, norm_w, in_proj, conv_w, conv_b, dt_bias, a_log, d_skip, inner_norm_w, out_proj):
    d_inner = out_proj.shape[0]
    conv_dim = conv_w.shape[0]
    n_heads = dt_bias.shape[0]
    n_groups = (conv_dim - d_inner) // 2 // SSD_D_STATE
    main = d_inner + conv_dim
    h = _rmsnorm(x, norm_w, NORM_EPS, BF16)
    zxbc = _matmul(h, in_proj[:, :main].astype(BF16), BF16)
    dt, cs, cst = _ssd_dt(h, in_proj[:, main:].astype(BF16), dt_bias, a_log, n_groups)
    xbc = _ssd_conv(zxbc, conv_w, conv_b, batch, d_inner)
    y = _ssd_scan(zxbc, xbc, dt, cs, cst, d_skip, inner_norm_w, batch, d_inner, n_groups,
                  n_heads // n_groups)
    return _matmul_residual(y, out_proj.astype(BF16), x)


def _gla_gate_kernel(h_ref, wlow_ref, up_ref, bias_ref, o_ref):
    low = jnp.dot(h_ref[...], wlow_ref[...], preferred_element_type=F32)
    x = jnp.dot(low, up_ref[...], precision=HIGHEST, preferred_element_type=F32) + bias_ref[...]
    o_ref[...] = (jnp.minimum(x, 0.0) - jnp.log1p(jnp.exp(-jnp.abs(x)))) * (1.0 / GLA_GATE_NORMALIZER)


def _gla_gate(h, w_low, gk_up, gk_bias):
    t, d = h.shape
    rank, kd = gk_up.shape
    pad = (-rank) % V7X_LANES
    w_low = jnp.pad(w_low, ((0, 0), (0, pad))).astype(BF16)
    gk_up = jnp.pad(gk_up, ((0, pad), (0, 0))).astype(F32)
    tr = _tile(t, 2 * ROW_TILE, 8)
    return pl.pallas_call(
        _gla_gate_kernel,
        grid=(t // tr,),
        in_specs=[pl.BlockSpec((tr, d), lambda i: (i, 0)),
                  pl.BlockSpec((d, rank + pad), lambda i: (0, 0)),
                  pl.BlockSpec((rank + pad, kd), lambda i: (0, 0)),
                  pl.BlockSpec((1, kd), lambda i: (0, 0))],
        out_specs=pl.BlockSpec((tr, kd), lambda i: (i, 0)),
        out_shape=jax.ShapeDtypeStruct((t, kd), F32),
        compiler_params=_params("parallel"),
        name="gla_gate",
    )(h, w_low, gk_up, gk_bias.reshape(1, kd).astype(F32))


def _gla_scan_kernel(q_ref, k_ref, v_ref, g_ref, go_ref, nw_ref, o_ref, state_ref, *, chunk, scale, eps):
    @pl.when(pl.program_id(2) == 0)
    def _():
        state_ref[...] = jnp.zeros_like(state_ref)

    rows = q_ref.shape[0]
    causal = lax.broadcasted_iota(jnp.int32, (chunk, chunk), 0) >= lax.broadcasted_iota(jnp.int32, (chunk, chunk), 1)
    tril = causal.astype(F32)
    for s in range(rows // chunk):
        sl = pl.ds(s * chunk, chunk)
        q = q_ref[sl, :].astype(F32) * scale
        k = k_ref[sl, :].astype(F32)
        v = v_ref[sl, :]
        g = g_ref[sl, :]
        cs = jnp.dot(tril, g, precision=HIGHEST, preferred_element_type=F32)
        q_dec = (q * jnp.exp(cs)).astype(BF16)
        k_inv = (k * jnp.exp(-cs)).astype(BF16)
        scores = lax.dot_general(q_dec, k_inv, (((1,), (1,)), ((), ())), preferred_element_type=F32)
        scores = jnp.where(causal, scores, 0.0).astype(BF16)
        state = state_ref[...]
        o = (jnp.dot(scores, v, preferred_element_type=F32)
             + jnp.dot(q_dec, state.astype(BF16), preferred_element_type=F32))
        k_end = (k * jnp.exp(cs[chunk - 1:chunk, :] - cs)).astype(BF16)
        decay_end = jnp.exp(jnp.sum(g.T, axis=1, keepdims=True))
        state_ref[...] = state * decay_end + lax.dot_general(
            k_end, v, (((0,), (0,)), ((), ())), preferred_element_type=F32)
        ms = jnp.mean(o * o, axis=-1, keepdims=True)
        o = o * lax.rsqrt(ms + eps) * nw_ref[...]
        o_ref[sl, :] = (o * _silu(go_ref[sl, :].astype(F32))).astype(o_ref.dtype)


def _gla_scan(qkvg, log_g, norm_w, batch, key_dim, value_dim, n_heads):
    t = qkvg.shape[0]
    hk, hv = key_dim // n_heads, value_dim // n_heads
    seqlen = t // batch
    rb = _tile(seqlen, GLA_ROW_BLOCK, GLA_CHUNK)
    nb = seqlen // rb
    assert rb % GLA_CHUNK == 0
    row = lambda b, h, i: b * nb + i
    assert (2 * key_dim) % hv == 0
    v0 = 2 * key_dim // hv
    return pl.pallas_call(
        functools.partial(_gla_scan_kernel, chunk=GLA_CHUNK, scale=hk ** -0.5, eps=INNER_NORM_EPS),
        grid=(batch, n_heads, nb),
        in_specs=[
            pl.BlockSpec((rb, hk), lambda b, h, i: (row(b, h, i), h)),
            pl.BlockSpec((rb, hk), lambda b, h, i: (row(b, h, i), n_heads + h)),
            pl.BlockSpec((rb, hv), lambda b, h, i: (row(b, h, i), v0 + h)),
            pl.BlockSpec((rb, hk), lambda b, h, i: (row(b, h, i), h)),
            pl.BlockSpec((rb, hv), lambda b, h, i: (row(b, h, i), v0 + n_heads + h)),
            pl.BlockSpec((1, hv), lambda b, h, i: (0, 0)),
        ],
        out_specs=pl.BlockSpec((rb, hv), lambda b, h, i: (row(b, h, i), h)),
        out_shape=jax.ShapeDtypeStruct((t, value_dim), BF16),
        scratch_shapes=[pltpu.VMEM((hk, hv), F32)],
        compiler_params=_params("parallel", "parallel", "arbitrary"),
        name="gla_scan",
    )(qkvg, qkvg, qkvg, log_g, qkvg, norm_w.reshape(1, hv).astype(F32))


def _gla_mixer(x, batch, norm_w, in_proj, gk_up, gk_bias, inner_norm_w, out_proj):
    key_dim = gk_up.shape[1]
    value_dim = out_proj.shape[0]
    n_heads = value_dim // inner_norm_w.shape[0]
    main = 2 * key_dim + 2 * value_dim
    h = _rmsnorm(x, norm_w, NORM_EPS, BF16)
    qkvg = _matmul(h, in_proj[:, :main].astype(BF16), BF16)
    log_g = _gla_gate(h, in_proj[:, main:], gk_up, gk_bias)
    o = _gla_scan(qkvg, log_g, inner_norm_w, batch, key_dim, value_dim, n_heads)
    return _matmul_residual(o, out_proj.astype(BF16), x)


def kernel(x, l0_mixer_norm, l0_ssd_in_proj, l0_ssd_conv_w, l0_ssd_conv_b, l0_ssd_dt_bias, l0_ssd_a_log, l0_ssd_d, l0_ssd_norm_w, l0_ssd_out_proj, l0_ffn_norm, l0_ffn_w_gate, l0_ffn_w_up, l0_ffn_w_down, l1_mixer_norm, l1_gla_in_proj, l1_gla_gk_up, l1_gla_gk_bias, l1_gla_norm_w, l1_gla_out_proj, l1_ffn_norm, l1_ffn_w_gate, l1_ffn_w_up, l1_ffn_w_down, final_norm):
    batch, seqlen, d = x.shape
    xf = x.reshape(batch * seqlen, d).astype(F32)
    xf = _ssd_mixer(xf, batch, l0_mixer_norm, l0_ssd_in_proj, l0_ssd_conv_w, l0_ssd_conv_b,
                    l0_ssd_dt_bias, l0_ssd_a_log, l0_ssd_d, l0_ssd_norm_w, l0_ssd_out_proj)
    xf = _ffn(xf, l0_ffn_norm, l0_ffn_w_gate, l0_ffn_w_up, l0_ffn_w_down)
    xf = _gla_mixer(xf, batch, l1_mixer_norm, l1_gla_in_proj, l1_gla_gk_up, l1_gla_gk_bias,
                    l1_gla_norm_w, l1_gla_out_proj)
    xf = _ffn(xf, l1_ffn_norm, l1_ffn_w_gate, l1_ffn_w_up, l1_ffn_w_down)
    return _rmsnorm(xf, final_norm, NORM_EPS, x.dtype).reshape(batch, seqlen, d)
```

```python
import functools

import jax
import jax.numpy as jnp
from jax import lax
from jax.experimental import pallas as pl
from jax.experimental.pallas import tpu as pltpu

F32 = jnp.float32
BF16 = jnp.bfloat16
HIGHEST = lax.Precision.HIGHEST

NORM_EPS = 1e-6
INNER_NORM_EPS = 1e-5
SSD_D_STATE = 128
SSD_CHUNK = 256
GLA_CHUNK = 64
GLA_GATE_NORMALIZER = 16.0

V7X_LANES = 128
V7X_BF16_SUBLANES = 16
V7X_VMEM_BYTES = 64 * 1024 * 1024
VMEM_LIMIT_BYTES = V7X_VMEM_BYTES - 8 * 1024 * 1024

MM_TILE_M = 1024
MM_TILE_N = 1024
MM_MAX_TILE_K = 3072
MM_MAX_K_STEPS = 8
WCAST_TILE_N = 512
GATEUP_TILE_N = 256
ROW_TILE = 256
CONV_COL_TILE = 1024
GLA_ROW_BLOCK = 256


def _params(*semantics):
    return pltpu.CompilerParams(dimension_semantics=semantics, vmem_limit_bytes=VMEM_LIMIT_BYTES)


def _tile(dim, pref, align=V7X_LANES):
    if dim <= pref:
        return dim
    t = (pref // align) * align
    while t >= align:
        if dim % t == 0:
            return t
        t -= align
    return dim


def _silu(x):
    return x * jax.nn.sigmoid(x)


def _softplus(x):
    return jnp.maximum(x, 0.0) + jnp.log1p(jnp.exp(-jnp.abs(x)))


def _rmsnorm_kernel(x_ref, w_ref, o_ref, *, eps):
    x = x_ref[...]
    ms = jnp.mean(x * x, axis=-1, keepdims=True)
    o_ref[...] = (x * lax.rsqrt(ms + eps) * w_ref[...]).astype(o_ref.dtype)


def _rmsnorm(x, w, eps, out_dtype):
    t, d = x.shape
    tr = _tile(t, ROW_TILE, 8)
    return pl.pallas_call(
        functools.partial(_rmsnorm_kernel, eps=eps),
        grid=(t // tr,),
        in_specs=[pl.BlockSpec((tr, d), lambda i: (i, 0)), pl.BlockSpec((1, d), lambda i: (0, 0))],
        out_specs=pl.BlockSpec((tr, d), lambda i: (i, 0)),
        out_shape=jax.ShapeDtypeStruct((t, d), out_dtype),
        compiler_params=_params("parallel"),
        name="rmsnorm",
    )(x, w.reshape(1, d).astype(F32))


def _mm_wcast_kernel(a_ref, w_ref, o_ref, wbf_ref):
    @pl.when(pl.program_id(1) == 0)
    def _():
        wbf_ref[...] = w_ref[...].astype(BF16)

    o_ref[...] = jnp.dot(a_ref[...], wbf_ref[...], preferred_element_type=F32).astype(o_ref.dtype)


def _matmul_wcast(a, w, n_cols, out_dtype):
    m, k = a.shape
    tm, tn = _tile(m, MM_TILE_M), _tile(n_cols, WCAST_TILE_N)
    return pl.pallas_call(
        _mm_wcast_kernel,
        grid=(n_cols // tn, m // tm),
        in_specs=[pl.BlockSpec((tm, k), lambda j, i: (i, 0)), pl.BlockSpec((k, tn), lambda j, i: (0, j))],
        out_specs=pl.BlockSpec((tm, tn), lambda j, i: (i, j)),
        out_shape=jax.ShapeDtypeStruct((m, n_cols), out_dtype),
        scratch_shapes=[pltpu.VMEM((k, tn), BF16)],
        compiler_params=_params("parallel", "arbitrary"),
        name="matmul_wcast",
    )(a, w)


def _mm_res_kernel(a_ref, w_ref, r_ref, o_ref):
    @pl.when(pl.program_id(2) == 0)
    def _():
        o_ref[...] = r_ref[...]

    o_ref[...] += jnp.dot(a_ref[...], w_ref[...], preferred_element_type=F32)


def _res_tiles(m, k):
    tm, tk = _tile(m, MM_TILE_M), _tile(k, MM_MAX_TILE_K)
    if k // tk > MM_MAX_K_STEPS:
        tm, tk = _tile(m, MM_TILE_M // 2), _tile(k, 2 * MM_MAX_TILE_K)
    return tm, tk


def _matmul_residual(a, w, res):
    m, k = a.shape
    n = w.shape[1]
    tn = _tile(n, MM_TILE_N)
    tm, tk = _res_tiles(m, k)
    return pl.pallas_call(
        _mm_res_kernel,
        grid=(m // tm, n // tn, k // tk),
        in_specs=[
            pl.BlockSpec((tm, tk), lambda i, j, kk: (i, kk)),
            pl.BlockSpec((tk, tn), lambda i, j, kk: (kk, j)),
            pl.BlockSpec((tm, tn), lambda i, j, kk: (i, j)),
        ],
        out_specs=pl.BlockSpec((tm, tn), lambda i, j, kk: (i, j)),
        out_shape=jax.ShapeDtypeStruct((m, n), F32),
        compiler_params=_params("parallel", "parallel", "arbitrary"),
        name="matmul_residual",
    )(a, w, res)


def _gateup_kernel(h_ref, wg_ref, wu_ref, o_ref, wgbf_ref, wubf_ref):
    @pl.when(pl.program_id(1) == 0)
    def _():
        wgbf_ref[...] = wg_ref[...].astype(BF16)
        wubf_ref[...] = wu_ref[...].astype(BF16)

    h = h_ref[...]
    g = jnp.dot(h, wgbf_ref[...], preferred_element_type=F32)
    u = jnp.dot(h, wubf_ref[...], preferred_element_type=F32)
    o_ref[...] = (_silu(g) * u).astype(o_ref.dtype)


def _ffn_gateup(h, wg, wu):
    m, k = h.shape
    n = wg.shape[1]
    tm, tn = _tile(m, MM_TILE_M), _tile(n, GATEUP_TILE_N)
    wspec = pl.BlockSpec((k, tn), lambda j, i: (0, j))
    return pl.pallas_call(
        _gateup_kernel,
        grid=(n // tn, m // tm),
        in_specs=[pl.BlockSpec((tm, k), lambda j, i: (i, 0)), wspec, wspec],
        out_specs=pl.BlockSpec((tm, tn), lambda j, i: (i, j)),
        out_shape=jax.ShapeDtypeStruct((m, n), BF16),
        scratch_shapes=[pltpu.VMEM((k, tn), BF16), pltpu.VMEM((k, tn), BF16)],
        compiler_params=_params("parallel", "arbitrary"),
        name="ffn_gateup",
    )(h, wg, wu)


def _ffn(x, norm_w, w_gate, w_up, w_down):
    h = _rmsnorm(x, norm_w, NORM_EPS, BF16)
    return _matmul_residual(_ffn_gateup(h, w_gate, w_up), w_down.astype(BF16), x)


def _ssd_dt_kernel(h_ref, w_ref, bias_ref, alog_ref, dt_ref, cs_ref, cst_ref, *, n_groups):
    h = h_ref[...]
    lc = h.shape[0]
    nh = bias_ref.shape[1]
    r = nh // n_groups
    raw = jnp.dot(h, w_ref[...].astype(BF16), preferred_element_type=F32)[:, :nh]
    dt = _softplus(raw + bias_ref[...])
    la = dt * -jnp.exp(alog_ref[...])
    row = lax.broadcasted_iota(jnp.int32, (lc, lc), 0)
    col = lax.broadcasted_iota(jnp.int32, (lc, lc), 1)
    cs = jnp.dot((row >= col).astype(F32), la, precision=HIGHEST, preferred_element_type=F32)
    cst = cs.T
    for g in range(n_groups):
        dt_ref[0, g] = dt[:, g * r:(g + 1) * r]
        cs_ref[0, g] = cs[:, g * r:(g + 1) * r]
        cst_ref[0, g] = cst[g * r:(g + 1) * r, :]


def _ssd_dt(h, in_proj, col0, dt_bias, a_log, n_groups):
    t, d = h.shape
    nh = dt_bias.shape[0]
    r = nh // n_groups
    lc = SSD_CHUNK
    nblk = t // lc
    assert col0 % V7X_LANES == 0 and nh <= V7X_LANES
    vec = pl.BlockSpec((1, nh), lambda i: (0, 0))
    tok = pl.BlockSpec((1, n_groups, lc, r), lambda i: (i, 0, 0, 0))
    head = pl.BlockSpec((1, n_groups, r, lc), lambda i: (i, 0, 0, 0))
    return pl.pallas_call(
        functools.partial(_ssd_dt_kernel, n_groups=n_groups),
        grid=(nblk,),
        in_specs=[pl.BlockSpec((lc, d), lambda i: (i, 0)),
                  pl.BlockSpec((d, V7X_LANES), lambda i: (0, col0 // V7X_LANES)), vec, vec],
        out_specs=[tok, tok, head],
        out_shape=[jax.ShapeDtypeStruct((nblk, n_groups, lc, r), F32),
                   jax.ShapeDtypeStruct((nblk, n_groups, lc, r), F32),
                   jax.ShapeDtypeStruct((nblk, n_groups, r, lc), F32)],
        compiler_params=_params("parallel"),
        name="ssd_dt",
    )(h, in_proj, dt_bias.reshape(1, nh).astype(F32), a_log.reshape(1, nh).astype(F32))


def _conv_kernel(cur_ref, prev_ref, w_ref, b_ref, o_ref, buf_ref, *, halo):
    ts = cur_ref.shape[0]
    kw = w_ref.shape[0]
    prev = prev_ref[...].astype(F32)
    buf_ref[0:halo, :] = jnp.where(pl.program_id(1) == 0, 0.0, prev)
    buf_ref[halo:halo + ts, :] = cur_ref[...].astype(F32)
    acc = b_ref[...] + buf_ref[halo:halo + ts, :] * w_ref[kw - 1:kw, :]
    for j in range(kw - 1):
        off = halo - (kw - 1) + j
        acc = acc + buf_ref[off:off + ts, :] * w_ref[j:j + 1, :]
    o_ref[...] = _silu(acc).astype(o_ref.dtype)


def _ssd_conv(zxbc, conv_w, conv_b, batch, col0):
    t = zxbc.shape[0]
    c, kw = conv_w.shape
    seqlen = t // batch
    halo = V7X_BF16_SUBLANES
    ts = _tile(seqlen, ROW_TILE, halo)
    tc = _tile(c, CONV_COL_TILE)
    nt = seqlen // ts
    cb0 = col0 // tc
    assert col0 % tc == 0 and ts % halo == 0 and kw - 1 <= halo
    return pl.pallas_call(
        functools.partial(_conv_kernel, halo=halo),
        grid=(batch, nt, c // tc),
        in_specs=[
            pl.BlockSpec((ts, tc), lambda b, i, j: (b * nt + i, cb0 + j)),
            pl.BlockSpec((halo, tc),
                         lambda b, i, j: (jnp.maximum((b * nt + i) * (ts // halo) - 1, 0), cb0 + j)),
            pl.BlockSpec((kw, tc), lambda b, i, j: (0, j)),
            pl.BlockSpec((1, tc), lambda b, i, j: (0, j)),
        ],
        out_specs=pl.BlockSpec((ts, tc), lambda b, i, j: (b * nt + i, j)),
        out_shape=jax.ShapeDtypeStruct((t, c), BF16),
        scratch_shapes=[pltpu.VMEM((ts + halo, tc), F32)],
        compiler_params=_params("parallel", "parallel", "parallel"),
        name="ssd_conv",
    )(zxbc, zxbc, conv_w.T.astype(F32), conv_b.reshape(1, c).astype(F32))


def _ssd_scan_kernel(x_ref, b_ref, c_ref, z_ref, dt_ref, cs_ref, cst_ref, dskip_ref, nw_ref,
                     o_ref, state_ref, *, heads, head_dim, eps):
    @pl.when(pl.program_id(2) == 0)
    def _():
        state_ref[...] = jnp.zeros_like(state_ref)

    lc = x_ref.shape[0]
    gw = heads * head_dim
    pair = 2 * head_dim
    x = x_ref[...].astype(F32)
    bm = b_ref[...]
    cm = c_ref[...]
    dt = dt_ref[0, 0]
    cs = cs_ref[0, 0]
    cst = cst_ref[0, 0]

    e_row = lax.broadcasted_iota(jnp.int32, (3 * heads, gw), 0) % heads
    e_col = lax.broadcasted_iota(jnp.int32, (3 * heads, gw), 1) // head_dim
    expand_m = (e_row == e_col).astype(BF16)

    def expand(a):
        hi = a.astype(BF16).astype(F32)
        mid = (a - hi).astype(BF16).astype(F32)
        lo = a - hi - mid
        pieces = jnp.concatenate([hi, mid, lo], axis=1).astype(BF16)
        return jnp.dot(pieces, expand_m, preferred_element_type=F32)

    cs_end = cs[lc - 1:lc, :]
    dt_e = expand(dt)
    ecs_e = expand(jnp.exp(cs))
    dtd_e = expand(dt * jnp.exp(cs_end - cs))

    xdt = (x * dt_e).astype(BF16)
    cb = lax.dot_general(cm, bm, (((1,), (1,)), ((), ())), preferred_element_type=F32)
    causal = lax.broadcasted_iota(jnp.int32, (lc, lc), 0) >= lax.broadcasted_iota(jnp.int32, (lc, lc), 1)
    first_half = lax.broadcasted_iota(jnp.int32, (lc, pair), 1) < head_dim

    def masked_cb(r):
        seg = cs[:, r:r + 1] - cst[r:r + 1, :]
        return (cb * jnp.exp(jnp.where(causal, seg, -jnp.inf))).astype(BF16)

    ys = []
    for p in range(heads // 2):
        xp = xdt[:, p * pair:(p + 1) * pair]
        xa = jnp.where(first_half, xp, jnp.zeros_like(xp))
        xb = jnp.where(first_half, jnp.zeros_like(xp), xp)
        ys.append(jnp.dot(masked_cb(2 * p), xa, preferred_element_type=F32)
                  + jnp.dot(masked_cb(2 * p + 1), xb, preferred_element_type=F32))
    y = jnp.concatenate(ys, axis=1) if len(ys) > 1 else ys[0]

    state = state_ref[...]
    y = y + jnp.dot(cm, state.astype(BF16), preferred_element_type=F32) * ecs_e
    state_ref[...] = state * ecs_e[lc - 1:lc, :] + lax.dot_general(
        bm, (x * dtd_e).astype(BF16), (((0,), (0,)), ((), ())), preferred_element_type=F32)

    y = y + x * dskip_ref[...]
    y = y * _silu(z_ref[...].astype(F32))
    ms = jnp.mean(y * y, axis=-1, keepdims=True)
    o_ref[...] = (y * lax.rsqrt(ms + eps) * nw_ref[...]).astype(o_ref.dtype)


def _ssd_scan(zxbc, xbc, dt, cs, cst, d_skip, norm_w, batch, d_inner, n_groups, heads):
    t = zxbc.shape[0]
    lc = SSD_CHUNK
    nc = t // batch // lc
    n = SSD_D_STATE
    gw = d_inner // n_groups
    head_dim = gw // heads
    assert heads % 2 == 0 and 2 * head_dim == V7X_LANES
    row = lambda b, g, c: b * nc + c
    small = lambda shape: pl.BlockSpec(shape, lambda b, g, c: (row(b, g, c), g, 0, 0))
    return pl.pallas_call(
        functools.partial(_ssd_scan_kernel, heads=heads, head_dim=head_dim, eps=INNER_NORM_EPS),
        grid=(batch, n_groups, nc),
        in_specs=[
            pl.BlockSpec((lc, gw), lambda b, g, c: (row(b, g, c), g)),
            pl.BlockSpec((lc, n), lambda b, g, c: (row(b, g, c), d_inner // n + g)),
            pl.BlockSpec((lc, n), lambda b, g, c: (row(b, g, c), d_inner // n + n_groups + g)),
            pl.BlockSpec((lc, gw), lambda b, g, c: (row(b, g, c), g)),
            small((1, 1, lc, heads)), small((1, 1, lc, heads)), small((1, 1, heads, lc)),
            pl.BlockSpec((1, gw), lambda b, g, c: (0, g)),
            pl.BlockSpec((1, gw), lambda b, g, c: (0, g)),
        ],
        out_specs=pl.BlockSpec((lc, gw), lambda b, g, c: (row(b, g, c), g)),
        out_shape=jax.ShapeDtypeStruct((t, d_inner), BF16),
        scratch_shapes=[pltpu.VMEM((n, gw), F32)],
        compiler_params=_params("parallel", "parallel", "arbitrary"),
        name="ssd_scan",
    )(xbc, xbc, xbc, zxbc, dt, cs, cst,
      jnp.repeat(d_skip.astype(F32), head_dim).reshape(1, d_inner),
      norm_w.reshape(1, d_inner).astype(F32))


def _ssd_mixer(x, batch, norm_w, in_proj, conv_w, conv_b, dt_bias, a_log, d_skip, inner_norm_w, out_proj):
    d_inner = out_proj.shape[0]
    conv_dim = conv_w.shape[0]
    n_heads = dt_bias.shape[0]
    n_groups = (conv_dim - d_inner) // 2 // SSD_D_STATE
    main = d_inner + conv_dim
    h = _rmsnorm(x, norm_w, NORM_EPS, BF16)
    zxbc = _matmul_wcast(h, in_proj, main, BF16)
    dt, cs, cst = _ssd_dt(h, in_proj, main, dt_bias, a_log, n_groups)
    xbc = _ssd_conv(zxbc, conv_w, conv_b, batch, d_inner)
    y = _ssd_scan(zxbc, xbc, dt, cs, cst, d_skip, inner_norm_w, batch, d_inner, n_groups,
                  n_heads // n_groups)
    return _matmul_residual(y, out_proj.astype(BF16), x)


def _gla_gate_kernel(h_ref, wlow_ref, up_ref, bias_ref, o_ref, *, rank):
    low = jnp.dot(h_ref[...], wlow_ref[...].astype(BF16), preferred_element_type=F32)
    low = jnp.where(lax.broadcasted_iota(jnp.int32, low.shape, 1) < rank, low, 0.0)
    x = jnp.dot(low, up_ref[...], precision=HIGHEST, preferred_element_type=F32) + bias_ref[...]
    o_ref[...] = (jnp.minimum(x, 0.0) - jnp.log1p(jnp.exp(-jnp.abs(x)))) * (1.0 / GLA_GATE_NORMALIZER)


def _gla_gate(h, in_proj, col0, gk_up, gk_bias):
    t, d = h.shape
    rank, kd = gk_up.shape
    assert col0 % V7X_LANES == 0 and rank <= V7X_LANES
    gk_up = jnp.pad(gk_up.astype(F32), ((0, V7X_LANES - rank), (0, 0)))
    tr = _tile(t, 2 * ROW_TILE, 8)
    return pl.pallas_call(
        functools.partial(_gla_gate_kernel, rank=rank),
        grid=(t // tr,),
        in_specs=[pl.BlockSpec((tr, d), lambda i: (i, 0)),
                  pl.BlockSpec((d, V7X_LANES), lambda i: (0, col0 // V7X_LANES)),
                  pl.BlockSpec((V7X_LANES, kd), lambda i: (0, 0)),
                  pl.BlockSpec((1, kd), lambda i: (0, 0))],
        out_specs=pl.BlockSpec((tr, kd), lambda i: (i, 0)),
        out_shape=jax.ShapeDtypeStruct((t, kd), F32),
        compiler_params=_params("parallel"),
        name="gla_gate",
    )(h, in_proj, gk_up, gk_bias.reshape(1, kd).astype(F32))


def _gla_scan_kernel(q_ref, k_ref, v_ref, g_ref, go_ref, nw_ref, o_ref, state_ref, *, chunk, scale, eps):
    @pl.when(pl.program_id(2) == 0)
    def _():
        state_ref[...] = jnp.zeros_like(state_ref)

    rows = q_ref.shape[0]
    causal = lax.broadcasted_iota(jnp.int32, (chunk, chunk), 0) >= lax.broadcasted_iota(jnp.int32, (chunk, chunk), 1)
    tril = causal.astype(F32)
    for s in range(rows // chunk):
        sl = pl.ds(s * chunk, chunk)
        q = q_ref[sl, :].astype(F32) * scale
        k = k_ref[sl, :].astype(F32)
        v = v_ref[sl, :]
        g = g_ref[sl, :]
        cs = jnp.dot(tril, g, precision=HIGHEST, preferred_element_type=F32)
        q_dec = (q * jnp.exp(cs)).astype(BF16)
        k_inv = (k * jnp.exp(-cs)).astype(BF16)
        scores = lax.dot_general(q_dec, k_inv, (((1,), (1,)), ((), ())), preferred_element_type=F32)
        scores = jnp.where(causal, scores, 0.0).astype(BF16)
        state = state_ref[...]
        o = (jnp.dot(scores, v, preferred_element_type=F32)
             + jnp.dot(q_dec, state.astype(BF16), preferred_element_type=F32))
        k_end = (k * jnp.exp(cs[chunk - 1:chunk, :] - cs)).astype(BF16)
        decay_end = jnp.exp(jnp.sum(g.T, axis=1, keepdims=True))
        state_ref[...] = state * decay_end + lax.dot_general(
            k_end, v, (((0,), (0,)), ((), ())), preferred_element_type=F32)
        ms = jnp.mean(o * o, axis=-1, keepdims=True)
        o = o * lax.rsqrt(ms + eps) * nw_ref[...]
        o_ref[sl, :] = (o * _silu(go_ref[sl, :].astype(F32))).astype(o_ref.dtype)


def _gla_scan(qkvg, log_g, norm_w, batch, key_dim, value_dim, n_heads):
    t = qkvg.shape[0]
    hk, hv = key_dim // n_heads, value_dim // n_heads
    seqlen = t // batch
    rb = _tile(seqlen, GLA_ROW_BLOCK, GLA_CHUNK)
    nb = seqlen // rb
    assert rb % GLA_CHUNK == 0
    row = lambda b, h, i: b * nb + i
    assert (2 * key_dim) % hv == 0
    v0 = 2 * key_dim // hv
    return pl.pallas_call(
        functools.partial(_gla_scan_kernel, chunk=GLA_CHUNK, scale=hk ** -0.5, eps=INNER_NORM_EPS),
        grid=(batch, n_heads, nb),
        in_specs=[
            pl.BlockSpec((rb, hk), lambda b, h, i: (row(b, h, i), h)),
            pl.BlockSpec((rb, hk), lambda b, h, i: (row(b, h, i), n_heads + h)),
            pl.BlockSpec((rb, hv), lambda b, h, i: (row(b, h, i), v0 + h)),
            pl.BlockSpec((rb, hk), lambda b, h, i: (row(b, h, i), h)),
            pl.BlockSpec((rb, hv), lambda b, h, i: (row(b, h, i), v0 + n_heads + h)),
            pl.BlockSpec((1, hv), lambda b, h, i: (0, 0)),
        ],
        out_specs=pl.BlockSpec((rb, hv), lambda b, h, i: (row(b, h, i), h)),
        out_shape=jax.ShapeDtypeStruct((t, value_dim), BF16),
        scratch_shapes=[pltpu.VMEM((hk, hv), F32)],
        compiler_params=_params("parallel", "parallel", "arbitrary"),
        name="gla_scan",
    )(qkvg, qkvg, qkvg, log_g, qkvg, norm_w.reshape(1, hv).astype(F32))


def _gla_mixer(x, batch, norm_w, in_proj, gk_up, gk_bias, inner_norm_w, out_proj):
    key_dim = gk_up.shape[1]
    value_dim = out_proj.shape[0]
    n_heads = value_dim // inner_norm_w.shape[0]
    main = 2 * key_dim + 2 * value_dim
    h = _rmsnorm(x, norm_w, NORM_EPS, BF16)
    qkvg = _matmul_wcast(h, in_proj, main, BF16)
    log_g = _gla_gate(h, in_proj, main, gk_up, gk_bias)
    o = _gla_scan(qkvg, log_g, inner_norm_w, batch, key_dim, value_dim, n_heads)
    return _matmul_residual(o, out_proj.astype(BF16), x)


def kernel(x, l0_mixer_norm, l0_ssd_in_proj, l0_ssd_conv_w, l0_ssd_conv_b, l0_ssd_dt_bias, l0_ssd_a_log, l0_ssd_d, l0_ssd_norm_w, l0_ssd_out_proj, l0_ffn_norm, l0_ffn_w_gate, l0_ffn_w_up, l0_ffn_w_down, l1_mixer_norm, l1_gla_in_proj, l1_gla_gk_up, l1_gla_gk_bias, l1_gla_norm_w, l1_gla_out_proj, l1_ffn_norm, l1_ffn_w_gate, l1_ffn_w_up, l1_ffn_w_down, final_norm):
    batch, seqlen, d = x.shape
    xf = x.reshape(batch * seqlen, d).astype(F32)
    xf = _ssd_mixer(xf, batch, l0_mixer_norm, l0_ssd_in_proj, l0_ssd_conv_w, l0_ssd_conv_b,
                    l0_ssd_dt_bias, l0_ssd_a_log, l0_ssd_d, l0_ssd_norm_w, l0_ssd_out_proj)
    xf = _ffn(xf, l0_ffn_norm, l0_ffn_w_gate, l0_ffn_w_up, l0_ffn_w_down)
    xf = _gla_mixer(xf, batch, l1_mixer_norm, l1_gla_in_proj, l1_gla_gk_up, l1_gla_gk_bias,
                    l1_gla_norm_w, l1_gla_out_proj)
    xf = _ffn(xf, l1_ffn_norm, l1_ffn_w_gate, l1_ffn_w_up, l1_ffn_w_down)
    return _rmsnorm(xf, final_norm, NORM_EPS, x.dtype).reshape(batch, seqlen, d)
```

```python
import functools

import jax
import jax.numpy as jnp
from jax import lax
from jax.experimental import pallas as pl
from jax.experimental.pallas import tpu as pltpu

F32 = jnp.float32
BF16 = jnp.bfloat16
HIGHEST = lax.Precision.HIGHEST

NORM_EPS = 1e-6
INNER_NORM_EPS = 1e-5
SSD_D_STATE = 128
SSD_CHUNK = 256
GLA_CHUNK = 64
GLA_GATE_NORMALIZER = 16.0

V7X_LANES = 128
V7X_BF16_SUBLANES = 16
V7X_VMEM_BYTES = 64 * 1024 * 1024
VMEM_LIMIT_BYTES = V7X_VMEM_BYTES - 8 * 1024 * 1024

MM_TILE_M = 1024
MM_TILE_N = 1024
MM_MAX_TILE_K = 3072
MM_MAX_K_STEPS = 8
WCAST_TILE_M = 2048
WCAST_TILE_N = 512
GATEUP_TILE_N = 256
ROW_TILE = 256
CONV_COL_TILE = 1024
GLA_ROW_BLOCK = 256


def _params(*semantics):
    return pltpu.CompilerParams(dimension_semantics=semantics, vmem_limit_bytes=VMEM_LIMIT_BYTES)


def _tile(dim, pref, align=V7X_LANES):
    if dim <= pref:
        return dim
    t = (pref // align) * align
    while t >= align:
        if dim % t == 0:
            return t
        t -= align
    return dim


def _silu(x):
    return x * jax.nn.sigmoid(x)


def _softplus(x):
    return jnp.maximum(x, 0.0) + jnp.log1p(jnp.exp(-jnp.abs(x)))


def _rmsnorm_kernel(x_ref, w_ref, o_ref, *, eps):
    x = x_ref[...]
    ms = jnp.mean(x * x, axis=-1, keepdims=True)
    o_ref[...] = (x * lax.rsqrt(ms + eps) * w_ref[...]).astype(o_ref.dtype)


def _rmsnorm(x, w, eps, out_dtype):
    t, d = x.shape
    tr = _tile(t, ROW_TILE, 8)
    return pl.pallas_call(
        functools.partial(_rmsnorm_kernel, eps=eps),
        grid=(t // tr,),
        in_specs=[pl.BlockSpec((tr, d), lambda i: (i, 0)), pl.BlockSpec((1, d), lambda i: (0, 0))],
        out_specs=pl.BlockSpec((tr, d), lambda i: (i, 0)),
        out_shape=jax.ShapeDtypeStruct((t, d), out_dtype),
        compiler_params=_params("parallel"),
        name="rmsnorm",
    )(x, w.reshape(1, d).astype(F32))


def _mm_wcast_kernel(a_ref, w_ref, o_ref):
    w = w_ref[...].astype(BF16)
    o_ref[...] = jnp.dot(a_ref[...], w, preferred_element_type=F32).astype(o_ref.dtype)


def _resident_rows(tm, k):
    return pl.BlockSpec((tm, k), lambda i, j: (i, 0), pipeline_mode=pl.Buffered(1))


def _matmul_wcast(a, w, n_cols, out_dtype):
    m, k = a.shape
    tm, tn = _tile(m, WCAST_TILE_M), _tile(n_cols, WCAST_TILE_N)
    return pl.pallas_call(
        _mm_wcast_kernel,
        grid=(m // tm, n_cols // tn),
        in_specs=[_resident_rows(tm, k), pl.BlockSpec((k, tn), lambda i, j: (0, j))],
        out_specs=pl.BlockSpec((tm, tn), lambda i, j: (i, j)),
        out_shape=jax.ShapeDtypeStruct((m, n_cols), out_dtype),
        compiler_params=_params("parallel", "arbitrary"),
        name="matmul_wcast",
    )(a, w)


def _mm_res_kernel(a_ref, w_ref, r_ref, o_ref):
    @pl.when(pl.program_id(2) == 0)
    def _():
        o_ref[...] = r_ref[...]

    o_ref[...] += jnp.dot(a_ref[...], w_ref[...], preferred_element_type=F32)


def _res_tiles(m, k):
    tm, tk = _tile(m, MM_TILE_M), _tile(k, MM_MAX_TILE_K)
    if k // tk > MM_MAX_K_STEPS:
        tm, tk = _tile(m, MM_TILE_M // 2), _tile(k, 2 * MM_MAX_TILE_K)
    return tm, tk


def _matmul_residual(a, w, res):
    m, k = a.shape
    n = w.shape[1]
    tn = _tile(n, MM_TILE_N)
    tm, tk = _res_tiles(m, k)
    return pl.pallas_call(
        _mm_res_kernel,
        grid=(m // tm, n // tn, k // tk),
        in_specs=[
            pl.BlockSpec((tm, tk), lambda i, j, kk: (i, kk)),
            pl.BlockSpec((tk, tn), lambda i, j, kk: (kk, j)),
            pl.BlockSpec((tm, tn), lambda i, j, kk: (i, j)),
        ],
        out_specs=pl.BlockSpec((tm, tn), lambda i, j, kk: (i, j)),
        out_shape=jax.ShapeDtypeStruct((m, n), F32),
        compiler_params=_params("parallel", "parallel", "arbitrary"),
        name="matmul_residual",
    )(a, w, res)


def _gateup_kernel(h_ref, wg_ref, wu_ref, o_ref):
    h = h_ref[...]
    g = jnp.dot(h, wg_ref[...].astype(BF16), preferred_element_type=F32)
    u = jnp.dot(h, wu_ref[...].astype(BF16), preferred_element_type=F32)
    o_ref[...] = (_silu(g) * u).astype(o_ref.dtype)


def _ffn_gateup(h, wg, wu):
    m, k = h.shape
    n = wg.shape[1]
    tm, tn = _tile(m, WCAST_TILE_M), _tile(n, GATEUP_TILE_N)
    wspec = pl.BlockSpec((k, tn), lambda i, j: (0, j))
    return pl.pallas_call(
        _gateup_kernel,
        grid=(m // tm, n // tn),
        in_specs=[_resident_rows(tm, k), wspec, wspec],
        out_specs=pl.BlockSpec((tm, tn), lambda i, j: (i, j)),
        out_shape=jax.ShapeDtypeStruct((m, n), BF16),
        compiler_params=_params("parallel", "arbitrary"),
        name="ffn_gateup",
    )(h, wg, wu)


def _ffn(x, norm_w, w_gate, w_up, w_down):
    h = _rmsnorm(x, norm_w, NORM_EPS, BF16)
    return _matmul_residual(_ffn_gateup(h, w_gate, w_up), w_down.astype(BF16), x)


def _ssd_dt_kernel(h_ref, w_ref, bias_ref, alog_ref, dt_ref, cs_ref, cst_ref, *, n_groups):
    h = h_ref[...]
    lc = h.shape[0]
    nh = bias_ref.shape[1]
    r = nh // n_groups
    raw = jnp.dot(h, w_ref[...].astype(BF16), preferred_element_type=F32)[:, :nh]
    dt = _softplus(raw + bias_ref[...])
    la = dt * -jnp.exp(alog_ref[...])
    row = lax.broadcasted_iota(jnp.int32, (lc, lc), 0)
    col = lax.broadcasted_iota(jnp.int32, (lc, lc), 1)
    cs = jnp.dot((row >= col).astype(F32), la, precision=HIGHEST, preferred_element_type=F32)
    cst = cs.T
    for g in range(n_groups):
        dt_ref[0, g] = dt[:, g * r:(g + 1) * r]
        cs_ref[0, g] = cs[:, g * r:(g + 1) * r]
        cst_ref[0, g] = cst[g * r:(g + 1) * r, :]


def _ssd_dt(h, in_proj, col0, dt_bias, a_log, n_groups):
    t, d = h.shape
    nh = dt_bias.shape[0]
    r = nh // n_groups
    lc = SSD_CHUNK
    nblk = t // lc
    assert col0 % V7X_LANES == 0 and nh <= V7X_LANES
    vec = pl.BlockSpec((1, nh), lambda i: (0, 0))
    tok = pl.BlockSpec((1, n_groups, lc, r), lambda i: (i, 0, 0, 0))
    head = pl.BlockSpec((1, n_groups, r, lc), lambda i: (i, 0, 0, 0))
    return pl.pallas_call(
        functools.partial(_ssd_dt_kernel, n_groups=n_groups),
        grid=(nblk,),
        in_specs=[pl.BlockSpec((lc, d), lambda i: (i, 0)),
                  pl.BlockSpec((d, V7X_LANES), lambda i: (0, col0 // V7X_LANES)), vec, vec],
        out_specs=[tok, tok, head],
        out_shape=[jax.ShapeDtypeStruct((nblk, n_groups, lc, r), F32),
                   jax.ShapeDtypeStruct((nblk, n_groups, lc, r), F32),
                   jax.ShapeDtypeStruct((nblk, n_groups, r, lc), F32)],
        compiler_params=_params("parallel"),
        name="ssd_dt",
    )(h, in_proj, dt_bias.reshape(1, nh).astype(F32), a_log.reshape(1, nh).astype(F32))


def _conv_kernel(cur_ref, prev_ref, w_ref, b_ref, o_ref, buf_ref, *, halo):
    ts = cur_ref.shape[0]
    kw = w_ref.shape[0]
    prev = prev_ref[...].astype(F32)
    buf_ref[0:halo, :] = jnp.where(pl.program_id(1) == 0, 0.0, prev)
    buf_ref[halo:halo + ts, :] = cur_ref[...].astype(F32)
    acc = b_ref[...] + buf_ref[halo:halo + ts, :] * w_ref[kw - 1:kw, :]
    for j in range(kw - 1):
        off = halo - (kw - 1) + j
        acc = acc + buf_ref[off:off + ts, :] * w_ref[j:j + 1, :]
    o_ref[...] = _silu(acc).astype(o_ref.dtype)


def _ssd_conv(zxbc, conv_w, conv_b, batch, col0):
    t = zxbc.shape[0]
    c, kw = conv_w.shape
    seqlen = t // batch
    halo = V7X_BF16_SUBLANES
    ts = _tile(seqlen, ROW_TILE, halo)
    tc = _tile(c, CONV_COL_TILE)
    nt = seqlen // ts
    cb0 = col0 // tc
    assert col0 % tc == 0 and ts % halo == 0 and kw - 1 <= halo
    return pl.pallas_call(
        functools.partial(_conv_kernel, halo=halo),
        grid=(batch, nt, c // tc),
        in_specs=[
            pl.BlockSpec((ts, tc), lambda b, i, j: (b * nt + i, cb0 + j)),
            pl.BlockSpec((halo, tc),
                         lambda b, i, j: (jnp.maximum((b * nt + i) * (ts // halo) - 1, 0), cb0 + j)),
            pl.BlockSpec((kw, tc), lambda b, i, j: (0, j)),
            pl.BlockSpec((1, tc), lambda b, i, j: (0, j)),
        ],
        out_specs=pl.BlockSpec((ts, tc), lambda b, i, j: (b * nt + i, j)),
        out_shape=jax.ShapeDtypeStruct((t, c), BF16),
        scratch_shapes=[pltpu.VMEM((ts + halo, tc), F32)],
        compiler_params=_params("parallel", "parallel", "parallel"),
        name="ssd_conv",
    )(zxbc, zxbc, conv_w.T.astype(F32), conv_b.reshape(1, c).astype(F32))


def _ssd_scan_kernel(x_ref, b_ref, c_ref, z_ref, dt_ref, cs_ref, cst_ref, dskip_ref, nw_ref,
                     o_ref, state_ref, *, heads, head_dim, eps):
    @pl.when(pl.program_id(2) == 0)
    def _():
        state_ref[...] = jnp.zeros_like(state_ref)

    lc = x_ref.shape[0]
    gw = heads * head_dim
    pair = 2 * head_dim
    x = x_ref[...].astype(F32)
    bm = b_ref[...]
    cm = c_ref[...]
    dt = dt_ref[0, 0]
    cs = cs_ref[0, 0]
    cst = cst_ref[0, 0]

    e_row = lax.broadcasted_iota(jnp.int32, (3 * heads, gw), 0) % heads
    e_col = lax.broadcasted_iota(jnp.int32, (3 * heads, gw), 1) // head_dim
    expand_m = (e_row == e_col).astype(BF16)

    def expand(a):
        hi = a.astype(BF16).astype(F32)
        mid = (a - hi).astype(BF16).astype(F32)
        lo = a - hi - mid
        pieces = jnp.concatenate([hi, mid, lo], axis=1).astype(BF16)
        return jnp.dot(pieces, expand_m, preferred_element_type=F32)

    cs_end = cs[lc - 1:lc, :]
    dt_e = expand(dt)
    ecs_e = expand(jnp.exp(cs))
    dtd_e = expand(dt * jnp.exp(cs_end - cs))

    xdt = (x * dt_e).astype(BF16)
    cb = lax.dot_general(cm, bm, (((1,), (1,)), ((), ())), preferred_element_type=F32)
    causal = lax.broadcasted_iota(jnp.int32, (lc, lc), 0) >= lax.broadcasted_iota(jnp.int32, (lc, lc), 1)
    first_half = lax.broadcasted_iota(jnp.int32, (lc, pair), 1) < head_dim

    def masked_cb(r):
        seg = cs[:, r:r + 1] - cst[r:r + 1, :]
        return (cb * jnp.exp(jnp.where(causal, seg, -jnp.inf))).astype(BF16)

    ys = []
    for p in range(heads // 2):
        xp = xdt[:, p * pair:(p + 1) * pair]
        xa = jnp.where(first_half, xp, jnp.zeros_like(xp))
        xb = jnp.where(first_half, jnp.zeros_like(xp), xp)
        ys.append(jnp.dot(masked_cb(2 * p), xa, preferred_element_type=F32)
                  + jnp.dot(masked_cb(2 * p + 1), xb, preferred_element_type=F32))
    y = jnp.concatenate(ys, axis=1) if len(ys) > 1 else ys[0]

    state = state_ref[...]
    y = y + jnp.dot(cm, state.astype(BF16), preferred_element_type=F32) * ecs_e
    state_ref[...] = state * ecs_e[lc - 1:lc, :] + lax.dot_general(
        bm, (x * dtd_e).astype(BF16), (((0,), (0,)), ((), ())), preferred_element_type=F32)

    y = y + x * dskip_ref[...]
    y = y * _silu(z_ref[...].astype(F32))
    ms = jnp.mean(y * y, axis=-1, keepdims=True)
    o_ref[...] = (y * lax.rsqrt(ms + eps) * nw_ref[...]).astype(o_ref.dtype)


def _ssd_scan(zxbc, xbc, dt, cs, cst, d_skip, norm_w, batch, d_inner, n_groups, heads):
    t = zxbc.shape[0]
    lc = SSD_CHUNK
    nc = t // batch // lc
    n = SSD_D_STATE
    gw = d_inner // n_groups
    head_dim = gw // heads
    assert heads % 2 == 0 and 2 * head_dim == V7X_LANES
    row = lambda b, g, c: b * nc + c
    small = lambda shape: pl.BlockSpec(shape, lambda b, g, c: (row(b, g, c), g, 0, 0))
    return pl.pallas_call(
        functools.partial(_ssd_scan_kernel, heads=heads, head_dim=head_dim, eps=INNER_NORM_EPS),
        grid=(batch, n_groups, nc),
        in_specs=[
            pl.BlockSpec((lc, gw), lambda b, g, c: (row(b, g, c), g)),
            pl.BlockSpec((lc, n), lambda b, g, c: (row(b, g, c), d_inner // n + g)),
            pl.BlockSpec((lc, n), lambda b, g, c: (row(b, g, c), d_inner // n + n_groups + g)),
            pl.BlockSpec((lc, gw), lambda b, g, c: (row(b, g, c), g)),
            small((1, 1, lc, heads)), small((1, 1, lc, heads)), small((1, 1, heads, lc)),
            pl.BlockSpec((1, gw), lambda b, g, c: (0, g)),
            pl.BlockSpec((1, gw), lambda b, g, c: (0, g)),
        ],
        out_specs=pl.BlockSpec((lc, gw), lambda b, g, c: (row(b, g, c), g)),
        out_shape=jax.ShapeDtypeStruct((t, d_inner), BF16),
        scratch_shapes=[pltpu.VMEM((n, gw), F32)],
        compiler_params=_params("parallel", "parallel", "arbitrary"),
        name="ssd_scan",
    )(xbc, xbc, xbc, zxbc, dt, cs, cst,
      jnp.repeat(d_skip.astype(F32), head_dim).reshape(1, d_inner),
      norm_w.reshape(1, d_inner).astype(F32))


def _ssd_mixer(x, batch, norm_w, in_proj, conv_w, conv_b, dt_bias, a_log, d_skip, inner_norm_w, out_proj):
    d_inner = out_proj.shape[0]
    conv_dim = conv_w.shape[0]
    n_heads = dt_bias.shape[0]
    n_groups = (conv_dim - d_inner) // 2 // SSD_D_STATE
    main = d_inner + conv_dim
    h = _rmsnorm(x, norm_w, NORM_EPS, BF16)
    zxbc = _matmul_wcast(h, in_proj, main, BF16)
    dt, cs, cst = _ssd_dt(h, in_proj, main, dt_bias, a_log, n_groups)
    xbc = _ssd_conv(zxbc, conv_w, conv_b, batch, d_inner)
    y = _ssd_scan(zxbc, xbc, dt, cs, cst, d_skip, inner_norm_w, batch, d_inner, n_groups,
                  n_heads // n_groups)
    return _matmul_residual(y, out_proj.astype(BF16), x)


def _gla_gate_kernel(h_ref, wlow_ref, up_ref, bias_ref, o_ref, *, rank):
    low = jnp.dot(h_ref[...], wlow_ref[...].astype(BF16), preferred_element_type=F32)
    low = jnp.where(lax.broadcasted_iota(jnp.int32, low.shape, 1) < rank, low, 0.0)
    x = jnp.dot(low, up_ref[...], precision=HIGHEST, preferred_element_type=F32) + bias_ref[...]
    o_ref[...] = (jnp.minimum(x, 0.0) - jnp.log1p(jnp.exp(-jnp.abs(x)))) * (1.0 / GLA_GATE_NORMALIZER)


def _gla_gate(h, in_proj, col0, gk_up, gk_bias):
    t, d = h.shape
    rank, kd = gk_up.shape
    assert col0 % V7X_LANES == 0 and rank <= V7X_LANES
    gk_up = jnp.pad(gk_up.astype(F32), ((0, V7X_LANES - rank), (0, 0)))
    tr = _tile(t, 2 * ROW_TILE, 8)
    return pl.pallas_call(
        functools.partial(_gla_gate_kernel, rank=rank),
        grid=(t // tr,),
        in_specs=[pl.BlockSpec((tr, d), lambda i: (i, 0)),
                  pl.BlockSpec((d, V7X_LANES), lambda i: (0, col0 // V7X_LANES)),
                  pl.BlockSpec((V7X_LANES, kd), lambda i: (0, 0)),
                  pl.BlockSpec((1, kd), lambda i: (0, 0))],
        out_specs=pl.BlockSpec((tr, kd), lambda i: (i, 0)),
        out_shape=jax.ShapeDtypeStruct((t, kd), F32),
        compiler_params=_params("parallel"),
        name="gla_gate",
    )(h, in_proj, gk_up, gk_bias.reshape(1, kd).astype(F32))


def _gla_scan_kernel(q_ref, k_ref, v_ref, g_ref, go_ref, nw_ref, o_ref, state_ref, *, chunk, scale, eps):
    @pl.when(pl.program_id(2) == 0)
    def _():
        state_ref[...] = jnp.zeros_like(state_ref)

    rows = q_ref.shape[0]
    causal = lax.broadcasted_iota(jnp.int32, (chunk, chunk), 0) >= lax.broadcasted_iota(jnp.int32, (chunk, chunk), 1)
    tril = causal.astype(F32)
    for s in range(rows // chunk):
        sl = pl.ds(s * chunk, chunk)
        q = q_ref[sl, :].astype(F32) * scale
        k = k_ref[sl, :].astype(F32)
        v = v_ref[sl, :]
        g = g_ref[sl, :]
        cs = jnp.dot(tril, g, precision=HIGHEST, preferred_element_type=F32)
        q_dec = (q * jnp.exp(cs)).astype(BF16)
        k_inv = (k * jnp.exp(-cs)).astype(BF16)
        scores = lax.dot_general(q_dec, k_inv, (((1,), (1,)), ((), ())), preferred_element_type=F32)
        scores = jnp.where(causal, scores, 0.0).astype(BF16)
        state = state_ref[...]
        o = (jnp.dot(scores, v, preferred_element_type=F32)
             + jnp.dot(q_dec, state.astype(BF16), preferred_element_type=F32))
        k_end = (k * jnp.exp(cs[chunk - 1:chunk, :] - cs)).astype(BF16)
        decay_end = jnp.exp(jnp.sum(g.T, axis=1, keepdims=True))
        state_ref[...] = state * decay_end + lax.dot_general(
            k_end, v, (((0,), (0,)), ((), ())), preferred_element_type=F32)
        ms = jnp.mean(o * o, axis=-1, keepdims=True)
        o = o * lax.rsqrt(ms + eps) * nw_ref[...]
        o_ref[sl, :] = (o * _silu(go_ref[sl, :].astype(F32))).astype(o_ref.dtype)


def _gla_scan(qkvg, log_g, norm_w, batch, key_dim, value_dim, n_heads):
    t = qkvg.shape[0]
    hk, hv = key_dim // n_heads, value_dim // n_heads
    seqlen = t // batch
    rb = _tile(seqlen, GLA_ROW_BLOCK, GLA_CHUNK)
    nb = seqlen // rb
    assert rb % GLA_CHUNK == 0
    row = lambda b, h, i: b * nb + i
    assert (2 * key_dim) % hv == 0
    v0 = 2 * key_dim // hv
    return pl.pallas_call(
        functools.partial(_gla_scan_kernel, chunk=GLA_CHUNK, scale=hk ** -0.5, eps=INNER_NORM_EPS),
        grid=(batch, n_heads, nb),
        in_specs=[
            pl.BlockSpec((rb, hk), lambda b, h, i: (row(b, h, i), h)),
            pl.BlockSpec((rb, hk), lambda b, h, i: (row(b, h, i), n_heads + h)),
            pl.BlockSpec((rb, hv), lambda b, h, i: (row(b, h, i), v0 + h)),
            pl.BlockSpec((rb, hk), lambda b, h, i: (row(b, h, i), h)),
            pl.BlockSpec((rb, hv), lambda b, h, i: (row(b, h, i), v0 + n_heads + h)),
            pl.BlockSpec((1, hv), lambda b, h, i: (0, 0)),
        ],
        out_specs=pl.BlockSpec((rb, hv), lambda b, h, i: (row(b, h, i), h)),
        out_shape=jax.ShapeDtypeStruct((t, value_dim), BF16),
        scratch_shapes=[pltpu.VMEM((hk, hv), F32)],
        compiler_params=_params("parallel", "parallel", "arbitrary"),
        name="gla_scan",
    )(qkvg, qkvg, qkvg, log_g, qkvg, norm_w.reshape(1, hv).astype(F32))


def _gla_mixer(x, batch, norm_w, in_proj, gk_up, gk_bias, inner_norm_w, out_proj):
    key_dim = gk_up.shape[1]
    value_dim = out_proj.shape[0]
    n_heads = value_dim // inner_norm_w.shape[0]
    main = 2 * key_dim + 2 * value_dim
    h = _rmsnorm(x, norm_w, NORM_EPS, BF16)
    qkvg = _matmul_wcast(h, in_proj, main, BF16)
    log_g = _gla_gate(h, in_proj, main, gk_up, gk_bias)
    o = _gla_scan(qkvg, log_g, inner_norm_w, batch, key_dim, value_dim, n_heads)
    return _matmul_residual(o, out_proj.astype(BF16), x)


def kernel(x, l0_mixer_norm, l0_ssd_in_proj, l0_ssd_conv_w, l0_ssd_conv_b, l0_ssd_dt_bias, l0_ssd_a_log, l0_ssd_d, l0_ssd_norm_w, l0_ssd_out_proj, l0_ffn_norm, l0_ffn_w_gate, l0_ffn_w_up, l0_ffn_w_down, l1_mixer_norm, l1_gla_in_proj, l1_gla_gk_up, l1_gla_gk_bias, l1_gla_norm_w, l1_gla_out_proj, l1_ffn_norm, l1_ffn_w_gate, l1_ffn_w_up, l1_ffn_w_down, final_norm):
    batch, seqlen, d = x.shape
    xf = x.reshape(batch * seqlen, d).astype(F32)
    xf = _ssd_mixer(xf, batch, l0_mixer_norm, l0_ssd_in_proj, l0_ssd_conv_w, l0_ssd_conv_b,
                    l0_ssd_dt_bias, l0_ssd_a_log, l0_ssd_d, l0_ssd_norm_w, l0_ssd_out_proj)
    xf = _ffn(xf, l0_ffn_norm, l0_ffn_w_gate, l0_ffn_w_up, l0_ffn_w_down)
    xf = _gla_mixer(xf, batch, l1_mixer_norm, l1_gla_in_proj, l1_gla_gk_up, l1_gla_gk_bias,
                    l1_gla_norm_w, l1_gla_out_proj)
    xf = _ffn(xf, l1_ffn_norm, l1_ffn_w_gate, l1_ffn_w_up, l1_ffn_w_down)
    return _rmsnorm(xf, final_norm, NORM_EPS, x.dtype).reshape(batch, seqlen, d)
```

```python
import functools

import jax
import jax.numpy as jnp
from jax import lax
from jax.experimental import pallas as pl
from jax.experimental.pallas import tpu as pltpu

F32 = jnp.float32
BF16 = jnp.bfloat16
HIGHEST = lax.Precision.HIGHEST

NORM_EPS = 1e-6
INNER_NORM_EPS = 1e-5
SSD_D_STATE = 128
SSD_CHUNK = 256
GLA_CHUNK = 64
GLA_GATE_NORMALIZER = 16.0

V7X_LANES = 128
V7X_BF16_SUBLANES = 16
V7X_VMEM_BYTES = 64 * 1024 * 1024
VMEM_LIMIT_BYTES = V7X_VMEM_BYTES - 8 * 1024 * 1024

MM_RES_VMEM_BUDGET = 46 * 1024 * 1024
WCAST_TILE_M = 2048
WCAST_TILE_N = 512
GATEUP_TILE_N = 256
ROW_TILE = 256
CONV_COL_TILE = 1024
GLA_ROW_BLOCK = 256


def _params(*semantics):
    return pltpu.CompilerParams(dimension_semantics=semantics, vmem_limit_bytes=VMEM_LIMIT_BYTES)


def _tile(dim, pref, align=V7X_LANES):
    if dim <= pref:
        return dim
    t = (pref // align) * align
    while t >= align:
        if dim % t == 0:
            return t
        t -= align
    return dim


def _silu(x):
    return x * jax.nn.sigmoid(x)


def _softplus(x):
    return jnp.maximum(x, 0.0) + jnp.log1p(jnp.exp(-jnp.abs(x)))


def _rmsnorm_kernel(x_ref, w_ref, o_ref, *, eps):
    x = x_ref[...]
    ms = jnp.mean(x * x, axis=-1, keepdims=True)
    o_ref[...] = (x * lax.rsqrt(ms + eps) * w_ref[...]).astype(o_ref.dtype)


def _rmsnorm(x, w, eps, out_dtype):
    t, d = x.shape
    tr = _tile(t, ROW_TILE, 8)
    return pl.pallas_call(
        functools.partial(_rmsnorm_kernel, eps=eps),
        grid=(t // tr,),
        in_specs=[pl.BlockSpec((tr, d), lambda i: (i, 0)), pl.BlockSpec((1, d), lambda i: (0, 0))],
        out_specs=pl.BlockSpec((tr, d), lambda i: (i, 0)),
        out_shape=jax.ShapeDtypeStruct((t, d), out_dtype),
        compiler_params=_params("parallel"),
        name="rmsnorm",
    )(x, w.reshape(1, d).astype(F32))


def _mm_wcast_kernel(a_ref, w_ref, o_ref):
    w = w_ref[...].astype(BF16)
    o_ref[...] = jnp.dot(a_ref[...], w, preferred_element_type=F32).astype(o_ref.dtype)


def _resident_rows(tm, k):
    return pl.BlockSpec((tm, k), lambda i, j: (i, 0), pipeline_mode=pl.Buffered(1))


def _matmul_wcast(a, w, n_cols, out_dtype):
    m, k = a.shape
    tm, tn = _tile(m, WCAST_TILE_M), _tile(n_cols, WCAST_TILE_N)
    return pl.pallas_call(
        _mm_wcast_kernel,
        grid=(m // tm, n_cols // tn),
        in_specs=[_resident_rows(tm, k), pl.BlockSpec((k, tn), lambda i, j: (0, j))],
        out_specs=pl.BlockSpec((tm, tn), lambda i, j: (i, j)),
        out_shape=jax.ShapeDtypeStruct((m, n_cols), out_dtype),
        compiler_params=_params("parallel", "arbitrary"),
        name="matmul_wcast",
    )(a, w)


def _mm_res_kernel(a_ref, w_ref, r_ref, o_ref):
    o_ref[...] = r_ref[...] + jnp.dot(a_ref[...], w_ref[...], preferred_element_type=F32)


def _res_tiles(m, k, n):
    best = None
    for tm in (2048, 1024, 512, 256):
        for tn in (1024, 512, 256, 128):
            if m % tm or n % tn:
                continue
            need = tm * k * 2 + 2 * k * tn * 2 + 5 * tm * tn * 4
            if need <= MM_RES_VMEM_BUDGET and (best is None or tm * tn > best[0] * best[1]):
                best = (tm, tn)
    return best if best is not None else (_tile(m, 256, 8), _tile(n, V7X_LANES))


def _matmul_residual(a, w, res):
    m, k = a.shape
    n = w.shape[1]
    tm, tn = _res_tiles(m, k, n)
    return pl.pallas_call(
        _mm_res_kernel,
        grid=(m // tm, n // tn),
        in_specs=[
            _resident_rows(tm, k),
            pl.BlockSpec((k, tn), lambda i, j: (0, j)),
            pl.BlockSpec((tm, tn), lambda i, j: (i, j)),
        ],
        out_specs=pl.BlockSpec((tm, tn), lambda i, j: (i, j)),
        out_shape=jax.ShapeDtypeStruct((m, n), F32),
        compiler_params=_params("parallel", "arbitrary"),
        name="matmul_residual",
    )(a, w, res)


def _gateup_kernel(h_ref, wg_ref, wu_ref, o_ref):
    h = h_ref[...]
    g = jnp.dot(h, wg_ref[...].astype(BF16), preferred_element_type=F32)
    u = jnp.dot(h, wu_ref[...].astype(BF16), preferred_element_type=F32)
    o_ref[...] = (_silu(g) * u).astype(o_ref.dtype)


def _ffn_gateup(h, wg, wu):
    m, k = h.shape
    n = wg.shape[1]
    tm, tn = _tile(m, WCAST_TILE_M), _tile(n, GATEUP_TILE_N)
    wspec = pl.BlockSpec((k, tn), lambda i, j: (0, j))
    return pl.pallas_call(
        _gateup_kernel,
        grid=(m // tm, n // tn),
        in_specs=[_resident_rows(tm, k), wspec, wspec],
        out_specs=pl.BlockSpec((tm, tn), lambda i, j: (i, j)),
        out_shape=jax.ShapeDtypeStruct((m, n), BF16),
        compiler_params=_params("parallel", "arbitrary"),
        name="ffn_gateup",
    )(h, wg, wu)


def _ffn(x, norm_w, w_gate, w_up, w_down):
    h = _rmsnorm(x, norm_w, NORM_EPS, BF16)
    return _matmul_residual(_ffn_gateup(h, w_gate, w_up), w_down.astype(BF16), x)


def _ssd_dt_kernel(h_ref, w_ref, bias_ref, alog_ref, dt_ref, cs_ref, cst_ref, *, n_groups):
    h = h_ref[...]
    lc = h.shape[0]
    nh = bias_ref.shape[1]
    r = nh // n_groups
    raw = jnp.dot(h, w_ref[...].astype(BF16), preferred_element_type=F32)[:, :nh]
    dt = _softplus(raw + bias_ref[...])
    la = dt * -jnp.exp(alog_ref[...])
    row = lax.broadcasted_iota(jnp.int32, (lc, lc), 0)
    col = lax.broadcasted_iota(jnp.int32, (lc, lc), 1)
    cs = jnp.dot((row >= col).astype(F32), la, precision=HIGHEST, preferred_element_type=F32)
    cst = cs.T
    for g in range(n_groups):
        dt_ref[0, g] = dt[:, g * r:(g + 1) * r]
        cs_ref[0, g] = cs[:, g * r:(g + 1) * r]
        cst_ref[0, g] = cst[g * r:(g + 1) * r, :]


def _ssd_dt(h, in_proj, col0, dt_bias, a_log, n_groups):
    t, d = h.shape
    nh = dt_bias.shape[0]
    r = nh // n_groups
    lc = SSD_CHUNK
    nblk = t // lc
    assert col0 % V7X_LANES == 0 and nh <= V7X_LANES
    vec = pl.BlockSpec((1, nh), lambda i: (0, 0))
    tok = pl.BlockSpec((1, n_groups, lc, r), lambda i: (i, 0, 0, 0))
    head = pl.BlockSpec((1, n_groups, r, lc), lambda i: (i, 0, 0, 0))
    return pl.pallas_call(
        functools.partial(_ssd_dt_kernel, n_groups=n_groups),
        grid=(nblk,),
        in_specs=[pl.BlockSpec((lc, d), lambda i: (i, 0)),
                  pl.BlockSpec((d, V7X_LANES), lambda i: (0, col0 // V7X_LANES)), vec, vec],
        out_specs=[tok, tok, head],
        out_shape=[jax.ShapeDtypeStruct((nblk, n_groups, lc, r), F32),
                   jax.ShapeDtypeStruct((nblk, n_groups, lc, r), F32),
                   jax.ShapeDtypeStruct((nblk, n_groups, r, lc), F32)],
        compiler_params=_params("parallel"),
        name="ssd_dt",
    )(h, in_proj, dt_bias.reshape(1, nh).astype(F32), a_log.reshape(1, nh).astype(F32))


def _conv_kernel(cur_ref, prev_ref, w_ref, b_ref, o_ref, buf_ref, *, halo):
    ts = cur_ref.shape[0]
    kw = w_ref.shape[0]
    prev = prev_ref[...].astype(F32)
    buf_ref[0:halo, :] = jnp.where(pl.program_id(1) == 0, 0.0, prev)
    buf_ref[halo:halo + ts, :] = cur_ref[...].astype(F32)
    acc = b_ref[...] + buf_ref[halo:halo + ts, :] * w_ref[kw - 1:kw, :]
    for j in range(kw - 1):
        off = halo - (kw - 1) + j
        acc = acc + buf_ref[off:off + ts, :] * w_ref[j:j + 1, :]
    o_ref[...] = _silu(acc).astype(o_ref.dtype)


def _ssd_conv(zxbc, conv_w, conv_b, batch, col0):
    t = zxbc.shape[0]
    c, kw = conv_w.shape
    seqlen = t // batch
    halo = V7X_BF16_SUBLANES
    ts = _tile(seqlen, ROW_TILE, halo)
    tc = _tile(c, CONV_COL_TILE)
    nt = seqlen // ts
    cb0 = col0 // tc
    assert col0 % tc == 0 and ts % halo == 0 and kw - 1 <= halo
    return pl.pallas_call(
        functools.partial(_conv_kernel, halo=halo),
        grid=(batch, nt, c // tc),
        in_specs=[
            pl.BlockSpec((ts, tc), lambda b, i, j: (b * nt + i, cb0 + j)),
            pl.BlockSpec((halo, tc),
                         lambda b, i, j: (jnp.maximum((b * nt + i) * (ts // halo) - 1, 0), cb0 + j)),
            pl.BlockSpec((kw, tc), lambda b, i, j: (0, j)),
            pl.BlockSpec((1, tc), lambda b, i, j: (0, j)),
        ],
        out_specs=pl.BlockSpec((ts, tc), lambda b, i, j: (b * nt + i, j)),
        out_shape=jax.ShapeDtypeStruct((t, c), BF16),
        scratch_shapes=[pltpu.VMEM((ts + halo, tc), F32)],
        compiler_params=_params("parallel", "parallel", "parallel"),
        name="ssd_conv",
    )(zxbc, zxbc, conv_w.T.astype(F32), conv_b.reshape(1, c).astype(F32))


def _ssd_scan_kernel(x_ref, b_ref, c_ref, z_ref, dt_ref, cs_ref, cst_ref, dskip_ref, nw_ref,
                     o_ref, state_ref, *, heads, head_dim, eps):
    @pl.when(pl.program_id(2) == 0)
    def _():
        state_ref[...] = jnp.zeros_like(state_ref)

    lc = x_ref.shape[0]
    gw = heads * head_dim
    pair = 2 * head_dim
    x = x_ref[...].astype(F32)
    bm = b_ref[...]
    cm = c_ref[...]
    dt = dt_ref[0, 0]
    cs = cs_ref[0, 0]
    cst = cst_ref[0, 0]

    e_row = lax.broadcasted_iota(jnp.int32, (3 * heads, gw), 0) % heads
    e_col = lax.broadcasted_iota(jnp.int32, (3 * heads, gw), 1) // head_dim
    expand_m = (e_row == e_col).astype(BF16)

    def expand(a):
        hi = a.astype(BF16).astype(F32)
        mid = (a - hi).astype(BF16).astype(F32)
        lo = a - hi - mid
        pieces = jnp.concatenate([hi, mid, lo], axis=1).astype(BF16)
        return jnp.dot(pieces, expand_m, preferred_element_type=F32)

    cs_end = cs[lc - 1:lc, :]
    dt_e = expand(dt)
    ecs_e = expand(jnp.exp(cs))
    dtd_e = expand(dt * jnp.exp(cs_end - cs))

    xdt = (x * dt_e).astype(BF16)
    cb = lax.dot_general(cm, bm, (((1,), (1,)), ((), ())), preferred_element_type=F32)
    causal = lax.broadcasted_iota(jnp.int32, (lc, lc), 0) >= lax.broadcasted_iota(jnp.int32, (lc, lc), 1)
    first_half = lax.broadcasted_iota(jnp.int32, (lc, pair), 1) < head_dim

    def masked_cb(r):
        seg = cs[:, r:r + 1] - cst[r:r + 1, :]
        return (cb * jnp.exp(jnp.where(causal, seg, -jnp.inf))).astype(BF16)

    ys = []
    for p in range(heads // 2):
        xp = xdt[:, p * pair:(p + 1) * pair]
        xa = jnp.where(first_half, xp, jnp.zeros_like(xp))
        xb = jnp.where(first_half, jnp.zeros_like(xp), xp)
        ys.append(jnp.dot(masked_cb(2 * p), xa, preferred_element_type=F32)
                  + jnp.dot(masked_cb(2 * p + 1), xb, preferred_element_type=F32))
    y = jnp.concatenate(ys, axis=1) if len(ys) > 1 else ys[0]

    state = state_ref[...]
    y = y + jnp.dot(cm, state.astype(BF16), preferred_element_type=F32) * ecs_e
    state_ref[...] = state * ecs_e[lc - 1:lc, :] + lax.dot_general(
        bm, (x * dtd_e).astype(BF16), (((0,), (0,)), ((), ())), preferred_element_type=F32)

    y = y + x * dskip_ref[...]
    y = y * _silu(z_ref[...].astype(F32))
    ms = jnp.mean(y * y, axis=-1, keepdims=True)
    o_ref[...] = (y * lax.rsqrt(ms + eps) * nw_ref[...]).astype(o_ref.dtype)


def _ssd_scan(zxbc, xbc, dt, cs, cst, d_skip, norm_w, batch, d_inner, n_groups, heads):
    t = zxbc.shape[0]
    lc = SSD_CHUNK
    nc = t // batch // lc
    n = SSD_D_STATE
    gw = d_inner // n_groups
    head_dim = gw // heads
    assert heads % 2 == 0 and 2 * head_dim == V7X_LANES
    row = lambda b, g, c: b * nc + c
    small = lambda shape: pl.BlockSpec(shape, lambda b, g, c: (row(b, g, c), g, 0, 0))
    return pl.pallas_call(
        functools.partial(_ssd_scan_kernel, heads=heads, head_dim=head_dim, eps=INNER_NORM_EPS),
        grid=(batch, n_groups, nc),
        in_specs=[
            pl.BlockSpec((lc, gw), lambda b, g, c: (row(b, g, c), g)),
            pl.BlockSpec((lc, n), lambda b, g, c: (row(b, g, c), d_inner // n + g)),
            pl.BlockSpec((lc, n), lambda b, g, c: (row(b, g, c), d_inner // n + n_groups + g)),
            pl.BlockSpec((lc, gw), lambda b, g, c: (row(b, g, c), g)),
            small((1, 1, lc, heads)), small((1, 1, lc, heads)), small((1, 1, heads, lc)),
            pl.BlockSpec((1, gw), lambda b, g, c: (0, g)),
            pl.BlockSpec((1, gw), lambda b, g, c: (0, g)),
        ],
        out_specs=pl.BlockSpec((lc, gw), lambda b, g, c: (row(b, g, c), g)),
        out_shape=jax.ShapeDtypeStruct((t, d_inner), BF16),
        scratch_shapes=[pltpu.VMEM((n, gw), F32)],
        compiler_params=_params("parallel", "parallel", "arbitrary"),
        name="ssd_scan",
    )(xbc, xbc, xbc, zxbc, dt, cs, cst,
      jnp.repeat(d_skip.astype(F32), head_dim).reshape(1, d_inner),
      norm_w.reshape(1, d_inner).astype(F32))


def _ssd_mixer(x, batch, norm_w, in_proj, conv_w, conv_b, dt_bias, a_log, d_skip, inner_norm_w, out_proj):
    d_inner = out_proj.shape[0]
    conv_dim = conv_w.shape[0]
    n_heads = dt_bias.shape[0]
    n_groups = (conv_dim - d_inner) // 2 // SSD_D_STATE
    main = d_inner + conv_dim
    h = _rmsnorm(x, norm_w, NORM_EPS, BF16)
    zxbc = _matmul_wcast(h, in_proj, main, BF16)
    dt, cs, cst = _ssd_dt(h, in_proj, main, dt_bias, a_log, n_groups)
    xbc = _ssd_conv(zxbc, conv_w, conv_b, batch, d_inner)
    y = _ssd_scan(zxbc, xbc, dt, cs, cst, d_skip, inner_norm_w, batch, d_inner, n_groups,
                  n_heads // n_groups)
    return _matmul_residual(y, out_proj.astype(BF16), x)


def _gla_gate_kernel(h_ref, wlow_ref, up_ref, bias_ref, o_ref):
    low = jnp.dot(h_ref[...], wlow_ref[...].astype(BF16), preferred_element_type=F32)
    x = jnp.dot(low, up_ref[...], precision=HIGHEST, preferred_element_type=F32) + bias_ref[...]
    o_ref[...] = (jnp.minimum(x, 0.0) - jnp.log1p(jnp.exp(-jnp.abs(x)))) * (1.0 / GLA_GATE_NORMALIZER)


def _gla_gate(h, w_low, gk_up, gk_bias):
    t, d = h.shape
    rank, kd = gk_up.shape
    pad = (-rank) % V7X_LANES
    w_low = jnp.pad(w_low.astype(F32), ((0, 0), (0, pad)))
    gk_up = jnp.pad(gk_up.astype(F32), ((0, pad), (0, 0)))
    tr = _tile(t, 2 * ROW_TILE, 8)
    return pl.pallas_call(
        _gla_gate_kernel,
        grid=(t // tr,),
        in_specs=[pl.BlockSpec((tr, d), lambda i: (i, 0)),
                  pl.BlockSpec((d, rank + pad), lambda i: (0, 0)),
                  pl.BlockSpec((rank + pad, kd), lambda i: (0, 0)),
                  pl.BlockSpec((1, kd), lambda i: (0, 0))],
        out_specs=pl.BlockSpec((tr, kd), lambda i: (i, 0)),
        out_shape=jax.ShapeDtypeStruct((t, kd), F32),
        compiler_params=_params("parallel"),
        name="gla_gate",
    )(h, w_low, gk_up, gk_bias.reshape(1, kd).astype(F32))


def _gla_scan_kernel(q_ref, k_ref, v_ref, g_ref, go_ref, nw_ref, o_ref, state_ref, *, chunk, scale, eps):
    @pl.when(pl.program_id(2) == 0)
    def _():
        state_ref[...] = jnp.zeros_like(state_ref)

    rows = q_ref.shape[0]
    causal = lax.broadcasted_iota(jnp.int32, (chunk, chunk), 0) >= lax.broadcasted_iota(jnp.int32, (chunk, chunk), 1)
    tril = causal.astype(F32)
    for s in range(rows // chunk):
        sl = pl.ds(s * chunk, chunk)
        q = q_ref[sl, :].astype(F32) * scale
        k = k_ref[sl, :].astype(F32)
        v = v_ref[sl, :]
        g = g_ref[sl, :]
        cs = jnp.dot(tril, g, precision=HIGHEST, preferred_element_type=F32)
        q_dec = (q * jnp.exp(cs)).astype(BF16)
        k_inv = (k * jnp.exp(-cs)).astype(BF16)
        scores = lax.dot_general(q_dec, k_inv, (((1,), (1,)), ((), ())), preferred_element_type=F32)
        scores = jnp.where(causal, scores, 0.0).astype(BF16)
        state = state_ref[...]
        o = (jnp.dot(scores, v, preferred_element_type=F32)
             + jnp.dot(q_dec, state.astype(BF16), preferred_element_type=F32))
        k_end = (k * jnp.exp(cs[chunk - 1:chunk, :] - cs)).astype(BF16)
        decay_end = jnp.exp(jnp.sum(g.T, axis=1, keepdims=True))
        state_ref[...] = state * decay_end + lax.dot_general(
            k_end, v, (((0,), (0,)), ((), ())), preferred_element_type=F32)
        ms = jnp.mean(o * o, axis=-1, keepdims=True)
        o = o * lax.rsqrt(ms + eps) * nw_ref[...]
        o_ref[sl, :] = (o * _silu(go_ref[sl, :].astype(F32))).astype(o_ref.dtype)


def _gla_scan(qkvg, log_g, norm_w, batch, key_dim, value_dim, n_heads):
    t = qkvg.shape[0]
    hk, hv = key_dim // n_heads, value_dim // n_heads
    seqlen = t // batch
    rb = _tile(seqlen, GLA_ROW_BLOCK, GLA_CHUNK)
    nb = seqlen // rb
    assert rb % GLA_CHUNK == 0
    row = lambda b, h, i: b * nb + i
    assert (2 * key_dim) % hv == 0
    v0 = 2 * key_dim // hv
    return pl.pallas_call(
        functools.partial(_gla_scan_kernel, chunk=GLA_CHUNK, scale=hk ** -0.5, eps=INNER_NORM_EPS),
        grid=(batch, n_heads, nb),
        in_specs=[
            pl.BlockSpec((rb, hk), lambda b, h, i: (row(b, h, i), h)),
            pl.BlockSpec((rb, hk), lambda b, h, i: (row(b, h, i), n_heads + h)),
            pl.BlockSpec((rb, hv), lambda b, h, i: (row(b, h, i), v0 + h)),
            pl.BlockSpec((rb, hk), lambda b, h, i: (row(b, h, i), h)),
            pl.BlockSpec((rb, hv), lambda b, h, i: (row(b, h, i), v0 + n_heads + h)),
            pl.BlockSpec((1, hv), lambda b, h, i: (0, 0)),
        ],
        out_specs=pl.BlockSpec((rb, hv), lambda b, h, i: (row(b, h, i), h)),
        out_shape=jax.ShapeDtypeStruct((t, value_dim), BF16),
        scratch_shapes=[pltpu.VMEM((hk, hv), F32)],
        compiler_params=_params("parallel", "parallel", "arbitrary"),
        name="gla_scan",
    )(qkvg, qkvg, qkvg, log_g, qkvg, norm_w.reshape(1, hv).astype(F32))


def _gla_mixer(x, batch, norm_w, in_proj, gk_up, gk_bias, inner_norm_w, out_proj):
    key_dim = gk_up.shape[1]
    value_dim = out_proj.shape[0]
    n_heads = value_dim // inner_norm_w.shape[0]
    main = 2 * key_dim + 2 * value_dim
    h = _rmsnorm(x, norm_w, NORM_EPS, BF16)
    qkvg = _matmul_wcast(h, in_proj, main, BF16)
    log_g = _gla_gate(h, in_proj[:, main:], gk_up, gk_bias)
    o = _gla_scan(qkvg, log_g, inner_norm_w, batch, key_dim, value_dim, n_heads)
    return _matmul_residual(o, out_proj.astype(BF16), x)


def kernel(x, l0_mixer_norm, l0_ssd_in_proj, l0_ssd_conv_w, l0_ssd_conv_b, l0_ssd_dt_bias, l0_ssd_a_log, l0_ssd_d, l0_ssd_norm_w, l0_ssd_out_proj, l0_ffn_norm, l0_ffn_w_gate, l0_ffn_w_up, l0_ffn_w_down, l1_mixer_norm, l1_gla_in_proj, l1_gla_gk_up, l1_gla_gk_bias, l1_gla_norm_w, l1_gla_out_proj, l1_ffn_norm, l1_ffn_w_gate, l1_ffn_w_up, l1_ffn_w_down, final_norm):
    batch, seqlen, d = x.shape
    xf = x.reshape(batch * seqlen, d).astype(F32)
    xf = _ssd_mixer(xf, batch, l0_mixer_norm, l0_ssd_in_proj, l0_ssd_conv_w, l0_ssd_conv_b,
                    l0_ssd_dt_bias, l0_ssd_a_log, l0_ssd_d, l0_ssd_norm_w, l0_ssd_out_proj)
    xf = _ffn(xf, l0_ffn_norm, l0_ffn_w_gate, l0_ffn_w_up, l0_ffn_w_down)
    xf = _gla_mixer(xf, batch, l1_mixer_norm, l1_gla_in_proj, l1_gla_gk_up, l1_gla_gk_bias,
                    l1_gla_norm_w, l1_gla_out_proj)
    xf = _ffn(xf, l1_ffn_norm, l1_ffn_w_gate, l1_ffn_w_up, l1_ffn_w_down)
    return _rmsnorm(xf, final_norm, NORM_EPS, x.dtype).reshape(batch, seqlen, d)
```

```python
import functools

import jax
import jax.numpy as jnp
from jax import lax
from jax.experimental import pallas as pl
from jax.experimental.pallas import tpu as pltpu

F32 = jnp.float32
BF16 = jnp.bfloat16
HIGHEST = lax.Precision.HIGHEST

NORM_EPS = 1e-6
INNER_NORM_EPS = 1e-5
SSD_D_STATE = 128
SSD_CHUNK = 256
GLA_CHUNK = 64
GLA_GATE_NORMALIZER = 16.0

V7X_LANES = 128
V7X_BF16_SUBLANES = 16
V7X_VMEM_BYTES = 64 * 1024 * 1024
VMEM_LIMIT_BYTES = V7X_VMEM_BYTES - 8 * 1024 * 1024

MM_RES_VMEM_BUDGET = 46 * 1024 * 1024
WCAST_TILE_M = 2048
WCAST_TILE_N = 512
GATEUP_TILE_N = 256
ROW_TILE = 256
CONV_COL_TILE = 1024
GLA_ROW_BLOCK = 256


def _params(*semantics):
    return pltpu.CompilerParams(dimension_semantics=semantics, vmem_limit_bytes=VMEM_LIMIT_BYTES)


def _tile(dim, pref, align=V7X_LANES):
    if dim <= pref:
        return dim
    t = (pref // align) * align
    while t >= align:
        if dim % t == 0:
            return t
        t -= align
    return dim


def _silu(x):
    return x * jax.nn.sigmoid(x)


def _softplus(x):
    return jnp.maximum(x, 0.0) + jnp.log1p(jnp.exp(-jnp.abs(x)))


def _rmsnorm_kernel(x_ref, w_ref, o_ref, *, eps):
    x = x_ref[...]
    ms = jnp.mean(x * x, axis=-1, keepdims=True)
    o_ref[...] = (x * lax.rsqrt(ms + eps) * w_ref[...]).astype(o_ref.dtype)


def _rmsnorm(x, w, eps, out_dtype):
    t, d = x.shape
    tr = _tile(t, ROW_TILE, 8)
    return pl.pallas_call(
        functools.partial(_rmsnorm_kernel, eps=eps),
        grid=(t // tr,),
        in_specs=[pl.BlockSpec((tr, d), lambda i: (i, 0)), pl.BlockSpec((1, d), lambda i: (0, 0))],
        out_specs=pl.BlockSpec((tr, d), lambda i: (i, 0)),
        out_shape=jax.ShapeDtypeStruct((t, d), out_dtype),
        compiler_params=_params("parallel"),
        name="rmsnorm",
    )(x, w.reshape(1, d).astype(F32))


def _mm_wcast_kernel(a_ref, w_ref, o_ref):
    w = w_ref[...].astype(BF16)
    o_ref[...] = jnp.dot(a_ref[...], w, preferred_element_type=F32).astype(o_ref.dtype)


def _resident_rows(tm, k):
    return pl.BlockSpec((tm, k), lambda i, j: (i, 0), pipeline_mode=pl.Buffered(1))


def _matmul_wcast(a, w, n_cols, out_dtype):
    m, k = a.shape
    tm, tn = _tile(m, WCAST_TILE_M), _tile(n_cols, WCAST_TILE_N)
    return pl.pallas_call(
        _mm_wcast_kernel,
        grid=(m // tm, n_cols // tn),
        in_specs=[_resident_rows(tm, k), pl.BlockSpec((k, tn), lambda i, j: (0, j))],
        out_specs=pl.BlockSpec((tm, tn), lambda i, j: (i, j)),
        out_shape=jax.ShapeDtypeStruct((m, n_cols), out_dtype),
        compiler_params=_params("parallel", "arbitrary"),
        name="matmul_wcast",
    )(a, w)


def _mm_res_kernel(a_ref, w_ref, r_ref, o_ref):
    o_ref[...] = r_ref[...] + jnp.dot(a_ref[...], w_ref[...], preferred_element_type=F32)


def _res_tiles(m, k, n):
    best = None
    for tm in (2048, 1024, 512):
        for tn in (1024, 512, 256):
            if m % tm or n % tn:
                continue
            need = k * tn * 2 + 2 * tm * k * 2 + 5 * tm * tn * 4
            if need <= MM_RES_VMEM_BUDGET and (best is None or tm * tn > best[0] * best[1]):
                best = (tm, tn)
    return best if best is not None else (_tile(m, 256, 8), _tile(n, V7X_LANES))


def _matmul_residual(a, w, res):
    m, k = a.shape
    n = w.shape[1]
    tm, tn = _res_tiles(m, k, n)
    return pl.pallas_call(
        _mm_res_kernel,
        grid=(n // tn, m // tm),
        in_specs=[
            pl.BlockSpec((tm, k), lambda j, i: (i, 0)),
            pl.BlockSpec((k, tn), lambda j, i: (0, j), pipeline_mode=pl.Buffered(1)),
            pl.BlockSpec((tm, tn), lambda j, i: (i, j)),
        ],
        out_specs=pl.BlockSpec((tm, tn), lambda j, i: (i, j)),
        out_shape=jax.ShapeDtypeStruct((m, n), F32),
        compiler_params=_params("parallel", "arbitrary"),
        name="matmul_residual",
    )(a, w, res)


def _gateup_kernel(h_ref, wg_ref, wu_ref, o_ref):
    h = h_ref[...]
    g = jnp.dot(h, wg_ref[...].astype(BF16), preferred_element_type=F32)
    u = jnp.dot(h, wu_ref[...].astype(BF16), preferred_element_type=F32)
    o_ref[...] = (_silu(g) * u).astype(o_ref.dtype)


def _ffn_gateup(h, wg, wu):
    m, k = h.shape
    n = wg.shape[1]
    tm, tn = _tile(m, WCAST_TILE_M), _tile(n, GATEUP_TILE_N)
    wspec = pl.BlockSpec((k, tn), lambda i, j: (0, j))
    return pl.pallas_call(
        _gateup_kernel,
        grid=(m // tm, n // tn),
        in_specs=[_resident_rows(tm, k), wspec, wspec],
        out_specs=pl.BlockSpec((tm, tn), lambda i, j: (i, j)),
        out_shape=jax.ShapeDtypeStruct((m, n), BF16),
        compiler_params=_params("parallel", "arbitrary"),
        name="ffn_gateup",
    )(h, wg, wu)


def _ffn(x, norm_w, w_gate, w_up, w_down):
    h = _rmsnorm(x, norm_w, NORM_EPS, BF16)
    return _matmul_residual(_ffn_gateup(h, w_gate, w_up), w_down.astype(BF16), x)


def _ssd_dt_kernel(h_ref, w_ref, bias_ref, alog_ref, dt_ref, cs_ref, cst_ref, *, n_groups):
    h = h_ref[...]
    lc = h.shape[0]
    nh = bias_ref.shape[1]
    r = nh // n_groups
    raw = jnp.dot(h, w_ref[...].astype(BF16), preferred_element_type=F32)[:, :nh]
    dt = _softplus(raw + bias_ref[...])
    la = dt * -jnp.exp(alog_ref[...])
    row = lax.broadcasted_iota(jnp.int32, (lc, lc), 0)
    col = lax.broadcasted_iota(jnp.int32, (lc, lc), 1)
    cs = jnp.dot((row >= col).astype(F32), la, precision=HIGHEST, preferred_element_type=F32)
    cst = cs.T
    for g in range(n_groups):
        dt_ref[0, g] = dt[:, g * r:(g + 1) * r]
        cs_ref[0, g] = cs[:, g * r:(g + 1) * r]
        cst_ref[0, g] = cst[g * r:(g + 1) * r, :]


def _ssd_dt(h, in_proj, col0, dt_bias, a_log, n_groups):
    t, d = h.shape
    nh = dt_bias.shape[0]
    r = nh // n_groups
    lc = SSD_CHUNK
    nblk = t // lc
    assert col0 % V7X_LANES == 0 and nh <= V7X_LANES
    vec = pl.BlockSpec((1, nh), lambda i: (0, 0))
    tok = pl.BlockSpec((1, n_groups, lc, r), lambda i: (i, 0, 0, 0))
    head = pl.BlockSpec((1, n_groups, r, lc), lambda i: (i, 0, 0, 0))
    return pl.pallas_call(
        functools.partial(_ssd_dt_kernel, n_groups=n_groups),
        grid=(nblk,),
        in_specs=[pl.BlockSpec((lc, d), lambda i: (i, 0)),
                  pl.BlockSpec((d, V7X_LANES), lambda i: (0, col0 // V7X_LANES)), vec, vec],
        out_specs=[tok, tok, head],
        out_shape=[jax.ShapeDtypeStruct((nblk, n_groups, lc, r), F32),
                   jax.ShapeDtypeStruct((nblk, n_groups, lc, r), F32),
                   jax.ShapeDtypeStruct((nblk, n_groups, r, lc), F32)],
        compiler_params=_params("parallel"),
        name="ssd_dt",
    )(h, in_proj, dt_bias.reshape(1, nh).astype(F32), a_log.reshape(1, nh).astype(F32))


def _conv_kernel(cur_ref, prev_ref, w_ref, b_ref, o_ref, buf_ref, *, halo):
    ts = cur_ref.shape[0]
    kw = w_ref.shape[0]
    prev = prev_ref[...].astype(F32)
    buf_ref[0:halo, :] = jnp.where(pl.program_id(1) == 0, 0.0, prev)
    buf_ref[halo:halo + ts, :] = cur_ref[...].astype(F32)
    acc = b_ref[...] + buf_ref[halo:halo + ts, :] * w_ref[kw - 1:kw, :]
    for j in range(kw - 1):
        off = halo - (kw - 1) + j
        acc = acc + buf_ref[off:off + ts, :] * w_ref[j:j + 1, :]
    o_ref[...] = _silu(acc).astype(o_ref.dtype)


def _ssd_conv(zxbc, conv_w, conv_b, batch, col0):
    t = zxbc.shape[0]
    c, kw = conv_w.shape
    seqlen = t // batch
    halo = V7X_BF16_SUBLANES
    ts = _tile(seqlen, ROW_TILE, halo)
    tc = _tile(c, CONV_COL_TILE)
    nt = seqlen // ts
    cb0 = col0 // tc
    assert col0 % tc == 0 and ts % halo == 0 and kw - 1 <= halo
    return pl.pallas_call(
        functools.partial(_conv_kernel, halo=halo),
        grid=(batch, nt, c // tc),
        in_specs=[
            pl.BlockSpec((ts, tc), lambda b, i, j: (b * nt + i, cb0 + j)),
            pl.BlockSpec((halo, tc),
                         lambda b, i, j: (jnp.maximum((b * nt + i) * (ts // halo) - 1, 0), cb0 + j)),
            pl.BlockSpec((kw, tc), lambda b, i, j: (0, j)),
            pl.BlockSpec((1, tc), lambda b, i, j: (0, j)),
        ],
        out_specs=pl.BlockSpec((ts, tc), lambda b, i, j: (b * nt + i, j)),
        out_shape=jax.ShapeDtypeStruct((t, c), BF16),
        scratch_shapes=[pltpu.VMEM((ts + halo, tc), F32)],
        compiler_params=_params("parallel", "parallel", "parallel"),
        name="ssd_conv",
    )(zxbc, zxbc, conv_w.T.astype(F32), conv_b.reshape(1, c).astype(F32))


def _ssd_scan_kernel(x_ref, b_ref, c_ref, z_ref, dt_ref, cs_ref, cst_ref, dskip_ref, nw_ref,
                     o_ref, state_ref, *, heads, head_dim, eps):
    @pl.when(pl.program_id(2) == 0)
    def _():
        state_ref[...] = jnp.zeros_like(state_ref)

    lc = x_ref.shape[0]
    gw = heads * head_dim
    pair = 2 * head_dim
    x = x_ref[...].astype(F32)
    bm = b_ref[...]
    cm = c_ref[...]
    dt = dt_ref[0, 0]
    cs = cs_ref[0, 0]
    cst = cst_ref[0, 0]

    e_row = lax.broadcasted_iota(jnp.int32, (3 * heads, gw), 0) % heads
    e_col = lax.broadcasted_iota(jnp.int32, (3 * heads, gw), 1) // head_dim
    expand_m = (e_row == e_col).astype(BF16)

    def expand(a):
        hi = a.astype(BF16).astype(F32)
        mid = (a - hi).astype(BF16).astype(F32)
        lo = a - hi - mid
        pieces = jnp.concatenate([hi, mid, lo], axis=1).astype(BF16)
        return jnp.dot(pieces, expand_m, preferred_element_type=F32)

    cs_end = cs[lc - 1:lc, :]
    dt_e = expand(dt)
    ecs_e = expand(jnp.exp(cs))
    dtd_e = expand(dt * jnp.exp(cs_end - cs))

    xdt = (x * dt_e).astype(BF16)
    cb = lax.dot_general(cm, bm, (((1,), (1,)), ((), ())), preferred_element_type=F32)
    causal = lax.broadcasted_iota(jnp.int32, (lc, lc), 0) >= lax.broadcasted_iota(jnp.int32, (lc, lc), 1)
    first_half = lax.broadcasted_iota(jnp.int32, (lc, pair), 1) < head_dim

    def masked_cb(r):
        seg = cs[:, r:r + 1] - cst[r:r + 1, :]
        return (cb * jnp.exp(jnp.where(causal, seg, -jnp.inf))).astype(BF16)

    ys = []
    for p in range(heads // 2):
        xp = xdt[:, p * pair:(p + 1) * pair]
        xa = jnp.where(first_half, xp, jnp.zeros_like(xp))
        xb = jnp.where(first_half, jnp.zeros_like(xp), xp)
        ys.append(jnp.dot(masked_cb(2 * p), xa, preferred_element_type=F32)
                  + jnp.dot(masked_cb(2 * p + 1), xb, preferred_element_type=F32))
    y = jnp.concatenate(ys, axis=1) if len(ys) > 1 else ys[0]

    state = state_ref[...]
    y = y + jnp.dot(cm, state.astype(BF16), preferred_element_type=F32) * ecs_e
    state_ref[...] = state * ecs_e[lc - 1:lc, :] + lax.dot_general(
        bm, (x * dtd_e).astype(BF16), (((0,), (0,)), ((), ())), preferred_element_type=F32)

    y = y + x * dskip_ref[...]
    y = y * _silu(z_ref[...].astype(F32))
    ms = jnp.mean(y * y, axis=-1, keepdims=True)
    o_ref[...] = (y * lax.rsqrt(ms + eps) * nw_ref[...]).astype(o_ref.dtype)


def _ssd_scan(zxbc, xbc, dt, cs, cst, d_skip, norm_w, batch, d_inner, n_groups, heads):
    t = zxbc.shape[0]
    lc = SSD_CHUNK
    nc = t // batch // lc
    n = SSD_D_STATE
    gw = d_inner // n_groups
    head_dim = gw // heads
    assert heads % 2 == 0 and 2 * head_dim == V7X_LANES
    row = lambda b, g, c: b * nc + c
    small = lambda shape: pl.BlockSpec(shape, lambda b, g, c: (row(b, g, c), g, 0, 0))
    return pl.pallas_call(
        functools.partial(_ssd_scan_kernel, heads=heads, head_dim=head_dim, eps=INNER_NORM_EPS),
        grid=(batch, n_groups, nc),
        in_specs=[
            pl.BlockSpec((lc, gw), lambda b, g, c: (row(b, g, c), g)),
            pl.BlockSpec((lc, n), lambda b, g, c: (row(b, g, c), d_inner // n + g)),
            pl.BlockSpec((lc, n), lambda b, g, c: (row(b, g, c), d_inner // n + n_groups + g)),
            pl.BlockSpec((lc, gw), lambda b, g, c: (row(b, g, c), g)),
            small((1, 1, lc, heads)), small((1, 1, lc, heads)), small((1, 1, heads, lc)),
            pl.BlockSpec((1, gw), lambda b, g, c: (0, g)),
            pl.BlockSpec((1, gw), lambda b, g, c: (0, g)),
        ],
        out_specs=pl.BlockSpec((lc, gw), lambda b, g, c: (row(b, g, c), g)),
        out_shape=jax.ShapeDtypeStruct((t, d_inner), BF16),
        scratch_shapes=[pltpu.VMEM((n, gw), F32)],
        compiler_params=_params("parallel", "parallel", "arbitrary"),
        name="ssd_scan",
    )(xbc, xbc, xbc, zxbc, dt, cs, cst,
      jnp.repeat(d_skip.astype(F32), head_dim).reshape(1, d_inner),
      norm_w.reshape(1, d_inner).astype(F32))


def _ssd_mixer(x, batch, norm_w, in_proj, conv_w, conv_b, dt_bias, a_log, d_skip, inner_norm_w, out_proj):
    d_inner = out_proj.shape[0]
    conv_dim = conv_w.shape[0]
    n_heads = dt_bias.shape[0]
    n_groups = (conv_dim - d_inner) // 2 // SSD_D_STATE
    main = d_inner + conv_dim
    h = _rmsnorm(x, norm_w, NORM_EPS, BF16)
    zxbc = _matmul_wcast(h, in_proj, main, BF16)
    dt, cs, cst = _ssd_dt(h, in_proj, main, dt_bias, a_log, n_groups)
    xbc = _ssd_conv(zxbc, conv_w, conv_b, batch, d_inner)
    y = _ssd_scan(zxbc, xbc, dt, cs, cst, d_skip, inner_norm_w, batch, d_inner, n_groups,
                  n_heads // n_groups)
    return _matmul_residual(y, out_proj.astype(BF16), x)


def _gla_gate_kernel(h_ref, wlow_ref, up_ref, bias_ref, o_ref):
    low = jnp.dot(h_ref[...], wlow_ref[...].astype(BF16), preferred_element_type=F32)
    x = jnp.dot(low, up_ref[...], precision=HIGHEST, preferred_element_type=F32) + bias_ref[...]
    o_ref[...] = (jnp.minimum(x, 0.0) - jnp.log1p(jnp.exp(-jnp.abs(x)))) * (1.0 / GLA_GATE_NORMALIZER)


def _gla_gate(h, w_low, gk_up, gk_bias):
    t, d = h.shape
    rank, kd = gk_up.shape
    pad = (-rank) % V7X_LANES
    w_low = jnp.pad(w_low.astype(F32), ((0, 0), (0, pad)))
    gk_up = jnp.pad(gk_up.astype(F32), ((0, pad), (0, 0)))
    tr = _tile(t, 2 * ROW_TILE, 8)
    return pl.pallas_call(
        _gla_gate_kernel,
        grid=(t // tr,),
        in_specs=[pl.BlockSpec((tr, d), lambda i: (i, 0)),
                  pl.BlockSpec((d, rank + pad), lambda i: (0, 0)),
                  pl.BlockSpec((rank + pad, kd), lambda i: (0, 0)),
                  pl.BlockSpec((1, kd), lambda i: (0, 0))],
        out_specs=pl.BlockSpec((tr, kd), lambda i: (i, 0)),
        out_shape=jax.ShapeDtypeStruct((t, kd), F32),
        compiler_params=_params("parallel"),
        name="gla_gate",
    )(h, w_low, gk_up, gk_bias.reshape(1, kd).astype(F32))


def _gla_scan_kernel(q_ref, k_ref, v_ref, g_ref, go_ref, nw_ref, o_ref, state_ref, *, chunk, scale, eps):
    @pl.when(pl.program_id(2) == 0)
    def _():
        state_ref[...] = jnp.zeros_like(state_ref)

    rows = q_ref.shape[0]
    causal = lax.broadcasted_iota(jnp.int32, (chunk, chunk), 0) >= lax.broadcasted_iota(jnp.int32, (chunk, chunk), 1)
    tril = causal.astype(F32)
    for s in range(rows // chunk):
        sl = pl.ds(s * chunk, chunk)
        q = q_ref[sl, :].astype(F32) * scale
        k = k_ref[sl, :].astype(F32)
        v = v_ref[sl, :]
        g = g_ref[sl, :]
        cs = jnp.dot(tril, g, precision=HIGHEST, preferred_element_type=F32)
        q_dec = (q * jnp.exp(cs)).astype(BF16)
        k_inv = (k * jnp.exp(-cs)).astype(BF16)
        scores = lax.dot_general(q_dec, k_inv, (((1,), (1,)), ((), ())), preferred_element_type=F32)
        scores = jnp.where(causal, scores, 0.0).astype(BF16)
        state = state_ref[...]
        o = (jnp.dot(scores, v, preferred_element_type=F32)
             + jnp.dot(q_dec, state.astype(BF16), preferred_element_type=F32))
        k_end = (k * jnp.exp(cs[chunk - 1:chunk, :] - cs)).astype(BF16)
        decay_end = jnp.exp(jnp.sum(g.T, axis=1, keepdims=True))
        state_ref[...] = state * decay_end + lax.dot_general(
            k_end, v, (((0,), (0,)), ((), ())), preferred_element_type=F32)
        ms = jnp.mean(o * o, axis=-1, keepdims=True)
        o = o * lax.rsqrt(ms + eps) * nw_ref[...]
        o_ref[sl, :] = (o * _silu(go_ref[sl, :].astype(F32))).astype(o_ref.dtype)


def _gla_scan(qkvg, log_g, norm_w, batch, key_dim, value_dim, n_heads):
    t = qkvg.shape[0]
    hk, hv = key_dim // n_heads, value_dim // n_heads
    seqlen = t // batch
    rb = _tile(seqlen, GLA_ROW_BLOCK, GLA_CHUNK)
    nb = seqlen // rb
    assert rb % GLA_CHUNK == 0
    row = lambda b, h, i: b * nb + i
    assert (2 * key_dim) % hv == 0
    v0 = 2 * key_dim // hv
    return pl.pallas_call(
        functools.partial(_gla_scan_kernel, chunk=GLA_CHUNK, scale=hk ** -0.5, eps=INNER_NORM_EPS),
        grid=(batch, n_heads, nb),
        in_specs=[
            pl.BlockSpec((rb, hk), lambda b, h, i: (row(b, h, i), h)),
            pl.BlockSpec((rb, hk), lambda b, h, i: (row(b, h, i), n_heads + h)),
            pl.BlockSpec((rb, hv), lambda b, h, i: (row(b, h, i), v0 + h)),
            pl.BlockSpec((rb, hk), lambda b, h, i: (row(b, h, i), h)),
            pl.BlockSpec((rb, hv), lambda b, h, i: (row(b, h, i), v0 + n_heads + h)),
            pl.BlockSpec((1, hv), lambda b, h, i: (0, 0)),
        ],
        out_specs=pl.BlockSpec((rb, hv), lambda b, h, i: (row(b, h, i), h)),
        out_shape=jax.ShapeDtypeStruct((t, value_dim), BF16),
        scratch_shapes=[pltpu.VMEM((hk, hv), F32)],
        compiler_params=_params("parallel", "parallel", "arbitrary"),
        name="gla_scan",
    )(qkvg, qkvg, qkvg, log_g, qkvg, norm_w.reshape(1, hv).astype(F32))


def _gla_mixer(x, batch, norm_w, in_proj, gk_up, gk_bias, inner_norm_w, out_proj):
    key_dim = gk_up.shape[1]
    value_dim = out_proj.shape[0]
    n_heads = value_dim // inner_norm_w.shape[0]
    main = 2 * key_dim + 2 * value_dim
    h = _rmsnorm(x, norm_w, NORM_EPS, BF16)
    qkvg = _matmul_wcast(h, in_proj, main, BF16)
    log_g = _gla_gate(h, in_proj[:, main:], gk_up, gk_bias)
    o = _gla_scan(qkvg, log_g, inner_norm_w, batch, key_dim, value_dim, n_heads)
    return _matmul_residual(o, out_proj.astype(BF16), x)


def kernel(x, l0_mixer_norm, l0_ssd_in_proj, l0_ssd_conv_w, l0_ssd_conv_b, l0_ssd_dt_bias, l0_ssd_a_log, l0_ssd_d, l0_ssd_norm_w, l0_ssd_out_proj, l0_ffn_norm, l0_ffn_w_gate, l0_ffn_w_up, l0_ffn_w_down, l1_mixer_norm, l1_gla_in_proj, l1_gla_gk_up, l1_gla_gk_bias, l1_gla_norm_w, l1_gla_out_proj, l1_ffn_norm, l1_ffn_w_gate, l1_ffn_w_up, l1_ffn_w_down, final_norm):
    batch, seqlen, d = x.shape
    xf = x.reshape(batch * seqlen, d).astype(F32)
    xf = _ssd_mixer(xf, batch, l0_mixer_norm, l0_ssd_in_proj, l0_ssd_conv_w, l0_ssd_conv_b,
                    l0_ssd_dt_bias, l0_ssd_a_log, l0_ssd_d, l0_ssd_norm_w, l0_ssd_out_proj)
    xf = _ffn(xf, l0_ffn_norm, l0_ffn_w_gate, l0_ffn_w_up, l0_ffn_w_down)
    xf = _gla_mixer(xf, batch, l1_mixer_norm, l1_gla_in_proj, l1_gla_gk_up, l1_gla_gk_bias,
                    l1_gla_norm_w, l1_gla_out_proj)
    xf = _ffn(xf, l1_ffn_norm, l1_ffn_w_gate, l1_ffn_w_up, l1_ffn_w_down)
    return _rmsnorm(xf, final_norm, NORM_EPS, x.dtype).reshape(batch, seqlen, d)
```

```python
import functools

import jax
import jax.numpy as jnp
from jax import lax
from jax.experimental import pallas as pl
from jax.experimental.pallas import tpu as pltpu

F32 = jnp.float32
BF16 = jnp.bfloat16
HIGHEST = lax.Precision.HIGHEST

NORM_EPS = 1e-6
INNER_NORM_EPS = 1e-5
SSD_D_STATE = 128
SSD_CHUNK = 256
GLA_CHUNK = 64
GLA_GATE_NORMALIZER = 16.0

V7X_LANES = 128
V7X_BF16_SUBLANES = 16
V7X_VMEM_BYTES = 64 * 1024 * 1024
VMEM_LIMIT_BYTES = V7X_VMEM_BYTES - 8 * 1024 * 1024

MM_RES_VMEM_BUDGET = 46 * 1024 * 1024
WCAST_TILE_M = 2048
WCAST_TILE_N = 512
GATEUP_TILE_N = 256
ROW_TILE = 256
CONV_COL_TILE = 1024
GLA_ROW_BLOCK = 256
GLA_HEADS_PER_STEP = 2


def _params(*semantics):
    return pltpu.CompilerParams(dimension_semantics=semantics, vmem_limit_bytes=VMEM_LIMIT_BYTES)


def _tile(dim, pref, align=V7X_LANES):
    if dim <= pref:
        return dim
    t = (pref // align) * align
    while t >= align:
        if dim % t == 0:
            return t
        t -= align
    return dim


def _silu(x):
    return x * jax.nn.sigmoid(x)


def _softplus(x):
    return jnp.maximum(x, 0.0) + jnp.log1p(jnp.exp(-jnp.abs(x)))


def _rmsnorm_kernel(x_ref, w_ref, o_ref, *, eps):
    x = x_ref[...]
    ms = jnp.mean(x * x, axis=-1, keepdims=True)
    o_ref[...] = (x * lax.rsqrt(ms + eps) * w_ref[...]).astype(o_ref.dtype)


def _rmsnorm(x, w, eps, out_dtype):
    t, d = x.shape
    tr = _tile(t, ROW_TILE, 8)
    return pl.pallas_call(
        functools.partial(_rmsnorm_kernel, eps=eps),
        grid=(t // tr,),
        in_specs=[pl.BlockSpec((tr, d), lambda i: (i, 0)), pl.BlockSpec((1, d), lambda i: (0, 0))],
        out_specs=pl.BlockSpec((tr, d), lambda i: (i, 0)),
        out_shape=jax.ShapeDtypeStruct((t, d), out_dtype),
        compiler_params=_params("parallel"),
        name="rmsnorm",
    )(x, w.reshape(1, d).astype(F32))


def _mm_wcast_kernel(a_ref, w_ref, o_ref):
    w = w_ref[...].astype(BF16)
    o_ref[...] = jnp.dot(a_ref[...], w, preferred_element_type=F32).astype(o_ref.dtype)


def _resident_rows(tm, k):
    return pl.BlockSpec((tm, k), lambda i, j: (i, 0), pipeline_mode=pl.Buffered(1))


def _matmul_wcast(a, w, n_cols, out_dtype):
    m, k = a.shape
    tm, tn = _tile(m, WCAST_TILE_M), _tile(n_cols, WCAST_TILE_N)
    return pl.pallas_call(
        _mm_wcast_kernel,
        grid=(m // tm, n_cols // tn),
        in_specs=[_resident_rows(tm, k), pl.BlockSpec((k, tn), lambda i, j: (0, j))],
        out_specs=pl.BlockSpec((tm, tn), lambda i, j: (i, j)),
        out_shape=jax.ShapeDtypeStruct((m, n_cols), out_dtype),
        compiler_params=_params("parallel", "arbitrary"),
        name="matmul_wcast",
    )(a, w)


def _mm_res_kernel(a_ref, w_ref, r_ref, o_ref):
    o_ref[...] = r_ref[...] + jnp.dot(a_ref[...], w_ref[...], preferred_element_type=F32)


def _res_tiles(m, k, n):
    best = None
    for tm in (2048, 1024, 512):
        for tn in (1024, 512, 256):
            if m % tm or n % tn:
                continue
            need = k * tn * 2 + 2 * tm * k * 2 + 5 * tm * tn * 4
            if need <= MM_RES_VMEM_BUDGET and (best is None or tm * tn > best[0] * best[1]):
                best = (tm, tn)
    return best if best is not None else (_tile(m, 256, 8), _tile(n, V7X_LANES))


def _matmul_residual(a, w, res):
    m, k = a.shape
    n = w.shape[1]
    tm, tn = _res_tiles(m, k, n)
    return pl.pallas_call(
        _mm_res_kernel,
        grid=(n // tn, m // tm),
        in_specs=[
            pl.BlockSpec((tm, k), lambda j, i: (i, 0)),
            pl.BlockSpec((k, tn), lambda j, i: (0, j), pipeline_mode=pl.Buffered(1)),
            pl.BlockSpec((tm, tn), lambda j, i: (i, j)),
        ],
        out_specs=pl.BlockSpec((tm, tn), lambda j, i: (i, j)),
        out_shape=jax.ShapeDtypeStruct((m, n), F32),
        compiler_params=_params("parallel", "arbitrary"),
        name="matmul_residual",
    )(a, w, res)


def _gateup_kernel(h_ref, wg_ref, wu_ref, o_ref):
    h = h_ref[...]
    g = jnp.dot(h, wg_ref[...].astype(BF16), preferred_element_type=F32)
    u = jnp.dot(h, wu_ref[...].astype(BF16), preferred_element_type=F32)
    o_ref[...] = (_silu(g) * u).astype(o_ref.dtype)


def _ffn_gateup(h, wg, wu):
    m, k = h.shape
    n = wg.shape[1]
    tm, tn = _tile(m, WCAST_TILE_M), _tile(n, GATEUP_TILE_N)
    wspec = pl.BlockSpec((k, tn), lambda i, j: (0, j))
    return pl.pallas_call(
        _gateup_kernel,
        grid=(m // tm, n // tn),
        in_specs=[_resident_rows(tm, k), wspec, wspec],
        out_specs=pl.BlockSpec((tm, tn), lambda i, j: (i, j)),
        out_shape=jax.ShapeDtypeStruct((m, n), BF16),
        compiler_params=_params("parallel", "arbitrary"),
        name="ffn_gateup",
    )(h, wg, wu)


def _ffn(x, norm_w, w_gate, w_up, w_down):
    h = _rmsnorm(x, norm_w, NORM_EPS, BF16)
    return _matmul_residual(_ffn_gateup(h, w_gate, w_up), w_down.astype(BF16), x)


def _ssd_dt_kernel(h_ref, w_ref, bias_ref, alog_ref, dt_ref, cs_ref, cst_ref, *, n_groups):
    h = h_ref[...]
    lc = h.shape[0]
    nh = bias_ref.shape[1]
    r = nh // n_groups
    raw = jnp.dot(h, w_ref[...].astype(BF16), preferred_element_type=F32)[:, :nh]
    dt = _softplus(raw + bias_ref[...])
    la = dt * -jnp.exp(alog_ref[...])
    row = lax.broadcasted_iota(jnp.int32, (lc, lc), 0)
    col = lax.broadcasted_iota(jnp.int32, (lc, lc), 1)
    cs = jnp.dot((row >= col).astype(F32), la, precision=HIGHEST, preferred_element_type=F32)
    cst = cs.T
    for g in range(n_groups):
        dt_ref[0, g] = dt[:, g * r:(g + 1) * r]
        cs_ref[0, g] = cs[:, g * r:(g + 1) * r]
        cst_ref[0, g] = cst[g * r:(g + 1) * r, :]


def _ssd_dt(h, in_proj, col0, dt_bias, a_log, n_groups):
    t, d = h.shape
    nh = dt_bias.shape[0]
    r = nh // n_groups
    lc = SSD_CHUNK
    nblk = t // lc
    assert col0 % V7X_LANES == 0 and nh <= V7X_LANES
    vec = pl.BlockSpec((1, nh), lambda i: (0, 0))
    tok = pl.BlockSpec((1, n_groups, lc, r), lambda i: (i, 0, 0, 0))
    head = pl.BlockSpec((1, n_groups, r, lc), lambda i: (i, 0, 0, 0))
    return pl.pallas_call(
        functools.partial(_ssd_dt_kernel, n_groups=n_groups),
        grid=(nblk,),
        in_specs=[pl.BlockSpec((lc, d), lambda i: (i, 0)),
                  pl.BlockSpec((d, V7X_LANES), lambda i: (0, col0 // V7X_LANES)), vec, vec],
        out_specs=[tok, tok, head],
        out_shape=[jax.ShapeDtypeStruct((nblk, n_groups, lc, r), F32),
                   jax.ShapeDtypeStruct((nblk, n_groups, lc, r), F32),
                   jax.ShapeDtypeStruct((nblk, n_groups, r, lc), F32)],
        compiler_params=_params("parallel"),
        name="ssd_dt",
    )(h, in_proj, dt_bias.reshape(1, nh).astype(F32), a_log.reshape(1, nh).astype(F32))


def _conv_kernel(cur_ref, prev_ref, w_ref, b_ref, o_ref, buf_ref, *, halo):
    ts = cur_ref.shape[0]
    kw = w_ref.shape[0]
    prev = prev_ref[...].astype(F32)
    buf_ref[0:halo, :] = jnp.where(pl.program_id(1) == 0, 0.0, prev)
    buf_ref[halo:halo + ts, :] = cur_ref[...].astype(F32)
    acc = b_ref[...] + buf_ref[halo:halo + ts, :] * w_ref[kw - 1:kw, :]
    for j in range(kw - 1):
        off = halo - (kw - 1) + j
        acc = acc + buf_ref[off:off + ts, :] * w_ref[j:j + 1, :]
    o_ref[...] = _silu(acc).astype(o_ref.dtype)


def _ssd_conv(zxbc, conv_w, conv_b, batch, col0):
    t = zxbc.shape[0]
    c, kw = conv_w.shape
    seqlen = t // batch
    halo = V7X_BF16_SUBLANES
    ts = _tile(seqlen, ROW_TILE, halo)
    tc = _tile(c, CONV_COL_TILE)
    nt = seqlen // ts
    cb0 = col0 // tc
    assert col0 % tc == 0 and ts % halo == 0 and kw - 1 <= halo
    return pl.pallas_call(
        functools.partial(_conv_kernel, halo=halo),
        grid=(batch, nt, c // tc),
        in_specs=[
            pl.BlockSpec((ts, tc), lambda b, i, j: (b * nt + i, cb0 + j)),
            pl.BlockSpec((halo, tc),
                         lambda b, i, j: (jnp.maximum((b * nt + i) * (ts // halo) - 1, 0), cb0 + j)),
            pl.BlockSpec((kw, tc), lambda b, i, j: (0, j)),
            pl.BlockSpec((1, tc), lambda b, i, j: (0, j)),
        ],
        out_specs=pl.BlockSpec((ts, tc), lambda b, i, j: (b * nt + i, j)),
        out_shape=jax.ShapeDtypeStruct((t, c), BF16),
        scratch_shapes=[pltpu.VMEM((ts + halo, tc), F32)],
        compiler_params=_params("parallel", "parallel", "parallel"),
        name="ssd_conv",
    )(zxbc, zxbc, conv_w.T.astype(F32), conv_b.reshape(1, c).astype(F32))


def _ssd_scan_kernel(x_ref, b_ref, c_ref, z_ref, dt_ref, cs_ref, cst_ref, dskip_ref, nw_ref,
                     o_ref, state_ref, *, heads, head_dim, eps):
    @pl.when(pl.program_id(2) == 0)
    def _():
        state_ref[...] = jnp.zeros_like(state_ref)

    lc = x_ref.shape[0]
    gw = heads * head_dim
    pair = 2 * head_dim
    x = x_ref[...].astype(F32)
    bm = b_ref[...]
    cm = c_ref[...]
    dt = dt_ref[0, 0]
    cs = cs_ref[0, 0]
    cst = cst_ref[0, 0]

    e_row = lax.broadcasted_iota(jnp.int32, (3 * heads, gw), 0) % heads
    e_col = lax.broadcasted_iota(jnp.int32, (3 * heads, gw), 1) // head_dim
    expand_m = (e_row == e_col).astype(BF16)

    def expand(a):
        hi = a.astype(BF16).astype(F32)
        mid = (a - hi).astype(BF16).astype(F32)
        lo = a - hi - mid
        pieces = jnp.concatenate([hi, mid, lo], axis=1).astype(BF16)
        return jnp.dot(pieces, expand_m, preferred_element_type=F32)

    cs_end = cs[lc - 1:lc, :]
    dt_e = expand(dt)
    ecs_e = expand(jnp.exp(cs))
    dtd_e = expand(dt * jnp.exp(cs_end - cs))

    xdt = (x * dt_e).astype(BF16)
    cb = lax.dot_general(cm, bm, (((1,), (1,)), ((), ())), preferred_element_type=F32)
    causal = lax.broadcasted_iota(jnp.int32, (lc, lc), 0) >= lax.broadcasted_iota(jnp.int32, (lc, lc), 1)
    first_half = lax.broadcasted_iota(jnp.int32, (lc, pair), 1) < head_dim

    def masked_cb(r):
        seg = cs[:, r:r + 1] - cst[r:r + 1, :]
        return (cb * jnp.exp(jnp.where(causal, seg, -jnp.inf))).astype(BF16)

    ys = []
    for p in range(heads // 2):
        xp = xdt[:, p * pair:(p + 1) * pair]
        xa = jnp.where(first_half, xp, jnp.zeros_like(xp))
        xb = jnp.where(first_half, jnp.zeros_like(xp), xp)
        ys.append(jnp.dot(masked_cb(2 * p), xa, preferred_element_type=F32)
                  + jnp.dot(masked_cb(2 * p + 1), xb, preferred_element_type=F32))
    y = jnp.concatenate(ys, axis=1) if len(ys) > 1 else ys[0]

    state = state_ref[...]
    y = y + jnp.dot(cm, state.astype(BF16), preferred_element_type=F32) * ecs_e
    state_ref[...] = state * ecs_e[lc - 1:lc, :] + lax.dot_general(
        bm, (x * dtd_e).astype(BF16), (((0,), (0,)), ((), ())), preferred_element_type=F32)

    y = y + x * dskip_ref[...]
    y = y * _silu(z_ref[...].astype(F32))
    ms = jnp.mean(y * y, axis=-1, keepdims=True)
    o_ref[...] = (y * lax.rsqrt(ms + eps) * nw_ref[...]).astype(o_ref.dtype)


def _ssd_scan(zxbc, xbc, dt, cs, cst, d_skip, norm_w, batch, d_inner, n_groups, heads):
    t = zxbc.shape[0]
    lc = SSD_CHUNK
    nc = t // batch // lc
    n = SSD_D_STATE
    gw = d_inner // n_groups
    head_dim = gw // heads
    assert heads % 2 == 0 and 2 * head_dim == V7X_LANES
    row = lambda b, g, c: b * nc + c
    small = lambda shape: pl.BlockSpec(shape, lambda b, g, c: (row(b, g, c), g, 0, 0))
    return pl.pallas_call(
        functools.partial(_ssd_scan_kernel, heads=heads, head_dim=head_dim, eps=INNER_NORM_EPS),
        grid=(batch, n_groups, nc),
        in_specs=[
            pl.BlockSpec((lc, gw), lambda b, g, c: (row(b, g, c), g)),
            pl.BlockSpec((lc, n), lambda b, g, c: (row(b, g, c), d_inner // n + g)),
            pl.BlockSpec((lc, n), lambda b, g, c: (row(b, g, c), d_inner // n + n_groups + g)),
            pl.BlockSpec((lc, gw), lambda b, g, c: (row(b, g, c), g)),
            small((1, 1, lc, heads)), small((1, 1, lc, heads)), small((1, 1, heads, lc)),
            pl.BlockSpec((1, gw), lambda b, g, c: (0, g)),
            pl.BlockSpec((1, gw), lambda b, g, c: (0, g)),
        ],
        out_specs=pl.BlockSpec((lc, gw), lambda b, g, c: (row(b, g, c), g)),
        out_shape=jax.ShapeDtypeStruct((t, d_inner), BF16),
        scratch_shapes=[pltpu.VMEM((n, gw), F32)],
        compiler_params=_params("parallel", "parallel", "arbitrary"),
        name="ssd_scan",
    )(xbc, xbc, xbc, zxbc, dt, cs, cst,
      jnp.repeat(d_skip.astype(F32), head_dim).reshape(1, d_inner),
      norm_w.reshape(1, d_inner).astype(F32))


def _ssd_mixer(x, batch, norm_w, in_proj, conv_w, conv_b, dt_bias, a_log, d_skip, inner_norm_w, out_proj):
    d_inner = out_proj.shape[0]
    conv_dim = conv_w.shape[0]
    n_heads = dt_bias.shape[0]
    n_groups = (conv_dim - d_inner) // 2 // SSD_D_STATE
    main = d_inner + conv_dim
    h = _rmsnorm(x, norm_w, NORM_EPS, BF16)
    zxbc = _matmul_wcast(h, in_proj, main, BF16)
    dt, cs, cst = _ssd_dt(h, in_proj, main, dt_bias, a_log, n_groups)
    xbc = _ssd_conv(zxbc, conv_w, conv_b, batch, d_inner)
    y = _ssd_scan(zxbc, xbc, dt, cs, cst, d_skip, inner_norm_w, batch, d_inner, n_groups,
                  n_heads // n_groups)
    return _matmul_residual(y, out_proj.astype(BF16), x)


def _gla_gate_kernel(h_ref, wlow_ref, up_ref, bias_ref, o_ref):
    low = jnp.dot(h_ref[...], wlow_ref[...].astype(BF16), preferred_element_type=F32)
    x = jnp.dot(low, up_ref[...], precision=HIGHEST, preferred_element_type=F32) + bias_ref[...]
    o_ref[...] = (jnp.minimum(x, 0.0) - jnp.log1p(jnp.exp(-jnp.abs(x)))) * (1.0 / GLA_GATE_NORMALIZER)


def _gla_gate(h, w_low, gk_up, gk_bias):
    t, d = h.shape
    rank, kd = gk_up.shape
    pad = (-rank) % V7X_LANES
    w_low = jnp.pad(w_low.astype(F32), ((0, 0), (0, pad)))
    gk_up = jnp.pad(gk_up.astype(F32), ((0, pad), (0, 0)))
    tr = _tile(t, 2 * ROW_TILE, 8)
    return pl.pallas_call(
        _gla_gate_kernel,
        grid=(t // tr,),
        in_specs=[pl.BlockSpec((tr, d), lambda i: (i, 0)),
                  pl.BlockSpec((d, rank + pad), lambda i: (0, 0)),
                  pl.BlockSpec((rank + pad, kd), lambda i: (0, 0)),
                  pl.BlockSpec((1, kd), lambda i: (0, 0))],
        out_specs=pl.BlockSpec((tr, kd), lambda i: (i, 0)),
        out_shape=jax.ShapeDtypeStruct((t, kd), F32),
        compiler_params=_params("parallel"),
        name="gla_gate",
    )(h, w_low, gk_up, gk_bias.reshape(1, kd).astype(F32))


def _split3_rows(a):
    hi = a.astype(BF16).astype(F32)
    mid = (a - hi).astype(BF16).astype(F32)
    return jnp.concatenate([hi, mid, a - hi - mid], axis=0).astype(BF16)


def _gla_scan_kernel(q_ref, k_ref, v_ref, g_ref, go_ref, nw_ref, o_ref, state_ref, *,
                     heads, chunk, scale, eps):
    @pl.when(pl.program_id(2) == 0)
    def _():
        state_ref[...] = jnp.zeros_like(state_ref)

    rows = q_ref.shape[0]
    nsub = rows // chunk
    hk = q_ref.shape[1] // heads
    hv = v_ref.shape[1] // heads
    shift = chunk.bit_length() - 1
    ri = lax.broadcasted_iota(jnp.int32, (rows, rows), 0)
    ci = lax.broadcasted_iota(jnp.int32, (rows, rows), 1)
    causal = ri >= ci
    in_chunk_causal = causal & (lax.shift_right_logical(ri, shift) == lax.shift_right_logical(ci, shift))
    cumsum_m = jnp.concatenate([in_chunk_causal.astype(BF16)] * 3, axis=1)
    blk = lambda a, s: a[s * chunk:(s + 1) * chunk]
    for hh in range(heads):
        kc = slice(hh * hk, (hh + 1) * hk)
        vc = slice(hh * hv, (hh + 1) * hv)
        q = q_ref[:, kc].astype(F32) * scale
        k = k_ref[:, kc].astype(F32)
        v = v_ref[:, vc]
        cs = jnp.dot(cumsum_m, _split3_rows(g_ref[:, kc]), preferred_element_type=F32)
        tot = [cs[(s + 1) * chunk - 1:(s + 1) * chunk, :] for s in range(nsub)]
        before = [jnp.zeros_like(tot[0])]
        for s in range(nsub):
            before.append(before[s] + tot[s])
        total = before[nsub]
        q_dec = q * jnp.exp(cs)
        q_dec_bf = q_dec.astype(BF16)
        k_inv = (k * jnp.exp(-cs)).astype(BF16)
        k_end = [blk(k, s) * jnp.exp(tot[s] - blk(cs, s)) for s in range(nsub)]
        q_in = jnp.concatenate([blk(q_dec, s) * jnp.exp(before[s]) for s in range(nsub)], axis=0)
        k_out = jnp.concatenate([k_end[s] * jnp.exp(total - before[s + 1]) for s in range(nsub)], axis=0)
        score_rows = []
        for s in range(nsub):
            keys = [(k_end[j] * jnp.exp(before[s] - before[j + 1])).astype(BF16) for j in range(s)]
            keys.append(blk(k_inv, s))
            keys += [jnp.zeros((chunk, hk), BF16)] * (nsub - 1 - s)
            score_rows.append(lax.dot_general(blk(q_dec_bf, s), jnp.concatenate(keys, axis=0),
                                              (((1,), (1,)), ((), ())), preferred_element_type=F32))
        scores = jnp.where(causal, jnp.concatenate(score_rows, axis=0), 0.0).astype(BF16)
        state = state_ref[hh]
        o = (jnp.dot(scores, v, preferred_element_type=F32)
             + jnp.dot(q_in.astype(BF16), state.astype(BF16), preferred_element_type=F32))
        total_col = jnp.broadcast_to(total, (V7X_LANES, hk)).T[:, 0:1]
        state_ref[hh] = state * jnp.exp(total_col) + lax.dot_general(
            k_out.astype(BF16), v, (((0,), (0,)), ((), ())), preferred_element_type=F32)
        ms = jnp.mean(o * o, axis=-1, keepdims=True)
        o = o * lax.rsqrt(ms + eps) * nw_ref[...]
        o_ref[:, vc] = (o * _silu(go_ref[:, vc].astype(F32))).astype(o_ref.dtype)


def _gla_scan(qkvg, log_g, norm_w, batch, key_dim, value_dim, n_heads):
    t = qkvg.shape[0]
    hk, hv = key_dim // n_heads, value_dim // n_heads
    seqlen = t // batch
    rb = _tile(seqlen, GLA_ROW_BLOCK, GLA_CHUNK)
    nb = seqlen // rb
    hpb = GLA_HEADS_PER_STEP if n_heads % GLA_HEADS_PER_STEP == 0 else 1
    kw, vw = hpb * hk, hpb * hv
    assert rb % GLA_CHUNK == 0 and (2 * key_dim) % vw == 0 and value_dim % vw == 0
    row = lambda b, h, i: b * nb + i
    k0 = key_dim // kw
    v0 = 2 * key_dim // vw
    g0 = (2 * key_dim + value_dim) // vw
    return pl.pallas_call(
        functools.partial(_gla_scan_kernel, heads=hpb, chunk=GLA_CHUNK, scale=hk ** -0.5,
                          eps=INNER_NORM_EPS),
        grid=(batch, n_heads // hpb, nb),
        in_specs=[
            pl.BlockSpec((rb, kw), lambda b, h, i: (row(b, h, i), h)),
            pl.BlockSpec((rb, kw), lambda b, h, i: (row(b, h, i), k0 + h)),
            pl.BlockSpec((rb, vw), lambda b, h, i: (row(b, h, i), v0 + h)),
            pl.BlockSpec((rb, kw), lambda b, h, i: (row(b, h, i), h)),
            pl.BlockSpec((rb, vw), lambda b, h, i: (row(b, h, i), g0 + h)),
            pl.BlockSpec((1, hv), lambda b, h, i: (0, 0)),
        ],
        out_specs=pl.BlockSpec((rb, vw), lambda b, h, i: (row(b, h, i), h)),
        out_shape=jax.ShapeDtypeStruct((t, value_dim), BF16),
        scratch_shapes=[pltpu.VMEM((hpb, hk, hv), F32)],
        compiler_params=_params("parallel", "parallel", "arbitrary"),
        name="gla_scan",
    )(qkvg, qkvg, qkvg, log_g, qkvg, norm_w.reshape(1, hv).astype(F32))


def _gla_mixer(x, batch, norm_w, in_proj, gk_up, gk_bias, inner_norm_w, out_proj):
    key_dim = gk_up.shape[1]
    value_dim = out_proj.shape[0]
    n_heads = value_dim // inner_norm_w.shape[0]
    main = 2 * key_dim + 2 * value_dim
    h = _rmsnorm(x, norm_w, NORM_EPS, BF16)
    qkvg = _matmul_wcast(h, in_proj, main, BF16)
    log_g = _gla_gate(h, in_proj[:, main:], gk_up, gk_bias)
    o = _gla_scan(qkvg, log_g, inner_norm_w, batch, key_dim, value_dim, n_heads)
    return _matmul_residual(o, out_proj.astype(BF16), x)


def kernel(x, l0_mixer_norm, l0_ssd_in_proj, l0_ssd_conv_w, l0_ssd_conv_b, l0_ssd_dt_bias, l0_ssd_a_log, l0_ssd_d, l0_ssd_norm_w, l0_ssd_out_proj, l0_ffn_norm, l0_ffn_w_gate, l0_ffn_w_up, l0_ffn_w_down, l1_mixer_norm, l1_gla_in_proj, l1_gla_gk_up, l1_gla_gk_bias, l1_gla_norm_w, l1_gla_out_proj, l1_ffn_norm, l1_ffn_w_gate, l1_ffn_w_up, l1_ffn_w_down, final_norm):
    batch, seqlen, d = x.shape
    xf = x.reshape(batch * seqlen, d).astype(F32)
    xf = _ssd_mixer(xf, batch, l0_mixer_norm, l0_ssd_in_proj, l0_ssd_conv_w, l0_ssd_conv_b,
                    l0_ssd_dt_bias, l0_ssd_a_log, l0_ssd_d, l0_ssd_norm_w, l0_ssd_out_proj)
    xf = _ffn(xf, l0_ffn_norm, l0_ffn_w_gate, l0_ffn_w_up, l0_ffn_w_down)
    xf = _gla_mixer(xf, batch, l1_mixer_norm, l1_gla_in_proj, l1_gla_gk_up, l1_gla_gk_bias,
                    l1_gla_norm_w, l1_gla_out_proj)
    xf = _ffn(xf, l1_ffn_norm, l1_ffn_w_gate, l1_ffn_w_up, l1_ffn_w_down)
    return _rmsnorm(xf, final_norm, NORM_EPS, x.dtype).reshape(batch, seqlen, d)
```

```python
import functools

import jax
import jax.numpy as jnp
from jax import lax
from jax.experimental import pallas as pl
from jax.experimental.pallas import tpu as pltpu

F32 = jnp.float32
BF16 = jnp.bfloat16

NORM_EPS = 1e-6
INNER_NORM_EPS = 1e-5
SSD_D_STATE = 128
SSD_CHUNK = 256
GLA_CHUNK = 64
GLA_GATE_NORMALIZER = 16.0
LOG2_E = 1.4426950408889634

V7X_LANES = 128
V7X_BF16_SUBLANES = 16
V7X_VMEM_BYTES = 64 * 1024 * 1024
VMEM_LIMIT_BYTES = V7X_VMEM_BYTES - 8 * 1024 * 1024

MM_RES_VMEM_BUDGET = 46 * 1024 * 1024
WCAST_TILE_M = 2048
WCAST_TILE_N = 512
GATEUP_TILE_N = 256
ROW_TILE = 256
CONV_COL_TILE = 1024
GLA_ROW_BLOCK = 256
GLA_HEADS_PER_STEP = 2


def _params(*semantics):
    return pltpu.CompilerParams(dimension_semantics=semantics, vmem_limit_bytes=VMEM_LIMIT_BYTES)


def _tile(dim, pref, align=V7X_LANES):
    if dim <= pref:
        return dim
    t = (pref // align) * align
    while t >= align:
        if dim % t == 0:
            return t
        t -= align
    return dim


def _silu(x):
    return x * jax.nn.sigmoid(x)


def _softplus(x):
    return jnp.maximum(x, 0.0) + jnp.log1p(jnp.exp(-jnp.abs(x)))


def _pieces3(a):
    hi = a.astype(BF16).astype(F32)
    mid = (a - hi).astype(BF16).astype(F32)
    return hi, mid, a - hi - mid


def _split3_cols(a):
    return jnp.concatenate(_pieces3(a), axis=1).astype(BF16)


def _split3_rows(a):
    return jnp.concatenate(_pieces3(a), axis=0).astype(BF16)


def _rmsnorm_kernel(x_ref, w_ref, o_ref, *, eps):
    x = x_ref[...]
    ms = jnp.mean(x * x, axis=-1, keepdims=True)
    o_ref[...] = (x * lax.rsqrt(ms + eps) * w_ref[...]).astype(o_ref.dtype)


def _rmsnorm(x, w, eps, out_dtype):
    t, d = x.shape
    tr = _tile(t, ROW_TILE, 8)
    return pl.pallas_call(
        functools.partial(_rmsnorm_kernel, eps=eps),
        grid=(t // tr,),
        in_specs=[pl.BlockSpec((tr, d), lambda i: (i, 0)), pl.BlockSpec((1, d), lambda i: (0, 0))],
        out_specs=pl.BlockSpec((tr, d), lambda i: (i, 0)),
        out_shape=jax.ShapeDtypeStruct((t, d), out_dtype),
        compiler_params=_params("parallel"),
        name="rmsnorm",
    )(x, w.reshape(1, d).astype(F32))


def _mm_wcast_kernel(a_ref, w_ref, o_ref):
    w = w_ref[...].astype(BF16)
    o_ref[...] = jnp.dot(a_ref[...], w, preferred_element_type=F32).astype(o_ref.dtype)


def _resident_rows(tm, k):
    return pl.BlockSpec((tm, k), lambda i, j: (i, 0), pipeline_mode=pl.Buffered(1))


def _matmul_wcast(a, w, n_cols, out_dtype):
    m, k = a.shape
    tm, tn = _tile(m, WCAST_TILE_M), _tile(n_cols, WCAST_TILE_N)
    return pl.pallas_call(
        _mm_wcast_kernel,
        grid=(m // tm, n_cols // tn),
        in_specs=[_resident_rows(tm, k), pl.BlockSpec((k, tn), lambda i, j: (0, j))],
        out_specs=pl.BlockSpec((tm, tn), lambda i, j: (i, j)),
        out_shape=jax.ShapeDtypeStruct((m, n_cols), out_dtype),
        compiler_params=_params("parallel", "arbitrary"),
        name="matmul_wcast",
    )(a, w)


def _mm_res_kernel(a_ref, w_ref, r_ref, o_ref):
    o_ref[...] = r_ref[...] + jnp.dot(a_ref[...], w_ref[...], preferred_element_type=F32)


def _res_tiles(m, k, n):
    best = None
    for tm in (2048, 1024, 512):
        for tn in (1024, 512, 256):
            if m % tm or n % tn:
                continue
            need = k * tn * 2 + 2 * tm * k * 2 + 5 * tm * tn * 4
            if need <= MM_RES_VMEM_BUDGET and (best is None or tm * tn > best[0] * best[1]):
                best = (tm, tn)
    return best if best is not None else (_tile(m, 256, 8), _tile(n, V7X_LANES))


def _matmul_residual(a, w, res):
    m, k = a.shape
    n = w.shape[1]
    tm, tn = _res_tiles(m, k, n)
    return pl.pallas_call(
        _mm_res_kernel,
        grid=(n // tn, m // tm),
        in_specs=[
            pl.BlockSpec((tm, k), lambda j, i: (i, 0)),
            pl.BlockSpec((k, tn), lambda j, i: (0, j), pipeline_mode=pl.Buffered(1)),
            pl.BlockSpec((tm, tn), lambda j, i: (i, j)),
        ],
        out_specs=pl.BlockSpec((tm, tn), lambda j, i: (i, j)),
        out_shape=jax.ShapeDtypeStruct((m, n), F32),
        compiler_params=_params("parallel", "arbitrary"),
        name="matmul_residual",
    )(a, w, res)


def _gateup_kernel(h_ref, wg_ref, wu_ref, o_ref):
    h = h_ref[...]
    g = jnp.dot(h, wg_ref[...].astype(BF16), preferred_element_type=F32)
    u = jnp.dot(h, wu_ref[...].astype(BF16), preferred_element_type=F32)
    o_ref[...] = (_silu(g) * u).astype(o_ref.dtype)


def _ffn_gateup(h, wg, wu):
    m, k = h.shape
    n = wg.shape[1]
    tm, tn = _tile(m, WCAST_TILE_M), _tile(n, GATEUP_TILE_N)
    wspec = pl.BlockSpec((k, tn), lambda i, j: (0, j))
    return pl.pallas_call(
        _gateup_kernel,
        grid=(m // tm, n // tn),
        in_specs=[_resident_rows(tm, k), wspec, wspec],
        out_specs=pl.BlockSpec((tm, tn), lambda i, j: (i, j)),
        out_shape=jax.ShapeDtypeStruct((m, n), BF16),
        compiler_params=_params("parallel", "arbitrary"),
        name="ffn_gateup",
    )(h, wg, wu)


def _ffn(x, norm_w, w_gate, w_up, w_down):
    h = _rmsnorm(x, norm_w, NORM_EPS, BF16)
    return _matmul_residual(_ffn_gateup(h, w_gate, w_up), w_down.astype(BF16), x)


def _ssd_dt_kernel(h_ref, w_ref, bias_ref, alog_ref, dt_ref, cs_ref, cst_ref, *, n_groups):
    h = h_ref[...]
    lc = h.shape[0]
    nh = bias_ref.shape[1]
    r = nh // n_groups
    raw = jnp.dot(h, w_ref[...].astype(BF16), preferred_element_type=F32)[:, :nh]
    dt = _softplus(raw + bias_ref[...])
    la = dt * (-LOG2_E * jnp.exp(alog_ref[...]))
    row = lax.broadcasted_iota(jnp.int32, (lc, lc), 0)
    col = lax.broadcasted_iota(jnp.int32, (lc, lc), 1)
    tril3 = jnp.concatenate([(row >= col).astype(BF16)] * 3, axis=1)
    cs = jnp.dot(tril3, _split3_rows(la), preferred_element_type=F32)
    cst = cs.T
    for g in range(n_groups):
        dt_ref[0, g] = dt[:, g * r:(g + 1) * r]
        cs_ref[0, g] = cs[:, g * r:(g + 1) * r]
        cst_ref[0, g] = cst[g * r:(g + 1) * r, :]


def _ssd_dt(h, in_proj, col0, dt_bias, a_log, n_groups):
    t, d = h.shape
    nh = dt_bias.shape[0]
    r = nh // n_groups
    lc = SSD_CHUNK
    nblk = t // lc
    assert col0 % V7X_LANES == 0 and nh <= V7X_LANES
    vec = pl.BlockSpec((1, nh), lambda i: (0, 0))
    tok = pl.BlockSpec((1, n_groups, lc, r), lambda i: (i, 0, 0, 0))
    head = pl.BlockSpec((1, n_groups, r, lc), lambda i: (i, 0, 0, 0))
    return pl.pallas_call(
        functools.partial(_ssd_dt_kernel, n_groups=n_groups),
        grid=(nblk,),
        in_specs=[pl.BlockSpec((lc, d), lambda i: (i, 0)),
                  pl.BlockSpec((d, V7X_LANES), lambda i: (0, col0 // V7X_LANES)), vec, vec],
        out_specs=[tok, tok, head],
        out_shape=[jax.ShapeDtypeStruct((nblk, n_groups, lc, r), F32),
                   jax.ShapeDtypeStruct((nblk, n_groups, lc, r), F32),
                   jax.ShapeDtypeStruct((nblk, n_groups, r, lc), F32)],
        compiler_params=_params("parallel"),
        name="ssd_dt",
    )(h, in_proj, dt_bias.reshape(1, nh).astype(F32), a_log.reshape(1, nh).astype(F32))


def _conv_kernel(cur_ref, prev_ref, w_ref, b_ref, o_ref, *, halo):
    ts = cur_ref.shape[0]
    kw = w_ref.shape[0]
    cur = cur_ref[...]
    prev = prev_ref[...]
    prev = jnp.where(pl.program_id(1) == 0, jnp.zeros_like(prev), prev)
    t = lax.broadcasted_iota(jnp.int32, (ts, ts), 0)
    s = lax.broadcasted_iota(jnp.int32, (ts, ts), 1)
    shift_m = jnp.concatenate([(s == t - k).astype(BF16) for k in range(1, kw)], axis=0)
    shifted = jnp.dot(shift_m, cur, preferred_element_type=F32)
    th = lax.broadcasted_iota(jnp.int32, (halo, halo), 0)
    sh = lax.broadcasted_iota(jnp.int32, (halo, halo), 1)
    halo_m = jnp.concatenate([(sh == th - k + halo).astype(BF16) for k in range(1, kw)], axis=0)
    from_prev = jnp.dot(halo_m, prev, preferred_element_type=F32)
    acc = b_ref[...] + cur.astype(F32) * w_ref[kw - 1:kw, :]
    top = jnp.zeros((halo, cur.shape[1]), F32)
    for k in range(1, kw):
        tap = w_ref[kw - 1 - k:kw - k, :]
        acc = acc + shifted[(k - 1) * ts:k * ts] * tap
        top = top + from_prev[(k - 1) * halo:k * halo] * tap
    acc = jnp.concatenate([acc[:halo] + top, acc[halo:]], axis=0)
    o_ref[...] = _silu(acc).astype(o_ref.dtype)


def _ssd_conv(zxbc, conv_w, conv_b, batch, col0):
    t = zxbc.shape[0]
    c, kw = conv_w.shape
    seqlen = t // batch
    halo = V7X_BF16_SUBLANES
    ts = _tile(seqlen, ROW_TILE, halo)
    tc = _tile(c, CONV_COL_TILE)
    nt = seqlen // ts
    cb0 = col0 // tc
    assert col0 % tc == 0 and ts % halo == 0 and kw - 1 <= halo
    return pl.pallas_call(
        functools.partial(_conv_kernel, halo=halo),
        grid=(batch, nt, c // tc),
        in_specs=[
            pl.BlockSpec((ts, tc), lambda b, i, j: (b * nt + i, cb0 + j)),
            pl.BlockSpec((halo, tc),
                         lambda b, i, j: (jnp.maximum((b * nt + i) * (ts // halo) - 1, 0), cb0 + j)),
            pl.BlockSpec((kw, tc), lambda b, i, j: (0, j)),
            pl.BlockSpec((1, tc), lambda b, i, j: (0, j)),
        ],
        out_specs=pl.BlockSpec((ts, tc), lambda b, i, j: (b * nt + i, j)),
        out_shape=jax.ShapeDtypeStruct((t, c), BF16),
        compiler_params=_params("parallel", "parallel", "parallel"),
        name="ssd_conv",
    )(zxbc, zxbc, conv_w.T.astype(F32), conv_b.reshape(1, c).astype(F32))


def _ssd_scan_kernel(x_ref, b_ref, c_ref, z_ref, dt_ref, cs_ref, cst_ref, dskip_ref, nw_ref,
                     o_ref, state_ref, *, heads, head_dim, eps):
    @pl.when(pl.program_id(2) == 0)
    def _():
        state_ref[...] = jnp.zeros_like(state_ref)

    lc = x_ref.shape[0]
    half = lc // 2
    gw = heads * head_dim
    pair = 2 * head_dim
    x = x_ref[...].astype(F32)
    bm = b_ref[...]
    cm = c_ref[...]
    dt = dt_ref[0, 0]
    cs = cs_ref[0, 0]
    cst = cst_ref[0, 0]

    e_row = lax.broadcasted_iota(jnp.int32, (3 * heads, gw), 0) % heads
    e_col = lax.broadcasted_iota(jnp.int32, (3 * heads, gw), 1) // head_dim
    expand_m = (e_row == e_col).astype(BF16)
    expand = lambda a: jnp.dot(_split3_cols(a), expand_m, preferred_element_type=F32)

    cs_end = cs[lc - 1:lc, :]
    dt_e = expand(dt)
    ecs_e = expand(jnp.exp2(cs))
    dtd_e = expand(dt * jnp.exp2(cs_end - cs))

    xdt = (x * dt_e).astype(BF16)
    cb = lax.dot_general(cm, bm, (((1,), (1,)), ((), ())), preferred_element_type=F32)
    cb_tl, cb_bl, cb_br = cb[:half, :half], cb[half:, :half], cb[half:, half:]
    tri = lax.broadcasted_iota(jnp.int32, (half, half), 0) >= lax.broadcasted_iota(jnp.int32, (half, half), 1)
    first_half = lax.broadcasted_iota(jnp.int32, (lc, pair), 1) < head_dim
    zero_tr = jnp.zeros((half, half), F32)

    def masked_cb(r):
        col, row = cs[:, r:r + 1], cst[r:r + 1, :]
        m_tl = cb_tl * jnp.exp2(jnp.where(tri, col[:half] - row[:, :half], -jnp.inf))
        m_bl = cb_bl * jnp.exp2(col[half:] - row[:, :half])
        m_br = cb_br * jnp.exp2(jnp.where(tri, col[half:] - row[:, half:], -jnp.inf))
        top = jnp.concatenate([m_tl, zero_tr], axis=1)
        return jnp.concatenate([top, jnp.concatenate([m_bl, m_br], axis=1)], axis=0).astype(BF16)

    ys = []
    for p in range(heads // 2):
        xp = xdt[:, p * pair:(p + 1) * pair]
        xa = jnp.where(first_half, xp, jnp.zeros_like(xp))
        xb = jnp.where(first_half, jnp.zeros_like(xp), xp)
        ys.append(jnp.dot(masked_cb(2 * p), xa, preferred_element_type=F32)
                  + jnp.dot(masked_cb(2 * p + 1), xb, preferred_element_type=F32))
    y = jnp.concatenate(ys, axis=1) if len(ys) > 1 else ys[0]

    state = state_ref[...]
    y = y + jnp.dot(cm, state.astype(BF16), preferred_element_type=F32) * ecs_e
    state_ref[...] = state * ecs_e[lc - 1:lc, :] + lax.dot_general(
        bm, (x * dtd_e).astype(BF16), (((0,), (0,)), ((), ())), preferred_element_type=F32)

    y = y + x * dskip_ref[...]
    y = y * _silu(z_ref[...].astype(F32))
    ms = jnp.mean(y * y, axis=-1, keepdims=True)
    o_ref[...] = (y * lax.rsqrt(ms + eps) * nw_ref[...]).astype(o_ref.dtype)


def _ssd_scan(zxbc, xbc, dt, cs, cst, d_skip, norm_w, batch, d_inner, n_groups, heads):
    t = zxbc.shape[0]
    lc = SSD_CHUNK
    nc = t // batch // lc
    n = SSD_D_STATE
    gw = d_inner // n_groups
    head_dim = gw // heads
    assert heads % 2 == 0 and 2 * head_dim == V7X_LANES
    row = lambda b, g, c: b * nc + c
    small = lambda shape: pl.BlockSpec(shape, lambda b, g, c: (row(b, g, c), g, 0, 0))
    return pl.pallas_call(
        functools.partial(_ssd_scan_kernel, heads=heads, head_dim=head_dim, eps=INNER_NORM_EPS),
        grid=(batch, n_groups, nc),
        in_specs=[
            pl.BlockSpec((lc, gw), lambda b, g, c: (row(b, g, c), g)),
            pl.BlockSpec((lc, n), lambda b, g, c: (row(b, g, c), d_inner // n + g)),
            pl.BlockSpec((lc, n), lambda b, g, c: (row(b, g, c), d_inner // n + n_groups + g)),
            pl.BlockSpec((lc, gw), lambda b, g, c: (row(b, g, c), g)),
            small((1, 1, lc, heads)), small((1, 1, lc, heads)), small((1, 1, heads, lc)),
            pl.BlockSpec((1, gw), lambda b, g, c: (0, g)),
            pl.BlockSpec((1, gw), lambda b, g, c: (0, g)),
        ],
        out_specs=pl.BlockSpec((lc, gw), lambda b, g, c: (row(b, g, c), g)),
        out_shape=jax.ShapeDtypeStruct((t, d_inner), BF16),
        scratch_shapes=[pltpu.VMEM((n, gw), F32)],
        compiler_params=_params("parallel", "parallel", "arbitrary"),
        name="ssd_scan",
    )(xbc, xbc, xbc, zxbc, dt, cs, cst,
      jnp.repeat(d_skip.astype(F32), head_dim).reshape(1, d_inner),
      norm_w.reshape(1, d_inner).astype(F32))


def _ssd_mixer(x, batch, norm_w, in_proj, conv_w, conv_b, dt_bias, a_log, d_skip, inner_norm_w, out_proj):
    d_inner = out_proj.shape[0]
    conv_dim = conv_w.shape[0]
    n_heads = dt_bias.shape[0]
    n_groups = (conv_dim - d_inner) // 2 // SSD_D_STATE
    main = d_inner + conv_dim
    h = _rmsnorm(x, norm_w, NORM_EPS, BF16)
    zxbc = _matmul_wcast(h, in_proj, main, BF16)
    dt, cs, cst = _ssd_dt(h, in_proj, main, dt_bias, a_log, n_groups)
    xbc = _ssd_conv(zxbc, conv_w, conv_b, batch, d_inner)
    y = _ssd_scan(zxbc, xbc, dt, cs, cst, d_skip, inner_norm_w, batch, d_inner, n_groups,
                  n_heads // n_groups)
    return _matmul_residual(y, out_proj.astype(BF16), x)


def _gla_gate_kernel(h_ref, wlow_ref, up_ref, bias_ref, o_ref):
    rank = up_ref.shape[0]
    low = jnp.dot(h_ref[...], wlow_ref[...].astype(BF16), preferred_element_type=F32)[:, :rank]
    l_hi, l_mid, l_lo = _pieces3(low)
    u_hi, u_mid, u_lo = _pieces3(up_ref[...])
    lhs = jnp.concatenate([l_hi, l_mid, l_lo, l_hi, l_mid, l_hi], axis=1).astype(BF16)
    rhs = jnp.concatenate([u_hi, u_hi, u_hi, u_mid, u_mid, u_lo], axis=0).astype(BF16)
    x = jnp.dot(lhs, rhs, preferred_element_type=F32) + bias_ref[...]
    o_ref[...] = (jnp.minimum(x, 0.0) - jnp.log1p(jnp.exp(-jnp.abs(x)))) * (1.0 / GLA_GATE_NORMALIZER)


def _gla_gate(h, w_low, gk_up, gk_bias):
    t, d = h.shape
    rank, kd = gk_up.shape
    pad = (-rank) % V7X_LANES
    w_low = jnp.pad(w_low.astype(F32), ((0, 0), (0, pad)))
    gk_up = gk_up.astype(F32)
    tr = _tile(t, 2 * ROW_TILE, 8)
    return pl.pallas_call(
        _gla_gate_kernel,
        grid=(t // tr,),
        in_specs=[pl.BlockSpec((tr, d), lambda i: (i, 0)),
                  pl.BlockSpec((d, rank + pad), lambda i: (0, 0)),
                  pl.BlockSpec((rank, kd), lambda i: (0, 0)),
                  pl.BlockSpec((1, kd), lambda i: (0, 0))],
        out_specs=pl.BlockSpec((tr, kd), lambda i: (i, 0)),
        out_shape=jax.ShapeDtypeStruct((t, kd), F32),
        compiler_params=_params("parallel"),
        name="gla_gate",
    )(h, w_low, gk_up, gk_bias.reshape(1, kd).astype(F32))


def _gla_scan_kernel(q_ref, k_ref, v_ref, g_ref, go_ref, nw_ref, o_ref, state_ref, *,
                     heads, chunk, scale, eps):
    @pl.when(pl.program_id(2) == 0)
    def _():
        state_ref[...] = jnp.zeros_like(state_ref)

    rows = q_ref.shape[0]
    nsub = rows // chunk
    hk = q_ref.shape[1] // heads
    hv = v_ref.shape[1] // heads
    shift = chunk.bit_length() - 1
    ri = lax.broadcasted_iota(jnp.int32, (rows, rows), 0)
    ci = lax.broadcasted_iota(jnp.int32, (rows, rows), 1)
    causal = ri >= ci
    in_chunk_causal = causal & (lax.shift_right_logical(ri, shift) == lax.shift_right_logical(ci, shift))
    cumsum_m = jnp.concatenate([in_chunk_causal.astype(BF16)] * 3, axis=1)
    blk = lambda a, s: a[s * chunk:(s + 1) * chunk]
    for hh in range(heads):
        kc = slice(hh * hk, (hh + 1) * hk)
        vc = slice(hh * hv, (hh + 1) * hv)
        q = q_ref[:, kc].astype(F32) * scale
        k = k_ref[:, kc].astype(F32)
        v = v_ref[:, vc]
        cs = jnp.dot(cumsum_m, _split3_rows(g_ref[:, kc]), preferred_element_type=F32)
        tot = [cs[(s + 1) * chunk - 1:(s + 1) * chunk, :] for s in range(nsub)]
        before = [jnp.zeros_like(tot[0])]
        for s in range(nsub):
            before.append(before[s] + tot[s])
        total = before[nsub]
        q_dec = q * jnp.exp(cs)
        q_dec_bf = q_dec.astype(BF16)
        k_inv = (k * jnp.exp(-cs)).astype(BF16)
        k_end = [blk(k, s) * jnp.exp(tot[s] - blk(cs, s)) for s in range(nsub)]
        q_in = jnp.concatenate([blk(q_dec, s) * jnp.exp(before[s]) for s in range(nsub)], axis=0)
        k_out = jnp.concatenate([k_end[s] * jnp.exp(total - before[s + 1]) for s in range(nsub)], axis=0)
        score_rows = []
        for s in range(nsub):
            keys = [(k_end[j] * jnp.exp(before[s] - before[j + 1])).astype(BF16) for j in range(s)]
            keys.append(blk(k_inv, s))
            keys += [jnp.zeros((chunk, hk), BF16)] * (nsub - 1 - s)
            score_rows.append(lax.dot_general(blk(q_dec_bf, s), jnp.concatenate(keys, axis=0),
                                              (((1,), (1,)), ((), ())), preferred_element_type=F32))
        scores = jnp.where(causal, jnp.concatenate(score_rows, axis=0), 0.0).astype(BF16)
        state = state_ref[hh]
        o = (jnp.dot(scores, v, preferred_element_type=F32)
             + jnp.dot(q_in.astype(BF16), state.astype(BF16), preferred_element_type=F32))
        total_col = jnp.broadcast_to(total, (V7X_LANES, hk)).T[:, 0:1]
        state_ref[hh] = state * jnp.exp(total_col) + lax.dot_general(
            k_out.astype(BF16), v, (((0,), (0,)), ((), ())), preferred_element_type=F32)
        ms = jnp.mean(o * o, axis=-1, keepdims=True)
        o = o * lax.rsqrt(ms + eps) * nw_ref[...]
        o_ref[:, vc] = (o * _silu(go_ref[:, vc].astype(F32))).astype(o_ref.dtype)


def _gla_scan(qkvg, log_g, norm_w, batch, key_dim, value_dim, n_heads):
    t = qkvg.shape[0]
    hk, hv = key_dim // n_heads, value_dim // n_heads
    seqlen = t // batch
    rb = _tile(seqlen, GLA_ROW_BLOCK, GLA_CHUNK)
    nb = seqlen // rb
    hpb = GLA_HEADS_PER_STEP if n_heads % GLA_HEADS_PER_STEP == 0 else 1
    kw, vw = hpb * hk, hpb * hv
    assert rb % GLA_CHUNK == 0 and (2 * key_dim) % vw == 0 and value_dim % vw == 0
    row = lambda b, h, i: b * nb + i
    k0 = key_dim // kw
    v0 = 2 * key_dim // vw
    g0 = (2 * key_dim + value_dim) // vw
    return pl.pallas_call(
        functools.partial(_gla_scan_kernel, heads=hpb, chunk=GLA_CHUNK, scale=hk ** -0.5,
                          eps=INNER_NORM_EPS),
        grid=(batch, n_heads // hpb, nb),
        in_specs=[
            pl.BlockSpec((rb, kw), lambda b, h, i: (row(b, h, i), h)),
            pl.BlockSpec((rb, kw), lambda b, h, i: (row(b, h, i), k0 + h)),
            pl.BlockSpec((rb, vw), lambda b, h, i: (row(b, h, i), v0 + h)),
            pl.BlockSpec((rb, kw), lambda b, h, i: (row(b, h, i), h)),
            pl.BlockSpec((rb, vw), lambda b, h, i: (row(b, h, i), g0 + h)),
            pl.BlockSpec((1, hv), lambda b, h, i: (0, 0)),
        ],
        out_specs=pl.BlockSpec((rb, vw), lambda b, h, i: (row(b, h, i), h)),
        out_shape=jax.ShapeDtypeStruct((t, value_dim), BF16),
        scratch_shapes=[pltpu.VMEM((hpb, hk, hv), F32)],
        compiler_params=_params("parallel", "parallel", "arbitrary"),
        name="gla_scan",
    )(qkvg, qkvg, qkvg, log_g, qkvg, norm_w.reshape(1, hv).astype(F32))


def _gla_mixer(x, batch, norm_w, in_proj, gk_up, gk_bias, inner_norm_w, out_proj):
    key_dim = gk_up.shape[1]
    value_dim = out_proj.shape[0]
    n_heads = value_dim // inner_norm_w.shape[0]
    main = 2 * key_dim + 2 * value_dim
    h = _rmsnorm(x, norm_w, NORM_EPS, BF16)
    qkvg = _matmul_wcast(h, in_proj, main, BF16)
    log_g = _gla_gate(h, in_proj[:, main:], gk_up, gk_bias)
    o = _gla_scan(qkvg, log_g, inner_norm_w, batch, key_dim, value_dim, n_heads)
    return _matmul_residual(o, out_proj.astype(BF16), x)


def kernel(x, l0_mixer_norm, l0_ssd_in_proj, l0_ssd_conv_w, l0_ssd_conv_b, l0_ssd_dt_bias, l0_ssd_a_log, l0_ssd_d, l0_ssd_norm_w, l0_ssd_out_proj, l0_ffn_norm, l0_ffn_w_gate, l0_ffn_w_up, l0_ffn_w_down, l1_mixer_norm, l1_gla_in_proj, l1_gla_gk_up, l1_gla_gk_bias, l1_gla_norm_w, l1_gla_out_proj, l1_ffn_norm, l1_ffn_w_gate, l1_ffn_w_up, l1_ffn_w_down, final_norm):
    batch, seqlen, d = x.shape
    xf = x.reshape(batch * seqlen, d).astype(F32)
    xf = _ssd_mixer(xf, batch, l0_mixer_norm, l0_ssd_in_proj, l0_ssd_conv_w, l0_ssd_conv_b,
                    l0_ssd_dt_bias, l0_ssd_a_log, l0_ssd_d, l0_ssd_norm_w, l0_ssd_out_proj)
    xf = _ffn(xf, l0_ffn_norm, l0_ffn_w_gate, l0_ffn_w_up, l0_ffn_w_down)
    xf = _gla_mixer(xf, batch, l1_mixer_norm, l1_gla_in_proj, l1_gla_gk_up, l1_gla_gk_bias,
                    l1_gla_norm_w, l1_gla_out_proj)
    xf = _ffn(xf, l1_ffn_norm, l1_ffn_w_gate, l1_ffn_w_up, l1_ffn_w_down)
    return _rmsnorm(xf, final_norm, NORM_EPS, x.dtype).reshape(batch, seqlen, d)
```

```python
import functools

import jax
import jax.numpy as jnp
from jax import lax
from jax.experimental import pallas as pl
from jax.experimental.pallas import tpu as pltpu

F32 = jnp.float32
BF16 = jnp.bfloat16

NORM_EPS = 1e-6
INNER_NORM_EPS = 1e-5
SSD_D_STATE = 128
SSD_CHUNK = 256
GLA_CHUNK = 64
GLA_GATE_NORMALIZER = 16.0
LOG2_E = 1.4426950408889634

V7X_LANES = 128
V7X_BF16_SUBLANES = 16
V7X_VMEM_BYTES = 64 * 1024 * 1024
VMEM_LIMIT_BYTES = V7X_VMEM_BYTES - 8 * 1024 * 1024

MM_RES_VMEM_BUDGET = 46 * 1024 * 1024
WCAST_TILE_M = 2048
WCAST_TILE_N = 512
GATEUP_TILE_N = 256
ROW_TILE = 256
GLA_ROW_BLOCK = 256
GLA_HEADS_PER_STEP = 2


def _params(*semantics):
    return pltpu.CompilerParams(dimension_semantics=semantics, vmem_limit_bytes=VMEM_LIMIT_BYTES)


def _tile(dim, pref, align=V7X_LANES):
    if dim <= pref:
        return dim
    t = (pref // align) * align
    while t >= align:
        if dim % t == 0:
            return t
        t -= align
    return dim


def _silu(x):
    return x * jax.nn.sigmoid(x)


def _softplus(x):
    return jnp.maximum(x, 0.0) + jnp.log1p(jnp.exp(-jnp.abs(x)))


def _pieces3(a):
    hi = a.astype(BF16).astype(F32)
    mid = (a - hi).astype(BF16).astype(F32)
    return hi, mid, a - hi - mid


def _split3_cols(a):
    return jnp.concatenate(_pieces3(a), axis=1).astype(BF16)


def _split3_rows(a):
    return jnp.concatenate(_pieces3(a), axis=0).astype(BF16)


def _rmsnorm_kernel(x_ref, w_ref, o_ref, *, eps):
    x = x_ref[...]
    ms = jnp.mean(x * x, axis=-1, keepdims=True)
    o_ref[...] = (x * lax.rsqrt(ms + eps) * w_ref[...]).astype(o_ref.dtype)


def _rmsnorm(x, w, eps, out_dtype):
    t, d = x.shape
    tr = _tile(t, ROW_TILE, 8)
    return pl.pallas_call(
        functools.partial(_rmsnorm_kernel, eps=eps),
        grid=(t // tr,),
        in_specs=[pl.BlockSpec((tr, d), lambda i: (i, 0)), pl.BlockSpec((1, d), lambda i: (0, 0))],
        out_specs=pl.BlockSpec((tr, d), lambda i: (i, 0)),
        out_shape=jax.ShapeDtypeStruct((t, d), out_dtype),
        compiler_params=_params("parallel"),
        name="rmsnorm",
    )(x, w.reshape(1, d).astype(F32))


def _mm_wcast_kernel(a_ref, w_ref, o_ref):
    w = w_ref[...].astype(BF16)
    o_ref[...] = jnp.dot(a_ref[...], w, preferred_element_type=F32).astype(o_ref.dtype)


def _resident_rows(tm, k):
    return pl.BlockSpec((tm, k), lambda i, j: (i, 0), pipeline_mode=pl.Buffered(1))


def _matmul_wcast(a, w, n_cols, out_dtype):
    m, k = a.shape
    tm, tn = _tile(m, WCAST_TILE_M), _tile(n_cols, WCAST_TILE_N)
    return pl.pallas_call(
        _mm_wcast_kernel,
        grid=(m // tm, n_cols // tn),
        in_specs=[_resident_rows(tm, k), pl.BlockSpec((k, tn), lambda i, j: (0, j))],
        out_specs=pl.BlockSpec((tm, tn), lambda i, j: (i, j)),
        out_shape=jax.ShapeDtypeStruct((m, n_cols), out_dtype),
        compiler_params=_params("parallel", "arbitrary"),
        name="matmul_wcast",
    )(a, w)


def _mm_wcast_conv_kernel(a_ref, w_ref, cw_ref, cb_ref, o_ref, buf_ref, carry_ref, *, halo, tiles_per_seq):
    i, j = pl.program_id(0), pl.program_id(1)
    tm = a_ref.shape[0]
    kw = cw_ref.shape[0]

    @pl.when((i == 0) & (j == 0))
    def _():
        carry_ref[...] = jnp.zeros_like(carry_ref)

    acc = jnp.dot(a_ref[...], w_ref[...].astype(BF16), preferred_element_type=F32)
    buf_ref[0:halo, :] = jnp.where(lax.rem(i, tiles_per_seq) == 0, 0.0, carry_ref[j])
    buf_ref[halo:halo + tm, :] = acc
    carry_ref[j] = acc[tm - halo:, :]
    out = cb_ref[...] + acc * cw_ref[kw - 1:kw, :]
    for tap in range(kw - 1):
        off = halo - (kw - 1) + tap
        out = out + buf_ref[off:off + tm, :] * cw_ref[tap:tap + 1, :]
    o_ref[...] = _silu(out).astype(o_ref.dtype)


def _matmul_wcast_conv(a, w, col0, conv_w, conv_b, seqlen):
    m, k = a.shape
    c, kw = conv_w.shape
    halo = V7X_BF16_SUBLANES
    tm, tn = _tile(seqlen, WCAST_TILE_M, halo), _tile(c, WCAST_TILE_N)
    assert col0 % tn == 0 and seqlen % tm == 0 and kw - 1 <= halo
    return pl.pallas_call(
        functools.partial(_mm_wcast_conv_kernel, halo=halo, tiles_per_seq=seqlen // tm),
        grid=(m // tm, c // tn),
        in_specs=[_resident_rows(tm, k),
                  pl.BlockSpec((k, tn), lambda i, j: (0, col0 // tn + j)),
                  pl.BlockSpec((kw, tn), lambda i, j: (0, j)),
                  pl.BlockSpec((1, tn), lambda i, j: (0, j))],
        out_specs=pl.BlockSpec((tm, tn), lambda i, j: (i, j)),
        out_shape=jax.ShapeDtypeStruct((m, c), BF16),
        scratch_shapes=[pltpu.VMEM((tm + halo, tn), F32), pltpu.VMEM((c // tn, halo, tn), F32)],
        compiler_params=_params("arbitrary", "arbitrary"),
        name="matmul_wcast_conv",
    )(a, w, conv_w.T.astype(F32), conv_b.reshape(1, c).astype(F32))


def _mm_res_kernel(a_ref, w_ref, r_ref, o_ref):
    o_ref[...] = r_ref[...] + jnp.dot(a_ref[...], w_ref[...], preferred_element_type=F32)


def _res_tiles(m, k, n):
    best = None
    for tm in (2048, 1024, 512):
        for tn in (1024, 512, 256):
            if m % tm or n % tn:
                continue
            need = k * tn * 2 + 2 * tm * k * 2 + 5 * tm * tn * 4
            if need <= MM_RES_VMEM_BUDGET and (best is None or tm * tn > best[0] * best[1]):
                best = (tm, tn)
    return best if best is not None else (_tile(m, 256, 8), _tile(n, V7X_LANES))


def _matmul_residual(a, w, res):
    m, k = a.shape
    n = w.shape[1]
    tm, tn = _res_tiles(m, k, n)
    return pl.pallas_call(
        _mm_res_kernel,
        grid=(n // tn, m // tm),
        in_specs=[
            pl.BlockSpec((tm, k), lambda j, i: (i, 0)),
            pl.BlockSpec((k, tn), lambda j, i: (0, j), pipeline_mode=pl.Buffered(1)),
            pl.BlockSpec((tm, tn), lambda j, i: (i, j)),
        ],
        out_specs=pl.BlockSpec((tm, tn), lambda j, i: (i, j)),
        out_shape=jax.ShapeDtypeStruct((m, n), F32),
        compiler_params=_params("parallel", "arbitrary"),
        name="matmul_residual",
    )(a, w, res)


def _gateup_kernel(h_ref, wg_ref, wu_ref, o_ref):
    h = h_ref[...]
    g = jnp.dot(h, wg_ref[...].astype(BF16), preferred_element_type=F32)
    u = jnp.dot(h, wu_ref[...].astype(BF16), preferred_element_type=F32)
    o_ref[...] = (_silu(g) * u).astype(o_ref.dtype)


def _ffn_gateup(h, wg, wu):
    m, k = h.shape
    n = wg.shape[1]
    tm, tn = _tile(m, WCAST_TILE_M), _tile(n, GATEUP_TILE_N)
    wspec = pl.BlockSpec((k, tn), lambda i, j: (0, j))
    return pl.pallas_call(
        _gateup_kernel,
        grid=(m // tm, n // tn),
        in_specs=[_resident_rows(tm, k), wspec, wspec],
        out_specs=pl.BlockSpec((tm, tn), lambda i, j: (i, j)),
        out_shape=jax.ShapeDtypeStruct((m, n), BF16),
        compiler_params=_params("parallel", "arbitrary"),
        name="ffn_gateup",
    )(h, wg, wu)


def _ffn(x, norm_w, w_gate, w_up, w_down):
    h = _rmsnorm(x, norm_w, NORM_EPS, BF16)
    return _matmul_residual(_ffn_gateup(h, w_gate, w_up), w_down.astype(BF16), x)


def _ssd_dt_kernel(h_ref, w_ref, bias_ref, alog_ref, dt_ref, cs_ref, cst_ref, *, n_groups):
    h = h_ref[...]
    lc = h.shape[0]
    nh = bias_ref.shape[1]
    r = nh // n_groups
    raw = jnp.dot(h, w_ref[...].astype(BF16), preferred_element_type=F32)[:, :nh]
    dt = _softplus(raw + bias_ref[...])
    la = dt * (-LOG2_E * jnp.exp(alog_ref[...]))
    row = lax.broadcasted_iota(jnp.int32, (lc, lc), 0)
    col = lax.broadcasted_iota(jnp.int32, (lc, lc), 1)
    tril3 = jnp.concatenate([(row >= col).astype(BF16)] * 3, axis=1)
    cs = jnp.dot(tril3, _split3_rows(la), preferred_element_type=F32)
    cst = cs.T
    for g in range(n_groups):
        dt_ref[0, g] = dt[:, g * r:(g + 1) * r]
        cs_ref[0, g] = cs[:, g * r:(g + 1) * r]
        cst_ref[0, g] = cst[g * r:(g + 1) * r, :]


def _ssd_dt(h, in_proj, col0, dt_bias, a_log, n_groups):
    t, d = h.shape
    nh = dt_bias.shape[0]
    r = nh // n_groups
    lc = SSD_CHUNK
    nblk = t // lc
    assert col0 % V7X_LANES == 0 and nh <= V7X_LANES
    vec = pl.BlockSpec((1, nh), lambda i: (0, 0))
    tok = pl.BlockSpec((1, n_groups, lc, r), lambda i: (i, 0, 0, 0))
    head = pl.BlockSpec((1, n_groups, r, lc), lambda i: (i, 0, 0, 0))
    return pl.pallas_call(
        functools.partial(_ssd_dt_kernel, n_groups=n_groups),
        grid=(nblk,),
        in_specs=[pl.BlockSpec((lc, d), lambda i: (i, 0)),
                  pl.BlockSpec((d, V7X_LANES), lambda i: (0, col0 // V7X_LANES)), vec, vec],
        out_specs=[tok, tok, head],
        out_shape=[jax.ShapeDtypeStruct((nblk, n_groups, lc, r), F32),
                   jax.ShapeDtypeStruct((nblk, n_groups, lc, r), F32),
                   jax.ShapeDtypeStruct((nblk, n_groups, r, lc), F32)],
        compiler_params=_params("parallel"),
        name="ssd_dt",
    )(h, in_proj, dt_bias.reshape(1, nh).astype(F32), a_log.reshape(1, nh).astype(F32))


def _ssd_scan_kernel(x_ref, b_ref, c_ref, z_ref, dt_ref, cs_ref, cst_ref, dskip_ref, nw_ref,
                     o_ref, state_ref, *, heads, head_dim, eps):
    @pl.when(pl.program_id(2) == 0)
    def _():
        state_ref[...] = jnp.zeros_like(state_ref)

    lc = x_ref.shape[0]
    half = lc // 2
    gw = heads * head_dim
    pair = 2 * head_dim
    x = x_ref[...].astype(F32)
    bm = b_ref[...]
    cm = c_ref[...]
    dt = dt_ref[0, 0]
    cs = cs_ref[0, 0]
    cst = cst_ref[0, 0]

    e_row = lax.broadcasted_iota(jnp.int32, (3 * heads, gw), 0) % heads
    e_col = lax.broadcasted_iota(jnp.int32, (3 * heads, gw), 1) // head_dim
    expand_m = (e_row == e_col).astype(BF16)
    expand = lambda a: jnp.dot(_split3_cols(a), expand_m, preferred_element_type=F32)

    cs_end = cs[lc - 1:lc, :]
    dt_e = expand(dt)
    ecs_e = expand(jnp.exp2(cs))
    dtd_e = expand(dt * jnp.exp2(cs_end - cs))

    xdt = (x * dt_e).astype(BF16)
    cb = lax.dot_general(cm, bm, (((1,), (1,)), ((), ())), preferred_element_type=F32)
    cb_tl, cb_bl, cb_br = cb[:half, :half], cb[half:, :half], cb[half:, half:]
    tri = lax.broadcasted_iota(jnp.int32, (half, half), 0) >= lax.broadcasted_iota(jnp.int32, (half, half), 1)
    first_half = lax.broadcasted_iota(jnp.int32, (lc, pair), 1) < head_dim
    zero_tr = jnp.zeros((half, half), F32)

    def masked_cb(r):
        col, row = cs[:, r:r + 1], cst[r:r + 1, :]
        m_tl = cb_tl * jnp.exp2(jnp.where(tri, col[:half] - row[:, :half], -jnp.inf))
        m_bl = cb_bl * jnp.exp2(col[half:] - row[:, :half])
        m_br = cb_br * jnp.exp2(jnp.where(tri, col[half:] - row[:, half:], -jnp.inf))
        top = jnp.concatenate([m_tl, zero_tr], axis=1)
        return jnp.concatenate([top, jnp.concatenate([m_bl, m_br], axis=1)], axis=0).astype(BF16)

    ys = []
    for p in range(heads // 2):
        xp = xdt[:, p * pair:(p + 1) * pair]
        xa = jnp.where(first_half, xp, jnp.zeros_like(xp))
        xb = jnp.where(first_half, jnp.zeros_like(xp), xp)
        ys.append(jnp.dot(masked_cb(2 * p), xa, preferred_element_type=F32)
                  + jnp.dot(masked_cb(2 * p + 1), xb, preferred_element_type=F32))
    y = jnp.concatenate(ys, axis=1) if len(ys) > 1 else ys[0]

    state = state_ref[...]
    y = y + jnp.dot(cm, state.astype(BF16), preferred_element_type=F32) * ecs_e
    state_ref[...] = state * ecs_e[lc - 1:lc, :] + lax.dot_general(
        bm, (x * dtd_e).astype(BF16), (((0,), (0,)), ((), ())), preferred_element_type=F32)

    y = y + x * dskip_ref[...]
    y = y * _silu(z_ref[...].astype(F32))
    ms = jnp.mean(y * y, axis=-1, keepdims=True)
    o_ref[...] = (y * lax.rsqrt(ms + eps) * nw_ref[...]).astype(o_ref.dtype)


def _ssd_scan(z, xbc, dt, cs, cst, d_skip, norm_w, batch, d_inner, n_groups, heads):
    t = z.shape[0]
    lc = SSD_CHUNK
    nc = t // batch // lc
    n = SSD_D_STATE
    gw = d_inner // n_groups
    head_dim = gw // heads
    assert heads % 2 == 0 and 2 * head_dim == V7X_LANES
    row = lambda b, g, c: b * nc + c
    small = lambda shape: pl.BlockSpec(shape, lambda b, g, c: (row(b, g, c), g, 0, 0))
    return pl.pallas_call(
        functools.partial(_ssd_scan_kernel, heads=heads, head_dim=head_dim, eps=INNER_NORM_EPS),
        grid=(batch, n_groups, nc),
        in_specs=[
            pl.BlockSpec((lc, gw), lambda b, g, c: (row(b, g, c), g)),
            pl.BlockSpec((lc, n), lambda b, g, c: (row(b, g, c), d_inner // n + g)),
            pl.BlockSpec((lc, n), lambda b, g, c: (row(b, g, c), d_inner // n + n_groups + g)),
            pl.BlockSpec((lc, gw), lambda b, g, c: (row(b, g, c), g)),
            small((1, 1, lc, heads)), small((1, 1, lc, heads)), small((1, 1, heads, lc)),
            pl.BlockSpec((1, gw), lambda b, g, c: (0, g)),
            pl.BlockSpec((1, gw), lambda b, g, c: (0, g)),
        ],
        out_specs=pl.BlockSpec((lc, gw), lambda b, g, c: (row(b, g, c), g)),
        out_shape=jax.ShapeDtypeStruct((t, d_inner), BF16),
        scratch_shapes=[pltpu.VMEM((n, gw), F32)],
        compiler_params=_params("parallel", "parallel", "arbitrary"),
        name="ssd_scan",
    )(xbc, xbc, xbc, z, dt, cs, cst,
      jnp.repeat(d_skip.astype(F32), head_dim).reshape(1, d_inner),
      norm_w.reshape(1, d_inner).astype(F32))


def _ssd_mixer(x, batch, norm_w, in_proj, conv_w, conv_b, dt_bias, a_log, d_skip, inner_norm_w, out_proj):
    d_inner = out_proj.shape[0]
    conv_dim = conv_w.shape[0]
    n_heads = dt_bias.shape[0]
    n_groups = (conv_dim - d_inner) // 2 // SSD_D_STATE
    main = d_inner + conv_dim
    h = _rmsnorm(x, norm_w, NORM_EPS, BF16)
    z = _matmul_wcast(h, in_proj, d_inner, BF16)
    xbc = _matmul_wcast_conv(h, in_proj, d_inner, conv_w, conv_b, x.shape[0] // batch)
    dt, cs, cst = _ssd_dt(h, in_proj, main, dt_bias, a_log, n_groups)
    y = _ssd_scan(z, xbc, dt, cs, cst, d_skip, inner_norm_w, batch, d_inner, n_groups,
                  n_heads // n_groups)
    return _matmul_residual(y, out_proj.astype(BF16), x)


def _gla_gate_kernel(h_ref, wlow_ref, up_ref, bias_ref, o_ref):
    rank = up_ref.shape[0]
    low = jnp.dot(h_ref[...], wlow_ref[...].astype(BF16), preferred_element_type=F32)[:, :rank]
    l_hi, l_mid, l_lo = _pieces3(low)
    u_hi, u_mid, u_lo = _pieces3(up_ref[...])
    lhs = jnp.concatenate([l_hi, l_mid, l_lo, l_hi, l_mid, l_hi], axis=1).astype(BF16)
    rhs = jnp.concatenate([u_hi, u_hi, u_hi, u_mid, u_mid, u_lo], axis=0).astype(BF16)
    x = jnp.dot(lhs, rhs, preferred_element_type=F32) + bias_ref[...]
    o_ref[...] = (jnp.minimum(x, 0.0) - jnp.log1p(jnp.exp(-jnp.abs(x)))) * (1.0 / GLA_GATE_NORMALIZER)


def _gla_gate(h, w_low, gk_up, gk_bias):
    t, d = h.shape
    rank, kd = gk_up.shape
    pad = (-rank) % V7X_LANES
    w_low = jnp.pad(w_low.astype(F32), ((0, 0), (0, pad)))
    gk_up = gk_up.astype(F32)
    tr = _tile(t, 2 * ROW_TILE, 8)
    return pl.pallas_call(
        _gla_gate_kernel,
        grid=(t // tr,),
        in_specs=[pl.BlockSpec((tr, d), lambda i: (i, 0)),
                  pl.BlockSpec((d, rank + pad), lambda i: (0, 0)),
                  pl.BlockSpec((rank, kd), lambda i: (0, 0)),
                  pl.BlockSpec((1, kd), lambda i: (0, 0))],
        out_specs=pl.BlockSpec((tr, kd), lambda i: (i, 0)),
        out_shape=jax.ShapeDtypeStruct((t, kd), F32),
        compiler_params=_params("parallel"),
        name="gla_gate",
    )(h, w_low, gk_up, gk_bias.reshape(1, kd).astype(F32))


def _gla_scan_kernel(q_ref, k_ref, v_ref, g_ref, go_ref, nw_ref, o_ref, state_ref, *,
                     heads, chunk, scale, eps):
    @pl.when(pl.program_id(2) == 0)
    def _():
        state_ref[...] = jnp.zeros_like(state_ref)

    rows = q_ref.shape[0]
    nsub = rows // chunk
    hk = q_ref.shape[1] // heads
    hv = v_ref.shape[1] // heads
    shift = chunk.bit_length() - 1
    ri = lax.broadcasted_iota(jnp.int32, (rows, rows), 0)
    ci = lax.broadcasted_iota(jnp.int32, (rows, rows), 1)
    causal = ri >= ci
    in_chunk_causal = causal & (lax.shift_right_logical(ri, shift) == lax.shift_right_logical(ci, shift))
    cumsum_m = jnp.concatenate([in_chunk_causal.astype(BF16)] * 3, axis=1)
    blk = lambda a, s: a[s * chunk:(s + 1) * chunk]
    for hh in range(heads):
        kc = slice(hh * hk, (hh + 1) * hk)
        vc = slice(hh * hv, (hh + 1) * hv)
        q = q_ref[:, kc].astype(F32) * scale
        k = k_ref[:, kc].astype(F32)
        v = v_ref[:, vc]
        cs = jnp.dot(cumsum_m, _split3_rows(g_ref[:, kc]), preferred_element_type=F32)
        tot = [cs[(s + 1) * chunk - 1:(s + 1) * chunk, :] for s in range(nsub)]
        before = [jnp.zeros_like(tot[0])]
        for s in range(nsub):
            before.append(before[s] + tot[s])
        total = before[nsub]
        q_dec = q * jnp.exp(cs)
        q_dec_bf = q_dec.astype(BF16)
        k_inv = (k * jnp.exp(-cs)).astype(BF16)
        k_end = [blk(k, s) * jnp.exp(tot[s] - blk(cs, s)) for s in range(nsub)]
        q_in = jnp.concatenate([blk(q_dec, s) * jnp.exp(before[s]) for s in range(nsub)], axis=0)
        k_out = jnp.concatenate([k_end[s] * jnp.exp(total - before[s + 1]) for s in range(nsub)], axis=0)
        score_rows = []
        for s in range(nsub):
            keys = [(k_end[j] * jnp.exp(before[s] - before[j + 1])).astype(BF16) for j in range(s)]
            keys.append(blk(k_inv, s))
            keys += [jnp.zeros((chunk, hk), BF16)] * (nsub - 1 - s)
            score_rows.append(lax.dot_general(blk(q_dec_bf, s), jnp.concatenate(keys, axis=0),
                                              (((1,), (1,)), ((), ())), preferred_element_type=F32))
        scores = jnp.where(causal, jnp.concatenate(score_rows, axis=0), 0.0).astype(BF16)
        state = state_ref[hh]
        o = (jnp.dot(scores, v, preferred_element_type=F32)
             + jnp.dot(q_in.astype(BF16), state.astype(BF16), preferred_element_type=F32))
        total_col = jnp.broadcast_to(total, (V7X_LANES, hk)).T[:, 0:1]
        state_ref[hh] = state * jnp.exp(total_col) + lax.dot_general(
            k_out.astype(BF16), v, (((0,), (0,)), ((), ())), preferred_element_type=F32)
        ms = jnp.mean(o * o, axis=-1, keepdims=True)
        o = o * lax.rsqrt(ms + eps) * nw_ref[...]
        o_ref[:, vc] = (o * _silu(go_ref[:, vc].astype(F32))).astype(o_ref.dtype)


def _gla_scan(qkvg, log_g, norm_w, batch, key_dim, value_dim, n_heads):
    t = qkvg.shape[0]
    hk, hv = key_dim // n_heads, value_dim // n_heads
    seqlen = t // batch
    rb = _tile(seqlen, GLA_ROW_BLOCK, GLA_CHUNK)
    nb = seqlen // rb
    hpb = GLA_HEADS_PER_STEP if n_heads % GLA_HEADS_PER_STEP == 0 else 1
    kw, vw = hpb * hk, hpb * hv
    assert rb % GLA_CHUNK == 0 and (2 * key_dim) % vw == 0 and value_dim % vw == 0
    row = lambda b, h, i: b * nb + i
    k0 = key_dim // kw
    v0 = 2 * key_dim // vw
    g0 = (2 * key_dim + value_dim) // vw
    return pl.pallas_call(
        functools.partial(_gla_scan_kernel, heads=hpb, chunk=GLA_CHUNK, scale=hk ** -0.5,
                          eps=INNER_NORM_EPS),
        grid=(batch, n_heads // hpb, nb),
        in_specs=[
            pl.BlockSpec((rb, kw), lambda b, h, i: (row(b, h, i), h)),
            pl.BlockSpec((rb, kw), lambda b, h, i: (row(b, h, i), k0 + h)),
            pl.BlockSpec((rb, vw), lambda b, h, i: (row(b, h, i), v0 + h)),
            pl.BlockSpec((rb, kw), lambda b, h, i: (row(b, h, i), h)),
            pl.BlockSpec((rb, vw), lambda b, h, i: (row(b, h, i), g0 + h)),
            pl.BlockSpec((1, hv), lambda b, h, i: (0, 0)),
        ],
        out_specs=pl.BlockSpec((rb, vw), lambda b, h, i: (row(b, h, i), h)),
        out_shape=jax.ShapeDtypeStruct((t, value_dim), BF16),
        scratch_shapes=[pltpu.VMEM((hpb, hk, hv), F32)],
        compiler_params=_params("parallel", "parallel", "arbitrary"),
        name="gla_scan",
    )(qkvg, qkvg, qkvg, log_g, qkvg, norm_w.reshape(1, hv).astype(F32))


def _gla_mixer(x, batch, norm_w, in_proj, gk_up, gk_bias, inner_norm_w, out_proj):
    key_dim = gk_up.shape[1]
    value_dim = out_proj.shape[0]
    n_heads = value_dim // inner_norm_w.shape[0]
    main = 2 * key_dim + 2 * value_dim
    h = _rmsnorm(x, norm_w, NORM_EPS, BF16)
    qkvg = _matmul_wcast(h, in_proj, main, BF16)
    log_g = _gla_gate(h, in_proj[:, main:], gk_up, gk_bias)
    o = _gla_scan(qkvg, log_g, inner_norm_w, batch, key_dim, value_dim, n_heads)
    return _matmul_residual(o, out_proj.astype(BF16), x)


def kernel(x, l0_mixer_norm, l0_ssd_in_proj, l0_ssd_conv_w, l0_ssd_conv_b, l0_ssd_dt_bias, l0_ssd_a_log, l0_ssd_d, l0_ssd_norm_w, l0_ssd_out_proj, l0_ffn_norm, l0_ffn_w_gate, l0_ffn_w_up, l0_ffn_w_down, l1_mixer_norm, l1_gla_in_proj, l1_gla_gk_up, l1_gla_gk_bias, l1_gla_norm_w, l1_gla_out_proj, l1_ffn_norm, l1_ffn_w_gate, l1_ffn_w_up, l1_ffn_w_down, final_norm):
    batch, seqlen, d = x.shape
    xf = x.reshape(batch * seqlen, d).astype(F32)
    xf = _ssd_mixer(xf, batch, l0_mixer_norm, l0_ssd_in_proj, l0_ssd_conv_w, l0_ssd_conv_b,
                    l0_ssd_dt_bias, l0_ssd_a_log, l0_ssd_d, l0_ssd_norm_w, l0_ssd_out_proj)
    xf = _ffn(xf, l0_ffn_norm, l0_ffn_w_gate, l0_ffn_w_up, l0_ffn_w_down)
    xf = _gla_mixer(xf, batch, l1_mixer_norm, l1_gla_in_proj, l1_gla_gk_up, l1_gla_gk_bias,
                    l1_gla_norm_w, l1_gla_out_proj)
    xf = _ffn(xf, l1_ffn_norm, l1_ffn_w_gate, l1_ffn_w_up, l1_ffn_w_down)
    return _rmsnorm(xf, final_norm, NORM_EPS, x.dtype).reshape(batch, seqlen, d)
```

```python
import functools

import jax
import jax.numpy as jnp
from jax import lax
from jax.experimental import pallas as pl
from jax.experimental.pallas import tpu as pltpu

F32 = jnp.float32
BF16 = jnp.bfloat16

NORM_EPS = 1e-6
INNER_NORM_EPS = 1e-5
SSD_D_STATE = 128
SSD_CHUNK = 256
GLA_CHUNK = 64
GLA_GATE_NORMALIZER = 16.0
LOG2_E = 1.4426950408889634

V7X_LANES = 128
V7X_BF16_SUBLANES = 16
V7X_VMEM_BYTES = 64 * 1024 * 1024
VMEM_LIMIT_BYTES = V7X_VMEM_BYTES - 8 * 1024 * 1024

MM_RES_VMEM_BUDGET = 46 * 1024 * 1024
WCAST_TILE_M = 2048
WCAST_TILE_N = 512
GATEUP_TILE_N = 256
ROW_TILE = 256
GLA_ROW_BLOCK = 256
GLA_HEADS_PER_STEP = 2


def _params(*semantics):
    return pltpu.CompilerParams(dimension_semantics=semantics, vmem_limit_bytes=VMEM_LIMIT_BYTES)


def _tile(dim, pref, align=V7X_LANES):
    if dim <= pref:
        return dim
    t = (pref // align) * align
    while t >= align:
        if dim % t == 0:
            return t
        t -= align
    return dim


def _silu(x):
    return x * jax.nn.sigmoid(x)


def _softplus(x):
    return jnp.maximum(x, 0.0) + jnp.log1p(jnp.exp(-jnp.abs(x)))


def _pieces3(a):
    hi = a.astype(BF16).astype(F32)
    mid = (a - hi).astype(BF16).astype(F32)
    return hi, mid, a - hi - mid


def _split3_cols(a):
    return jnp.concatenate(_pieces3(a), axis=1).astype(BF16)


def _split3_rows(a):
    return jnp.concatenate(_pieces3(a), axis=0).astype(BF16)


def _rms_normalize(x, w, eps):
    ms = jnp.mean(x * x, axis=-1, keepdims=True)
    return x * lax.rsqrt(ms + eps) * w


def _rmsnorm_kernel(x_ref, w_ref, o_ref, *, eps):
    o_ref[...] = _rms_normalize(x_ref[...], w_ref[...], eps).astype(o_ref.dtype)


def _rmsnorm(x, w, eps, out_dtype):
    t, d = x.shape
    tr = _tile(t, ROW_TILE, 8)
    return pl.pallas_call(
        functools.partial(_rmsnorm_kernel, eps=eps),
        grid=(t // tr,),
        in_specs=[pl.BlockSpec((tr, d), lambda i: (i, 0)), pl.BlockSpec((1, d), lambda i: (0, 0))],
        out_specs=pl.BlockSpec((tr, d), lambda i: (i, 0)),
        out_shape=jax.ShapeDtypeStruct((t, d), out_dtype),
        compiler_params=_params("parallel"),
        name="rmsnorm",
    )(x, w.reshape(1, d).astype(F32))


def _mm_wcast_kernel(a_ref, w_ref, o_ref):
    w = w_ref[...].astype(BF16)
    o_ref[...] = jnp.dot(a_ref[...], w, preferred_element_type=F32).astype(o_ref.dtype)


def _resident_rows(tm, k):
    return pl.BlockSpec((tm, k), lambda i, j: (i, 0), pipeline_mode=pl.Buffered(1))


def _matmul_wcast(a, w, n_cols, out_dtype):
    m, k = a.shape
    tm, tn = _tile(m, WCAST_TILE_M), _tile(n_cols, WCAST_TILE_N)
    return pl.pallas_call(
        _mm_wcast_kernel,
        grid=(m // tm, n_cols // tn),
        in_specs=[_resident_rows(tm, k), pl.BlockSpec((k, tn), lambda i, j: (0, j))],
        out_specs=pl.BlockSpec((tm, tn), lambda i, j: (i, j)),
        out_shape=jax.ShapeDtypeStruct((m, n_cols), out_dtype),
        compiler_params=_params("parallel", "arbitrary"),
        name="matmul_wcast",
    )(a, w)


def _mm_wcast_conv_kernel(a_ref, w_ref, cw_ref, cb_ref, o_ref, buf_ref, carry_ref, *, halo, tiles_per_seq):
    i, j = pl.program_id(0), pl.program_id(1)
    tm = a_ref.shape[0]
    kw = cw_ref.shape[0]

    @pl.when((i == 0) & (j == 0))
    def _():
        carry_ref[...] = jnp.zeros_like(carry_ref)

    acc = jnp.dot(a_ref[...], w_ref[...].astype(BF16), preferred_element_type=F32)
    buf_ref[0:halo, :] = jnp.where(lax.rem(i, tiles_per_seq) == 0, 0.0, carry_ref[j])
    buf_ref[halo:halo + tm, :] = acc
    carry_ref[j] = acc[tm - halo:, :]
    out = cb_ref[...] + acc * cw_ref[kw - 1:kw, :]
    for tap in range(kw - 1):
        off = halo - (kw - 1) + tap
        out = out + buf_ref[off:off + tm, :] * cw_ref[tap:tap + 1, :]
    o_ref[...] = _silu(out).astype(o_ref.dtype)


def _matmul_wcast_conv(a, w, col0, conv_w, conv_b, seqlen):
    m, k = a.shape
    c, kw = conv_w.shape
    halo = V7X_BF16_SUBLANES
    tm, tn = _tile(seqlen, WCAST_TILE_M, halo), _tile(c, WCAST_TILE_N)
    assert col0 % tn == 0 and seqlen % tm == 0 and kw - 1 <= halo
    return pl.pallas_call(
        functools.partial(_mm_wcast_conv_kernel, halo=halo, tiles_per_seq=seqlen // tm),
        grid=(m // tm, c // tn),
        in_specs=[_resident_rows(tm, k),
                  pl.BlockSpec((k, tn), lambda i, j: (0, col0 // tn + j)),
                  pl.BlockSpec((kw, tn), lambda i, j: (0, j)),
                  pl.BlockSpec((1, tn), lambda i, j: (0, j))],
        out_specs=pl.BlockSpec((tm, tn), lambda i, j: (i, j)),
        out_shape=jax.ShapeDtypeStruct((m, c), BF16),
        scratch_shapes=[pltpu.VMEM((tm + halo, tn), F32), pltpu.VMEM((c // tn, halo, tn), F32)],
        compiler_params=_params("arbitrary", "arbitrary"),
        name="matmul_wcast_conv",
    )(a, w, conv_w.T.astype(F32), conv_b.reshape(1, c).astype(F32))


def _mm_res_kernel(a_ref, w_ref, r_ref, o_ref):
    o_ref[...] = r_ref[...] + jnp.dot(a_ref[...], w_ref[...], preferred_element_type=F32)


def _res_tiles(m, k, n):
    best = None
    for tm in (2048, 1024, 512):
        for tn in (1024, 512, 256):
            if m % tm or n % tn:
                continue
            need = k * tn * 2 + 2 * tm * k * 2 + 5 * tm * tn * 4
            if need <= MM_RES_VMEM_BUDGET and (best is None or tm * tn > best[0] * best[1]):
                best = (tm, tn)
    return best if best is not None else (_tile(m, 256, 8), _tile(n, V7X_LANES))


def _matmul_residual(a, w, res):
    m, k = a.shape
    n = w.shape[1]
    tm, tn = _res_tiles(m, k, n)
    return pl.pallas_call(
        _mm_res_kernel,
        grid=(n // tn, m // tm),
        in_specs=[
            pl.BlockSpec((tm, k), lambda j, i: (i, 0)),
            pl.BlockSpec((k, tn), lambda j, i: (0, j), pipeline_mode=pl.Buffered(1)),
            pl.BlockSpec((tm, tn), lambda j, i: (i, j)),
        ],
        out_specs=pl.BlockSpec((tm, tn), lambda j, i: (i, j)),
        out_shape=jax.ShapeDtypeStruct((m, n), F32),
        compiler_params=_params("parallel", "arbitrary"),
        name="matmul_residual",
    )(a, w, res)


def _gateup_kernel(h_ref, wg_ref, wu_ref, o_ref):
    h = h_ref[...]
    g = jnp.dot(h, wg_ref[...].astype(BF16), preferred_element_type=F32)
    u = jnp.dot(h, wu_ref[...].astype(BF16), preferred_element_type=F32)
    o_ref[...] = (_silu(g) * u).astype(o_ref.dtype)


def _ffn_gateup(h, wg, wu):
    m, k = h.shape
    n = wg.shape[1]
    tm, tn = _tile(m, WCAST_TILE_M), _tile(n, GATEUP_TILE_N)
    wspec = pl.BlockSpec((k, tn), lambda i, j: (0, j))
    return pl.pallas_call(
        _gateup_kernel,
        grid=(m // tm, n // tn),
        in_specs=[_resident_rows(tm, k), wspec, wspec],
        out_specs=pl.BlockSpec((tm, tn), lambda i, j: (i, j)),
        out_shape=jax.ShapeDtypeStruct((m, n), BF16),
        compiler_params=_params("parallel", "arbitrary"),
        name="ffn_gateup",
    )(h, wg, wu)


def _ffn(x, norm_w, w_gate, w_up, w_down):
    h = _rmsnorm(x, norm_w, NORM_EPS, BF16)
    return _matmul_residual(_ffn_gateup(h, w_gate, w_up), w_down.astype(BF16), x)


def _ssd_norm_dt_kernel(x_ref, nw_ref, w_ref, bias_ref, alog_ref, h_ref, dt_ref, cs_ref, cst_ref, *,
                        n_groups, eps):
    h = _rms_normalize(x_ref[...], nw_ref[...], eps).astype(BF16)
    h_ref[...] = h
    lc = h.shape[0]
    nh = bias_ref.shape[1]
    r = nh // n_groups
    raw = jnp.dot(h, w_ref[...].astype(BF16), preferred_element_type=F32)[:, :nh]
    dt = _softplus(raw + bias_ref[...])
    la = dt * (-LOG2_E * jnp.exp(alog_ref[...]))
    row = lax.broadcasted_iota(jnp.int32, (lc, lc), 0)
    col = lax.broadcasted_iota(jnp.int32, (lc, lc), 1)
    tril3 = jnp.concatenate([(row >= col).astype(BF16)] * 3, axis=1)
    cs = jnp.dot(tril3, _split3_rows(la), preferred_element_type=F32)
    cst = cs.T
    for g in range(n_groups):
        dt_ref[0, g] = dt[:, g * r:(g + 1) * r]
        cs_ref[0, g] = cs[:, g * r:(g + 1) * r]
        cst_ref[0, g] = cst[g * r:(g + 1) * r, :]


def _ssd_norm_dt(x, norm_w, in_proj, col0, dt_bias, a_log, n_groups):
    t, d = x.shape
    nh = dt_bias.shape[0]
    r = nh // n_groups
    lc = SSD_CHUNK
    nblk = t // lc
    assert col0 % V7X_LANES == 0 and nh <= V7X_LANES
    rows = pl.BlockSpec((lc, d), lambda i: (i, 0))
    vec = pl.BlockSpec((1, nh), lambda i: (0, 0))
    tok = pl.BlockSpec((1, n_groups, lc, r), lambda i: (i, 0, 0, 0))
    head = pl.BlockSpec((1, n_groups, r, lc), lambda i: (i, 0, 0, 0))
    return pl.pallas_call(
        functools.partial(_ssd_norm_dt_kernel, n_groups=n_groups, eps=NORM_EPS),
        grid=(nblk,),
        in_specs=[rows, pl.BlockSpec((1, d), lambda i: (0, 0)),
                  pl.BlockSpec((d, V7X_LANES), lambda i: (0, col0 // V7X_LANES)), vec, vec],
        out_specs=[rows, tok, tok, head],
        out_shape=[jax.ShapeDtypeStruct((t, d), BF16),
                   jax.ShapeDtypeStruct((nblk, n_groups, lc, r), F32),
                   jax.ShapeDtypeStruct((nblk, n_groups, lc, r), F32),
                   jax.ShapeDtypeStruct((nblk, n_groups, r, lc), F32)],
        compiler_params=_params("parallel"),
        name="ssd_norm_dt",
    )(x, norm_w.reshape(1, d).astype(F32), in_proj,
      dt_bias.reshape(1, nh).astype(F32), a_log.reshape(1, nh).astype(F32))


def _ssd_scan_kernel(x_ref, b_ref, c_ref, z_ref, dt_ref, cs_ref, cst_ref, dskip_ref, nw_ref,
                     o_ref, state_ref, *, heads, head_dim, eps):
    @pl.when(pl.program_id(2) == 0)
    def _():
        state_ref[...] = jnp.zeros_like(state_ref)

    lc = x_ref.shape[0]
    half = lc // 2
    gw = heads * head_dim
    pair = 2 * head_dim
    x = x_ref[...].astype(F32)
    bm = b_ref[...]
    cm = c_ref[...]
    dt = dt_ref[0, 0]
    cs = cs_ref[0, 0]
    cst = cst_ref[0, 0]

    e_row = lax.broadcasted_iota(jnp.int32, (3 * heads, gw), 0) % heads
    e_col = lax.broadcasted_iota(jnp.int32, (3 * heads, gw), 1) // head_dim
    expand_m = (e_row == e_col).astype(BF16)
    expand = lambda a: jnp.dot(_split3_cols(a), expand_m, preferred_element_type=F32)

    cs_end = cs[lc - 1:lc, :]
    dt_e = expand(dt)
    ecs_e = expand(jnp.exp2(cs))
    dtd_e = expand(dt * jnp.exp2(cs_end - cs))

    xdt = (x * dt_e).astype(BF16)
    cb = lax.dot_general(cm, bm, (((1,), (1,)), ((), ())), preferred_element_type=F32)
    cb_tl, cb_bl, cb_br = cb[:half, :half], cb[half:, :half], cb[half:, half:]
    tri = lax.broadcasted_iota(jnp.int32, (half, half), 0) >= lax.broadcasted_iota(jnp.int32, (half, half), 1)
    first_half = lax.broadcasted_iota(jnp.int32, (lc, pair), 1) < head_dim
    zero_tr = jnp.zeros((half, half), F32)

    def masked_cb(r):
        col, row = cs[:, r:r + 1], cst[r:r + 1, :]
        m_tl = cb_tl * jnp.exp2(jnp.where(tri, col[:half] - row[:, :half], -jnp.inf))
        m_bl = cb_bl * jnp.exp2(col[half:] - row[:, :half])
        m_br = cb_br * jnp.exp2(jnp.where(tri, col[half:] - row[:, half:], -jnp.inf))
        top = jnp.concatenate([m_tl, zero_tr], axis=1)
        return jnp.concatenate([top, jnp.concatenate([m_bl, m_br], axis=1)], axis=0).astype(BF16)

    ys = []
    for p in range(heads // 2):
        xp = xdt[:, p * pair:(p + 1) * pair]
        xa = jnp.where(first_half, xp, jnp.zeros_like(xp))
        xb = jnp.where(first_half, jnp.zeros_like(xp), xp)
        ys.append(jnp.dot(masked_cb(2 * p), xa, preferred_element_type=F32)
                  + jnp.dot(masked_cb(2 * p + 1), xb, preferred_element_type=F32))
    y = jnp.concatenate(ys, axis=1) if len(ys) > 1 else ys[0]

    state = state_ref[...]
    y = y + jnp.dot(cm, state.astype(BF16), preferred_element_type=F32) * ecs_e
    state_ref[...] = state * ecs_e[lc - 1:lc, :] + lax.dot_general(
        bm, (x * dtd_e).astype(BF16), (((0,), (0,)), ((), ())), preferred_element_type=F32)

    y = y + x * dskip_ref[...]
    y = y * _silu(z_ref[...].astype(F32))
    ms = jnp.mean(y * y, axis=-1, keepdims=True)
    o_ref[...] = (y * lax.rsqrt(ms + eps) * nw_ref[...]).astype(o_ref.dtype)


def _ssd_scan(z, xbc, dt, cs, cst, d_skip, norm_w, batch, d_inner, n_groups, heads):
    t = z.shape[0]
    lc = SSD_CHUNK
    nc = t // batch // lc
    n = SSD_D_STATE
    gw = d_inner // n_groups
    head_dim = gw // heads
    assert heads % 2 == 0 and 2 * head_dim == V7X_LANES
    row = lambda b, g, c: b * nc + c
    small = lambda shape: pl.BlockSpec(shape, lambda b, g, c: (row(b, g, c), g, 0, 0))
    return pl.pallas_call(
        functools.partial(_ssd_scan_kernel, heads=heads, head_dim=head_dim, eps=INNER_NORM_EPS),
        grid=(batch, n_groups, nc),
        in_specs=[
            pl.BlockSpec((lc, gw), lambda b, g, c: (row(b, g, c), g)),
            pl.BlockSpec((lc, n), lambda b, g, c: (row(b, g, c), d_inner // n + g)),
            pl.BlockSpec((lc, n), lambda b, g, c: (row(b, g, c), d_inner // n + n_groups + g)),
            pl.BlockSpec((lc, gw), lambda b, g, c: (row(b, g, c), g)),
            small((1, 1, lc, heads)), small((1, 1, lc, heads)), small((1, 1, heads, lc)),
            pl.BlockSpec((1, gw), lambda b, g, c: (0, g)),
            pl.BlockSpec((1, gw), lambda b, g, c: (0, g)),
        ],
        out_specs=pl.BlockSpec((lc, gw), lambda b, g, c: (row(b, g, c), g)),
        out_shape=jax.ShapeDtypeStruct((t, d_inner), BF16),
        scratch_shapes=[pltpu.VMEM((n, gw), F32)],
        compiler_params=_params("parallel", "parallel", "arbitrary"),
        name="ssd_scan",
    )(xbc, xbc, xbc, z, dt, cs, cst,
      jnp.repeat(d_skip.astype(F32), head_dim).reshape(1, d_inner),
      norm_w.reshape(1, d_inner).astype(F32))


def _ssd_mixer(x, batch, norm_w, in_proj, conv_w, conv_b, dt_bias, a_log, d_skip, inner_norm_w, out_proj):
    d_inner = out_proj.shape[0]
    conv_dim = conv_w.shape[0]
    n_heads = dt_bias.shape[0]
    n_groups = (conv_dim - d_inner) // 2 // SSD_D_STATE
    main = d_inner + conv_dim
    h, dt, cs, cst = _ssd_norm_dt(x, norm_w, in_proj, main, dt_bias, a_log, n_groups)
    z = _matmul_wcast(h, in_proj, d_inner, BF16)
    xbc = _matmul_wcast_conv(h, in_proj, d_inner, conv_w, conv_b, x.shape[0] // batch)
    y = _ssd_scan(z, xbc, dt, cs, cst, d_skip, inner_norm_w, batch, d_inner, n_groups,
                  n_heads // n_groups)
    return _matmul_residual(y, out_proj.astype(BF16), x)


def _gla_norm_gate_kernel(x_ref, nw_ref, wlow_ref, up_ref, bias_ref, h_ref, o_ref, *, eps):
    h = _rms_normalize(x_ref[...], nw_ref[...], eps).astype(BF16)
    h_ref[...] = h
    rank = up_ref.shape[0]
    low = jnp.dot(h, wlow_ref[...].astype(BF16), preferred_element_type=F32)[:, :rank]
    l_hi, l_mid, l_lo = _pieces3(low)
    u_hi, u_mid, u_lo = _pieces3(up_ref[...])
    lhs = jnp.concatenate([l_hi, l_mid, l_lo, l_hi, l_mid, l_hi], axis=1).astype(BF16)
    rhs = jnp.concatenate([u_hi, u_hi, u_hi, u_mid, u_mid, u_lo], axis=0).astype(BF16)
    x = jnp.dot(lhs, rhs, preferred_element_type=F32) + bias_ref[...]
    o_ref[...] = (jnp.minimum(x, 0.0) - jnp.log1p(jnp.exp(-jnp.abs(x)))) * (1.0 / GLA_GATE_NORMALIZER)


def _gla_norm_gate(x, norm_w, w_low, gk_up, gk_bias):
    t, d = x.shape
    rank, kd = gk_up.shape
    pad = (-rank) % V7X_LANES
    w_low = jnp.pad(w_low.astype(F32), ((0, 0), (0, pad)))
    gk_up = gk_up.astype(F32)
    tr = _tile(t, ROW_TILE, 8)
    rows = pl.BlockSpec((tr, d), lambda i: (i, 0))
    return pl.pallas_call(
        functools.partial(_gla_norm_gate_kernel, eps=NORM_EPS),
        grid=(t // tr,),
        in_specs=[rows, pl.BlockSpec((1, d), lambda i: (0, 0)),
                  pl.BlockSpec((d, rank + pad), lambda i: (0, 0)),
                  pl.BlockSpec((rank, kd), lambda i: (0, 0)),
                  pl.BlockSpec((1, kd), lambda i: (0, 0))],
        out_specs=[rows, pl.BlockSpec((tr, kd), lambda i: (i, 0))],
        out_shape=[jax.ShapeDtypeStruct((t, d), BF16), jax.ShapeDtypeStruct((t, kd), F32)],
        compiler_params=_params("parallel"),
        name="gla_norm_gate",
    )(x, norm_w.reshape(1, d).astype(F32), w_low, gk_up, gk_bias.reshape(1, kd).astype(F32))


def _gla_scan_kernel(q_ref, k_ref, v_ref, g_ref, go_ref, nw_ref, o_ref, state_ref, *,
                     heads, chunk, scale, eps):
    @pl.when(pl.program_id(2) == 0)
    def _():
        state_ref[...] = jnp.zeros_like(state_ref)

    rows = q_ref.shape[0]
    nsub = rows // chunk
    hk = q_ref.shape[1] // heads
    hv = v_ref.shape[1] // heads
    shift = chunk.bit_length() - 1
    ri = lax.broadcasted_iota(jnp.int32, (rows, rows), 0)
    ci = lax.broadcasted_iota(jnp.int32, (rows, rows), 1)
    causal = ri >= ci
    in_chunk_causal = causal & (lax.shift_right_logical(ri, shift) == lax.shift_right_logical(ci, shift))
    cumsum_m = jnp.concatenate([in_chunk_causal.astype(BF16)] * 3, axis=1)
    blk = lambda a, s: a[s * chunk:(s + 1) * chunk]
    for hh in range(heads):
        kc = slice(hh * hk, (hh + 1) * hk)
        vc = slice(hh * hv, (hh + 1) * hv)
        q = q_ref[:, kc].astype(F32) * scale
        k = k_ref[:, kc].astype(F32)
        v = v_ref[:, vc]
        cs = jnp.dot(cumsum_m, _split3_rows(g_ref[:, kc]), preferred_element_type=F32)
        tot = [cs[(s + 1) * chunk - 1:(s + 1) * chunk, :] for s in range(nsub)]
        before = [jnp.zeros_like(tot[0])]
        for s in range(nsub):
            before.append(before[s] + tot[s])
        total = before[nsub]
        q_dec = q * jnp.exp(cs)
        q_dec_bf = q_dec.astype(BF16)
        k_inv = (k * jnp.exp(-cs)).astype(BF16)
        k_end = [blk(k, s) * jnp.exp(tot[s] - blk(cs, s)) for s in range(nsub)]
        q_in = jnp.concatenate([blk(q_dec, s) * jnp.exp(before[s]) for s in range(nsub)], axis=0)
        k_out = jnp.concatenate([k_end[s] * jnp.exp(total - before[s + 1]) for s in range(nsub)], axis=0)
        score_rows = []
        for s in range(nsub):
            keys = [(k_end[j] * jnp.exp(before[s] - before[j + 1])).astype(BF16) for j in range(s)]
            keys.append(blk(k_inv, s))
            keys += [jnp.zeros((chunk, hk), BF16)] * (nsub - 1 - s)
            score_rows.append(lax.dot_general(blk(q_dec_bf, s), jnp.concatenate(keys, axis=0),
                                              (((1,), (1,)), ((), ())), preferred_element_type=F32))
        scores = jnp.where(causal, jnp.concatenate(score_rows, axis=0), 0.0).astype(BF16)
        state = state_ref[hh]
        o = (jnp.dot(scores, v, preferred_element_type=F32)
             + jnp.dot(q_in.astype(BF16), state.astype(BF16), preferred_element_type=F32))
        total_col = jnp.broadcast_to(total, (V7X_LANES, hk)).T[:, 0:1]
        state_ref[hh] = state * jnp.exp(total_col) + lax.dot_general(
            k_out.astype(BF16), v, (((0,), (0,)), ((), ())), preferred_element_type=F32)
        ms = jnp.mean(o * o, axis=-1, keepdims=True)
        o = o * lax.rsqrt(ms + eps) * nw_ref[...]
        o_ref[:, vc] = (o * _silu(go_ref[:, vc].astype(F32))).astype(o_ref.dtype)


def _gla_scan(qkvg, log_g, norm_w, batch, key_dim, value_dim, n_heads):
    t = qkvg.shape[0]
    hk, hv = key_dim // n_heads, value_dim // n_heads
    seqlen = t // batch
    rb = _tile(seqlen, GLA_ROW_BLOCK, GLA_CHUNK)
    nb = seqlen // rb
    hpb = GLA_HEADS_PER_STEP if n_heads % GLA_HEADS_PER_STEP == 0 else 1
    kw, vw = hpb * hk, hpb * hv
    assert rb % GLA_CHUNK == 0 and (2 * key_dim) % vw == 0 and value_dim % vw == 0
    row = lambda b, h, i: b * nb + i
    k0 = key_dim // kw
    v0 = 2 * key_dim // vw
    g0 = (2 * key_dim + value_dim) // vw
    return pl.pallas_call(
        functools.partial(_gla_scan_kernel, heads=hpb, chunk=GLA_CHUNK, scale=hk ** -0.5,
                          eps=INNER_NORM_EPS),
        grid=(batch, n_heads // hpb, nb),
        in_specs=[
            pl.BlockSpec((rb, kw), lambda b, h, i: (row(b, h, i), h)),
            pl.BlockSpec((rb, kw), lambda b, h, i: (row(b, h, i), k0 + h)),
            pl.BlockSpec((rb, vw), lambda b, h, i: (row(b, h, i), v0 + h)),
            pl.BlockSpec((rb, kw), lambda b, h, i: (row(b, h, i), h)),
            pl.BlockSpec((rb, vw), lambda b, h, i: (row(b, h, i), g0 + h)),
            pl.BlockSpec((1, hv), lambda b, h, i: (0, 0)),
        ],
        out_specs=pl.BlockSpec((rb, vw), lambda b, h, i: (row(b, h, i), h)),
        out_shape=jax.ShapeDtypeStruct((t, value_dim), BF16),
        scratch_shapes=[pltpu.VMEM((hpb, hk, hv), F32)],
        compiler_params=_params("parallel", "parallel", "arbitrary"),
        name="gla_scan",
    )(qkvg, qkvg, qkvg, log_g, qkvg, norm_w.reshape(1, hv).astype(F32))


def _gla_mixer(x, batch, norm_w, in_proj, gk_up, gk_bias, inner_norm_w, out_proj):
    key_dim = gk_up.shape[1]
    value_dim = out_proj.shape[0]
    n_heads = value_dim // inner_norm_w.shape[0]
    main = 2 * key_dim + 2 * value_dim
    h, log_g = _gla_norm_gate(x, norm_w, in_proj[:, main:], gk_up, gk_bias)
    qkvg = _matmul_wcast(h, in_proj, main, BF16)
    o = _gla_scan(qkvg, log_g, inner_norm_w, batch, key_dim, value_dim, n_heads)
    return _matmul_residual(o, out_proj.astype(BF16), x)


def kernel(x, l0_mixer_norm, l0_ssd_in_proj, l0_ssd_conv_w, l0_ssd_conv_b, l0_ssd_dt_bias, l0_ssd_a_log, l0_ssd_d, l0_ssd_norm_w, l0_ssd_out_proj, l0_ffn_norm, l0_ffn_w_gate, l0_ffn_w_up, l0_ffn_w_down, l1_mixer_norm, l1_gla_in_proj, l1_gla_gk_up, l1_gla_gk_bias, l1_gla_norm_w, l1_gla_out_proj, l1_ffn_norm, l1_ffn_w_gate, l1_ffn_w_up, l1_ffn_w_down, final_norm):
    batch, seqlen, d = x.shape
    xf = x.reshape(batch * seqlen, d).astype(F32)
    xf = _ssd_mixer(xf, batch, l0_mixer_norm, l0_ssd_in_proj, l0_ssd_conv_w, l0_ssd_conv_b,
                    l0_ssd_dt_bias, l0_ssd_a_log, l0_ssd_d, l0_ssd_norm_w, l0_ssd_out_proj)
    xf = _ffn(xf, l0_ffn_norm, l0_ffn_w_gate, l0_ffn_w_up, l0_ffn_w_down)
    xf = _gla_mixer(xf, batch, l1_mixer_norm, l1_gla_in_proj, l1_gla_gk_up, l1_gla_gk_bias,
                    l1_gla_norm_w, l1_gla_out_proj)
    xf = _ffn(xf, l1_ffn_norm, l1_ffn_w_gate, l1_ffn_w_up, l1_ffn_w_down)
    return _rmsnorm(xf, final_norm, NORM_EPS, x.dtype).reshape(batch, seqlen, d)
```

```python
import functools

import jax
import jax.numpy as jnp
from jax import lax
from jax.experimental import pallas as pl
from jax.experimental.pallas import tpu as pltpu

F32 = jnp.float32
BF16 = jnp.bfloat16

NORM_EPS = 1e-6
INNER_NORM_EPS = 1e-5
SSD_D_STATE = 128
SSD_CHUNK = 256
GLA_CHUNK = 64
GLA_GATE_NORMALIZER = 16.0
LOG2_E = 1.4426950408889634

V7X_LANES = 128
V7X_BF16_SUBLANES = 16
V7X_VMEM_BYTES = 64 * 1024 * 1024
VMEM_LIMIT_BYTES = V7X_VMEM_BYTES - 8 * 1024 * 1024

MM_RES_VMEM_BUDGET = 46 * 1024 * 1024
WCAST_TILE_M = 2048
WCAST_TILE_N = 512
GATEUP_TILE_N = 256
ROW_TILE = 256
GLA_ROW_BLOCK = 256
GLA_HEADS_PER_STEP = 4


def _params(*semantics):
    return pltpu.CompilerParams(dimension_semantics=semantics, vmem_limit_bytes=VMEM_LIMIT_BYTES)


def _tile(dim, pref, align=V7X_LANES):
    if dim <= pref:
        return dim
    t = (pref // align) * align
    while t >= align:
        if dim % t == 0:
            return t
        t -= align
    return dim


def _silu(x):
    return x * jax.nn.sigmoid(x)


def _softplus(x):
    return jnp.maximum(x, 0.0) + jnp.log1p(jnp.exp(-jnp.abs(x)))


def _pieces3(a):
    hi = a.astype(BF16).astype(F32)
    mid = (a - hi).astype(BF16).astype(F32)
    return hi, mid, a - hi - mid


def _split3_cols(a):
    return jnp.concatenate(_pieces3(a), axis=1).astype(BF16)


def _split3_rows(a):
    return jnp.concatenate(_pieces3(a), axis=0).astype(BF16)


def _rms_normalize(x, w, eps):
    ms = jnp.mean(x * x, axis=-1, keepdims=True)
    return x * lax.rsqrt(ms + eps) * w


def _rmsnorm_kernel(x_ref, w_ref, o_ref, *, eps):
    o_ref[...] = _rms_normalize(x_ref[...], w_ref[...], eps).astype(o_ref.dtype)


def _rmsnorm(x, w, eps, out_dtype):
    t, d = x.shape
    tr = _tile(t, ROW_TILE, 8)
    return pl.pallas_call(
        functools.partial(_rmsnorm_kernel, eps=eps),
        grid=(t // tr,),
        in_specs=[pl.BlockSpec((tr, d), lambda i: (i, 0)), pl.BlockSpec((1, d), lambda i: (0, 0))],
        out_specs=pl.BlockSpec((tr, d), lambda i: (i, 0)),
        out_shape=jax.ShapeDtypeStruct((t, d), out_dtype),
        compiler_params=_params("parallel"),
        name="rmsnorm",
    )(x, w.reshape(1, d).astype(F32))


def _mm_wcast_kernel(a_ref, w_ref, o_ref):
    w = w_ref[...].astype(BF16)
    o_ref[...] = jnp.dot(a_ref[...], w, preferred_element_type=F32).astype(o_ref.dtype)


def _resident_rows(tm, k):
    return pl.BlockSpec((tm, k), lambda i, j: (i, 0), pipeline_mode=pl.Buffered(1))


def _matmul_wcast(a, w, n_cols, out_dtype):
    m, k = a.shape
    tm, tn = _tile(m, WCAST_TILE_M), _tile(n_cols, WCAST_TILE_N)
    return pl.pallas_call(
        _mm_wcast_kernel,
        grid=(m // tm, n_cols // tn),
        in_specs=[_resident_rows(tm, k), pl.BlockSpec((k, tn), lambda i, j: (0, j))],
        out_specs=pl.BlockSpec((tm, tn), lambda i, j: (i, j)),
        out_shape=jax.ShapeDtypeStruct((m, n_cols), out_dtype),
        compiler_params=_params("parallel", "arbitrary"),
        name="matmul_wcast",
    )(a, w)


def _mm_wcast_conv_kernel(a_ref, w_ref, cw_ref, cb_ref, o_ref, buf_ref, carry_ref, *, halo, tiles_per_seq):
    i, j = pl.program_id(0), pl.program_id(1)
    tm = a_ref.shape[0]
    kw = cw_ref.shape[0]

    @pl.when((i == 0) & (j == 0))
    def _():
        carry_ref[...] = jnp.zeros_like(carry_ref)

    acc = jnp.dot(a_ref[...], w_ref[...].astype(BF16), preferred_element_type=F32)
    buf_ref[0:halo, :] = jnp.where(lax.rem(i, tiles_per_seq) == 0, 0.0, carry_ref[j])
    buf_ref[halo:halo + tm, :] = acc
    carry_ref[j] = acc[tm - halo:, :]
    out = cb_ref[...] + acc * cw_ref[kw - 1:kw, :]
    for tap in range(kw - 1):
        off = halo - (kw - 1) + tap
        out = out + buf_ref[off:off + tm, :] * cw_ref[tap:tap + 1, :]
    o_ref[...] = _silu(out).astype(o_ref.dtype)


def _matmul_wcast_conv(a, w, col0, conv_w, conv_b, seqlen):
    m, k = a.shape
    c, kw = conv_w.shape
    halo = V7X_BF16_SUBLANES
    tm, tn = _tile(seqlen, WCAST_TILE_M, halo), _tile(c, WCAST_TILE_N)
    assert col0 % tn == 0 and seqlen % tm == 0 and kw - 1 <= halo
    return pl.pallas_call(
        functools.partial(_mm_wcast_conv_kernel, halo=halo, tiles_per_seq=seqlen // tm),
        grid=(m // tm, c // tn),
        in_specs=[_resident_rows(tm, k),
                  pl.BlockSpec((k, tn), lambda i, j: (0, col0 // tn + j)),
                  pl.BlockSpec((kw, tn), lambda i, j: (0, j)),
                  pl.BlockSpec((1, tn), lambda i, j: (0, j))],
        out_specs=pl.BlockSpec((tm, tn), lambda i, j: (i, j)),
        out_shape=jax.ShapeDtypeStruct((m, c), BF16),
        scratch_shapes=[pltpu.VMEM((tm + halo, tn), F32), pltpu.VMEM((c // tn, halo, tn), F32)],
        compiler_params=_params("arbitrary", "arbitrary"),
        name="matmul_wcast_conv",
    )(a, w, conv_w.T.astype(F32), conv_b.reshape(1, c).astype(F32))


def _mm_res_kernel(a_ref, w_ref, r_ref, o_ref):
    o_ref[...] = r_ref[...] + jnp.dot(a_ref[...], w_ref[...], preferred_element_type=F32)


def _res_tiles(m, k, n):
    best = None
    for tm in (2048, 1024, 512):
        for tn in (1024, 512, 256):
            if m % tm or n % tn:
                continue
            need = k * tn * 2 + 2 * tm * k * 2 + 5 * tm * tn * 4
            if need <= MM_RES_VMEM_BUDGET and (best is None or tm * tn > best[0] * best[1]):
                best = (tm, tn)
    return best if best is not None else (_tile(m, 256, 8), _tile(n, V7X_LANES))


def _matmul_residual(a, w, res):
    m, k = a.shape
    n = w.shape[1]
    tm, tn = _res_tiles(m, k, n)
    return pl.pallas_call(
        _mm_res_kernel,
        grid=(n // tn, m // tm),
        in_specs=[
            pl.BlockSpec((tm, k), lambda j, i: (i, 0)),
            pl.BlockSpec((k, tn), lambda j, i: (0, j), pipeline_mode=pl.Buffered(1)),
            pl.BlockSpec((tm, tn), lambda j, i: (i, j)),
        ],
        out_specs=pl.BlockSpec((tm, tn), lambda j, i: (i, j)),
        out_shape=jax.ShapeDtypeStruct((m, n), F32),
        compiler_params=_params("parallel", "arbitrary"),
        name="matmul_residual",
    )(a, w, res)


def _gateup_kernel(h_ref, wg_ref, wu_ref, o_ref):
    h = h_ref[...]
    g = jnp.dot(h, wg_ref[...].astype(BF16), preferred_element_type=F32)
    u = jnp.dot(h, wu_ref[...].astype(BF16), preferred_element_type=F32)
    o_ref[...] = (_silu(g) * u).astype(o_ref.dtype)


def _ffn_gateup(h, wg, wu):
    m, k = h.shape
    n = wg.shape[1]
    tm, tn = _tile(m, WCAST_TILE_M), _tile(n, GATEUP_TILE_N)
    wspec = pl.BlockSpec((k, tn), lambda i, j: (0, j))
    return pl.pallas_call(
        _gateup_kernel,
        grid=(m // tm, n // tn),
        in_specs=[_resident_rows(tm, k), wspec, wspec],
        out_specs=pl.BlockSpec((tm, tn), lambda i, j: (i, j)),
        out_shape=jax.ShapeDtypeStruct((m, n), BF16),
        compiler_params=_params("parallel", "arbitrary"),
        name="ffn_gateup",
    )(h, wg, wu)


def _ffn(x, norm_w, w_gate, w_up, w_down):
    h = _rmsnorm(x, norm_w, NORM_EPS, BF16)
    return _matmul_residual(_ffn_gateup(h, w_gate, w_up), w_down.astype(BF16), x)


def _ssd_norm_dt_kernel(x_ref, nw_ref, w_ref, bias_ref, alog_ref, h_ref, dt_ref, cs_ref, cst_ref, *,
                        n_groups, eps):
    h = _rms_normalize(x_ref[...], nw_ref[...], eps).astype(BF16)
    h_ref[...] = h
    lc = h.shape[0]
    nh = bias_ref.shape[1]
    r = nh // n_groups
    raw = jnp.dot(h, w_ref[...].astype(BF16), preferred_element_type=F32)[:, :nh]
    dt = _softplus(raw + bias_ref[...])
    la = dt * (-LOG2_E * jnp.exp(alog_ref[...]))
    row = lax.broadcasted_iota(jnp.int32, (lc, lc), 0)
    col = lax.broadcasted_iota(jnp.int32, (lc, lc), 1)
    tril3 = jnp.concatenate([(row >= col).astype(BF16)] * 3, axis=1)
    cs = jnp.dot(tril3, _split3_rows(la), preferred_element_type=F32)
    cst = cs.T
    for g in range(n_groups):
        dt_ref[0, g] = dt[:, g * r:(g + 1) * r]
        cs_ref[0, g] = cs[:, g * r:(g + 1) * r]
        cst_ref[0, g] = cst[g * r:(g + 1) * r, :]


def _ssd_norm_dt(x, norm_w, in_proj, col0, dt_bias, a_log, n_groups):
    t, d = x.shape
    nh = dt_bias.shape[0]
    r = nh // n_groups
    lc = SSD_CHUNK
    nblk = t // lc
    assert col0 % V7X_LANES == 0 and nh <= V7X_LANES
    rows = pl.BlockSpec((lc, d), lambda i: (i, 0))
    vec = pl.BlockSpec((1, nh), lambda i: (0, 0))
    tok = pl.BlockSpec((1, n_groups, lc, r), lambda i: (i, 0, 0, 0))
    head = pl.BlockSpec((1, n_groups, r, lc), lambda i: (i, 0, 0, 0))
    return pl.pallas_call(
        functools.partial(_ssd_norm_dt_kernel, n_groups=n_groups, eps=NORM_EPS),
        grid=(nblk,),
        in_specs=[rows, pl.BlockSpec((1, d), lambda i: (0, 0)),
                  pl.BlockSpec((d, V7X_LANES), lambda i: (0, col0 // V7X_LANES)), vec, vec],
        out_specs=[rows, tok, tok, head],
        out_shape=[jax.ShapeDtypeStruct((t, d), BF16),
                   jax.ShapeDtypeStruct((nblk, n_groups, lc, r), F32),
                   jax.ShapeDtypeStruct((nblk, n_groups, lc, r), F32),
                   jax.ShapeDtypeStruct((nblk, n_groups, r, lc), F32)],
        compiler_params=_params("parallel"),
        name="ssd_norm_dt",
    )(x, norm_w.reshape(1, d).astype(F32), in_proj,
      dt_bias.reshape(1, nh).astype(F32), a_log.reshape(1, nh).astype(F32))


def _ssd_scan_kernel(x_ref, b_ref, c_ref, z_ref, dt_ref, cs_ref, cst_ref, dskip_ref, nw_ref,
                     o_ref, state_ref, *, heads, head_dim, eps):
    @pl.when(pl.program_id(2) == 0)
    def _():
        state_ref[...] = jnp.zeros_like(state_ref)

    lc = x_ref.shape[0]
    half = lc // 2
    gw = heads * head_dim
    pair = 2 * head_dim
    x = x_ref[...].astype(F32)
    bm = b_ref[...]
    cm = c_ref[...]
    dt = dt_ref[0, 0]
    cs = cs_ref[0, 0]
    cst = cst_ref[0, 0]

    e_row = lax.broadcasted_iota(jnp.int32, (3 * heads, gw), 0) % heads
    e_col = lax.broadcasted_iota(jnp.int32, (3 * heads, gw), 1) // head_dim
    expand_m = (e_row == e_col).astype(BF16)
    expand = lambda a: jnp.dot(_split3_cols(a), expand_m, preferred_element_type=F32)

    cs_end = cs[lc - 1:lc, :]
    dt_e = expand(dt)
    ecs_e = expand(jnp.exp2(cs))
    dtd_e = expand(dt * jnp.exp2(cs_end - cs))

    xdt = (x * dt_e).astype(BF16)
    cb = lax.dot_general(cm, bm, (((1,), (1,)), ((), ())), preferred_element_type=F32)
    cb_tl, cb_bl, cb_br = cb[:half, :half], cb[half:, :half], cb[half:, half:]
    tri = lax.broadcasted_iota(jnp.int32, (half, half), 0) >= lax.broadcasted_iota(jnp.int32, (half, half), 1)
    first_half = lax.broadcasted_iota(jnp.int32, (lc, pair), 1) < head_dim
    zero_tr = jnp.zeros((half, half), F32)

    def masked_cb(r):
        col, row = cs[:, r:r + 1], cst[r:r + 1, :]
        m_tl = cb_tl * jnp.exp2(jnp.where(tri, col[:half] - row[:, :half], -jnp.inf))
        m_bl = cb_bl * jnp.exp2(col[half:] - row[:, :half])
        m_br = cb_br * jnp.exp2(jnp.where(tri, col[half:] - row[:, half:], -jnp.inf))
        top = jnp.concatenate([m_tl, zero_tr], axis=1)
        return jnp.concatenate([top, jnp.concatenate([m_bl, m_br], axis=1)], axis=0).astype(BF16)

    ys = []
    for p in range(heads // 2):
        xp = xdt[:, p * pair:(p + 1) * pair]
        xa = jnp.where(first_half, xp, jnp.zeros_like(xp))
        xb = jnp.where(first_half, jnp.zeros_like(xp), xp)
        ys.append(jnp.dot(masked_cb(2 * p), xa, preferred_element_type=F32)
                  + jnp.dot(masked_cb(2 * p + 1), xb, preferred_element_type=F32))
    y = jnp.concatenate(ys, axis=1) if len(ys) > 1 else ys[0]

    state = state_ref[...]
    y = y + jnp.dot(cm, state.astype(BF16), preferred_element_type=F32) * ecs_e
    state_ref[...] = state * ecs_e[lc - 1:lc, :] + lax.dot_general(
        bm, (x * dtd_e).astype(BF16), (((0,), (0,)), ((), ())), preferred_element_type=F32)

    y = y + x * dskip_ref[...]
    y = y * _silu(z_ref[...].astype(F32))
    ms = jnp.mean(y * y, axis=-1, keepdims=True)
    o_ref[...] = (y * lax.rsqrt(ms + eps) * nw_ref[...]).astype(o_ref.dtype)


def _ssd_scan(z, xbc, dt, cs, cst, d_skip, norm_w, batch, d_inner, n_groups, heads):
    t = z.shape[0]
    lc = SSD_CHUNK
    nc = t // batch // lc
    n = SSD_D_STATE
    gw = d_inner // n_groups
    head_dim = gw // heads
    assert heads % 2 == 0 and 2 * head_dim == V7X_LANES
    row = lambda b, g, c: b * nc + c
    small = lambda shape: pl.BlockSpec(shape, lambda b, g, c: (row(b, g, c), g, 0, 0))
    return pl.pallas_call(
        functools.partial(_ssd_scan_kernel, heads=heads, head_dim=head_dim, eps=INNER_NORM_EPS),
        grid=(batch, n_groups, nc),
        in_specs=[
            pl.BlockSpec((lc, gw), lambda b, g, c: (row(b, g, c), g)),
            pl.BlockSpec((lc, n), lambda b, g, c: (row(b, g, c), d_inner // n + g)),
            pl.BlockSpec((lc, n), lambda b, g, c: (row(b, g, c), d_inner // n + n_groups + g)),
            pl.BlockSpec((lc, gw), lambda b, g, c: (row(b, g, c), g)),
            small((1, 1, lc, heads)), small((1, 1, lc, heads)), small((1, 1, heads, lc)),
            pl.BlockSpec((1, gw), lambda b, g, c: (0, g)),
            pl.BlockSpec((1, gw), lambda b, g, c: (0, g)),
        ],
        out_specs=pl.BlockSpec((lc, gw), lambda b, g, c: (row(b, g, c), g)),
        out_shape=jax.ShapeDtypeStruct((t, d_inner), BF16),
        scratch_shapes=[pltpu.VMEM((n, gw), F32)],
        compiler_params=_params("parallel", "parallel", "arbitrary"),
        name="ssd_scan",
    )(xbc, xbc, xbc, z, dt, cs, cst,
      jnp.repeat(d_skip.astype(F32), head_dim).reshape(1, d_inner),
      norm_w.reshape(1, d_inner).astype(F32))


def _ssd_mixer(x, batch, norm_w, in_proj, conv_w, conv_b, dt_bias, a_log, d_skip, inner_norm_w, out_proj):
    d_inner = out_proj.shape[0]
    conv_dim = conv_w.shape[0]
    n_heads = dt_bias.shape[0]
    n_groups = (conv_dim - d_inner) // 2 // SSD_D_STATE
    main = d_inner + conv_dim
    h, dt, cs, cst = _ssd_norm_dt(x, norm_w, in_proj, main, dt_bias, a_log, n_groups)
    z = _matmul_wcast(h, in_proj, d_inner, BF16)
    xbc = _matmul_wcast_conv(h, in_proj, d_inner, conv_w, conv_b, x.shape[0] // batch)
    y = _ssd_scan(z, xbc, dt, cs, cst, d_skip, inner_norm_w, batch, d_inner, n_groups,
                  n_heads // n_groups)
    return _matmul_residual(y, out_proj.astype(BF16), x)


def _gla_norm_gate_kernel(x_ref, nw_ref, wlow_ref, up_ref, bias_ref, h_ref, o_ref, *, eps):
    h = _rms_normalize(x_ref[...], nw_ref[...], eps).astype(BF16)
    h_ref[...] = h
    rank = up_ref.shape[0]
    low = jnp.dot(h, wlow_ref[...].astype(BF16), preferred_element_type=F32)[:, :rank]
    l_hi, l_mid, l_lo = _pieces3(low)
    u_hi, u_mid, u_lo = _pieces3(up_ref[...])
    lhs = jnp.concatenate([l_hi, l_mid, l_lo, l_hi, l_mid, l_hi], axis=1).astype(BF16)
    rhs = jnp.concatenate([u_hi, u_hi, u_hi, u_mid, u_mid, u_lo], axis=0).astype(BF16)
    x = jnp.dot(lhs, rhs, preferred_element_type=F32) + bias_ref[...]
    o_ref[...] = (jnp.minimum(x, 0.0) - jnp.log1p(jnp.exp(-jnp.abs(x)))) * (1.0 / GLA_GATE_NORMALIZER)


def _gla_norm_gate(x, norm_w, w_low, gk_up, gk_bias):
    t, d = x.shape
    rank, kd = gk_up.shape
    pad = (-rank) % V7X_LANES
    w_low = jnp.pad(w_low.astype(F32), ((0, 0), (0, pad)))
    gk_up = gk_up.astype(F32)
    tr = _tile(t, ROW_TILE, 8)
    rows = pl.BlockSpec((tr, d), lambda i: (i, 0))
    return pl.pallas_call(
        functools.partial(_gla_norm_gate_kernel, eps=NORM_EPS),
        grid=(t // tr,),
        in_specs=[rows, pl.BlockSpec((1, d), lambda i: (0, 0)),
                  pl.BlockSpec((d, rank + pad), lambda i: (0, 0)),
                  pl.BlockSpec((rank, kd), lambda i: (0, 0)),
                  pl.BlockSpec((1, kd), lambda i: (0, 0))],
        out_specs=[rows, pl.BlockSpec((tr, kd), lambda i: (i, 0))],
        out_shape=[jax.ShapeDtypeStruct((t, d), BF16), jax.ShapeDtypeStruct((t, kd), F32)],
        compiler_params=_params("parallel"),
        name="gla_norm_gate",
    )(x, norm_w.reshape(1, d).astype(F32), w_low, gk_up, gk_bias.reshape(1, kd).astype(F32))


def _gla_scan_kernel(q_ref, k_ref, v_ref, g_ref, go_ref, nw_ref, o_ref, state_ref, *,
                     heads, chunk, scale, eps):
    @pl.when(pl.program_id(2) == 0)
    def _():
        state_ref[...] = jnp.zeros_like(state_ref)

    rows = q_ref.shape[0]
    nsub = rows // chunk
    hk = q_ref.shape[1] // heads
    hv = v_ref.shape[1] // heads
    shift = chunk.bit_length() - 1
    ri = lax.broadcasted_iota(jnp.int32, (rows, rows), 0)
    ci = lax.broadcasted_iota(jnp.int32, (rows, rows), 1)
    causal = ri >= ci
    in_chunk_causal = causal & (lax.shift_right_logical(ri, shift) == lax.shift_right_logical(ci, shift))
    cumsum_m = jnp.concatenate([in_chunk_causal.astype(BF16)] * 3, axis=1)
    blk = lambda a, s: a[s * chunk:(s + 1) * chunk]
    q = q_ref[...].astype(F32) * scale
    k = k_ref[...].astype(F32)
    cs = jnp.dot(cumsum_m, _split3_rows(g_ref[...]), preferred_element_type=F32)
    tot = [cs[(s + 1) * chunk - 1:(s + 1) * chunk, :] for s in range(nsub)]
    before = [jnp.zeros_like(tot[0])]
    for s in range(nsub):
        before.append(before[s] + tot[s])
    total = before[nsub]
    q_dec = q * jnp.exp(cs)
    q_dec_bf = q_dec.astype(BF16)
    k_inv = (k * jnp.exp(-cs)).astype(BF16)
    k_end = [blk(k, s) * jnp.exp(tot[s] - blk(cs, s)) for s in range(nsub)]
    q_in = jnp.concatenate([blk(q_dec, s) * jnp.exp(before[s]) for s in range(nsub)], axis=0).astype(BF16)
    k_out = jnp.concatenate([k_end[s] * jnp.exp(total - before[s + 1]) for s in range(nsub)],
                            axis=0).astype(BF16)
    keys = []
    for s in range(nsub):
        rows_s = [(k_end[j] * jnp.exp(before[s] - before[j + 1])).astype(BF16) for j in range(s)]
        rows_s.append(blk(k_inv, s))
        rows_s += [jnp.zeros((chunk, heads * hk), BF16)] * (nsub - 1 - s)
        keys.append(jnp.concatenate(rows_s, axis=0))
    total_col = jnp.exp(jnp.broadcast_to(total, (V7X_LANES, heads * hk)).T[:, 0:1])
    for hh in range(heads):
        kc = slice(hh * hk, (hh + 1) * hk)
        vc = slice(hh * hv, (hh + 1) * hv)
        v = v_ref[:, vc]
        score_rows = [lax.dot_general(blk(q_dec_bf, s)[:, kc], keys[s][:, kc], (((1,), (1,)), ((), ())),
                                      preferred_element_type=F32) for s in range(nsub)]
        scores = jnp.where(causal, jnp.concatenate(score_rows, axis=0), 0.0).astype(BF16)
        state = state_ref[hh]
        o = (jnp.dot(scores, v, preferred_element_type=F32)
             + jnp.dot(q_in[:, kc], state.astype(BF16), preferred_element_type=F32))
        state_ref[hh] = state * total_col[kc] + lax.dot_general(
            k_out[:, kc], v, (((0,), (0,)), ((), ())), preferred_element_type=F32)
        ms = jnp.mean(o * o, axis=-1, keepdims=True)
        o = o * lax.rsqrt(ms + eps) * nw_ref[...]
        o_ref[:, vc] = (o * _silu(go_ref[:, vc].astype(F32))).astype(o_ref.dtype)


def _gla_scan(qkvg, log_g, norm_w, batch, key_dim, value_dim, n_heads):
    t = qkvg.shape[0]
    hk, hv = key_dim // n_heads, value_dim // n_heads
    seqlen = t // batch
    rb = _tile(seqlen, GLA_ROW_BLOCK, GLA_CHUNK)
    nb = seqlen // rb
    hpb = GLA_HEADS_PER_STEP if n_heads % GLA_HEADS_PER_STEP == 0 else 1
    kw, vw = hpb * hk, hpb * hv
    assert rb % GLA_CHUNK == 0 and (2 * key_dim) % vw == 0 and value_dim % vw == 0
    row = lambda b, h, i: b * nb + i
    k0 = key_dim // kw
    v0 = 2 * key_dim // vw
    g0 = (2 * key_dim + value_dim) // vw
    return pl.pallas_call(
        functools.partial(_gla_scan_kernel, heads=hpb, chunk=GLA_CHUNK, scale=hk ** -0.5,
                          eps=INNER_NORM_EPS),
        grid=(batch, n_heads // hpb, nb),
        in_specs=[
            pl.BlockSpec((rb, kw), lambda b, h, i: (row(b, h, i), h)),
            pl.BlockSpec((rb, kw), lambda b, h, i: (row(b, h, i), k0 + h)),
            pl.BlockSpec((rb, vw), lambda b, h, i: (row(b, h, i), v0 + h)),
            pl.BlockSpec((rb, kw), lambda b, h, i: (row(b, h, i), h)),
            pl.BlockSpec((rb, vw), lambda b, h, i: (row(b, h, i), g0 + h)),
            pl.BlockSpec((1, hv), lambda b, h, i: (0, 0)),
        ],
        out_specs=pl.BlockSpec((rb, vw), lambda b, h, i: (row(b, h, i), h)),
        out_shape=jax.ShapeDtypeStruct((t, value_dim), BF16),
        scratch_shapes=[pltpu.VMEM((hpb, hk, hv), F32)],
        compiler_params=_params("parallel", "parallel", "arbitrary"),
        name="gla_scan",
    )(qkvg, qkvg, qkvg, log_g, qkvg, norm_w.reshape(1, hv).astype(F32))


def _gla_mixer(x, batch, norm_w, in_proj, gk_up, gk_bias, inner_norm_w, out_proj):
    key_dim = gk_up.shape[1]
    value_dim = out_proj.shape[0]
    n_heads = value_dim // inner_norm_w.shape[0]
    main = 2 * key_dim + 2 * value_dim
    h, log_g = _gla_norm_gate(x, norm_w, in_proj[:, main:], gk_up, gk_bias)
    qkvg = _matmul_wcast(h, in_proj, main, BF16)
    o = _gla_scan(qkvg, log_g, inner_norm_w, batch, key_dim, value_dim, n_heads)
    return _matmul_residual(o, out_proj.astype(BF16), x)


def kernel(x, l0_mixer_norm, l0_ssd_in_proj, l0_ssd_conv_w, l0_ssd_conv_b, l0_ssd_dt_bias, l0_ssd_a_log, l0_ssd_d, l0_ssd_norm_w, l0_ssd_out_proj, l0_ffn_norm, l0_ffn_w_gate, l0_ffn_w_up, l0_ffn_w_down, l1_mixer_norm, l1_gla_in_proj, l1_gla_gk_up, l1_gla_gk_bias, l1_gla_norm_w, l1_gla_out_proj, l1_ffn_norm, l1_ffn_w_gate, l1_ffn_w_up, l1_ffn_w_down, final_norm):
    batch, seqlen, d = x.shape
    xf = x.reshape(batch * seqlen, d).astype(F32)
    xf = _ssd_mixer(xf, batch, l0_mixer_norm, l0_ssd_in_proj, l0_ssd_conv_w, l0_ssd_conv_b,
                    l0_ssd_dt_bias, l0_ssd_a_log, l0_ssd_d, l0_ssd_norm_w, l0_ssd_out_proj)
    xf = _ffn(xf, l0_ffn_norm, l0_ffn_w_gate, l0_ffn_w_up, l0_ffn_w_down)
    xf = _gla_mixer(xf, batch, l1_mixer_norm, l1_gla_in_proj, l1_gla_gk_up, l1_gla_gk_bias,
                    l1_gla_norm_w, l1_gla_out_proj)
    xf = _ffn(xf, l1_ffn_norm, l1_ffn_w_gate, l1_ffn_w_up, l1_ffn_w_down)
    return _rmsnorm(xf, final_norm, NORM_EPS, x.dtype).reshape(batch, seqlen, d)
```

```python
import functools

import jax
import jax.numpy as jnp
from jax import lax
from jax.experimental import pallas as pl
from jax.experimental.pallas import tpu as pltpu

F32 = jnp.float32
BF16 = jnp.bfloat16

NORM_EPS = 1e-6
INNER_NORM_EPS = 1e-5
SSD_D_STATE = 128
SSD_CHUNK = 256
GLA_CHUNK = 64
GLA_GATE_NORMALIZER = 16.0
LOG2_E = 1.4426950408889634

V7X_LANES = 128
V7X_BF16_SUBLANES = 16
V7X_VMEM_BYTES = 64 * 1024 * 1024
VMEM_LIMIT_BYTES = V7X_VMEM_BYTES - 8 * 1024 * 1024

MM_RES_VMEM_BUDGET = 46 * 1024 * 1024
WCAST_TILE_M = 2048
WCAST_TILE_N = 512
GATEUP_TILE_N = 256
ROW_TILE = 256
GLA_ROW_BLOCK = 256
GLA_HEADS_PER_STEP = 8


def _params(*semantics):
    return pltpu.CompilerParams(dimension_semantics=semantics, vmem_limit_bytes=VMEM_LIMIT_BYTES)


def _tile(dim, pref, align=V7X_LANES):
    if dim <= pref:
        return dim
    t = (pref // align) * align
    while t >= align:
        if dim % t == 0:
            return t
        t -= align
    return dim


def _silu(x):
    return x * jax.nn.sigmoid(x)


def _softplus(x):
    return jnp.maximum(x, 0.0) + jnp.log1p(jnp.exp(-jnp.abs(x)))


def _pieces3(a):
    hi = a.astype(BF16).astype(F32)
    mid = (a - hi).astype(BF16).astype(F32)
    return hi, mid, a - hi - mid


def _split3_cols(a):
    return jnp.concatenate(_pieces3(a), axis=1).astype(BF16)


def _split3_rows(a):
    return jnp.concatenate(_pieces3(a), axis=0).astype(BF16)


def _rms_normalize(x, w, eps):
    ms = jnp.mean(x * x, axis=-1, keepdims=True)
    return x * lax.rsqrt(ms + eps) * w


def _rmsnorm_kernel(x_ref, w_ref, o_ref, *, eps):
    o_ref[...] = _rms_normalize(x_ref[...], w_ref[...], eps).astype(o_ref.dtype)


def _rmsnorm(x, w, eps, out_dtype):
    t, d = x.shape
    tr = _tile(t, 2 * ROW_TILE, 8)
    return pl.pallas_call(
        functools.partial(_rmsnorm_kernel, eps=eps),
        grid=(t // tr,),
        in_specs=[pl.BlockSpec((tr, d), lambda i: (i, 0)), pl.BlockSpec((1, d), lambda i: (0, 0))],
        out_specs=pl.BlockSpec((tr, d), lambda i: (i, 0)),
        out_shape=jax.ShapeDtypeStruct((t, d), out_dtype),
        compiler_params=_params("parallel"),
        name="rmsnorm",
    )(x, w.reshape(1, d).astype(F32))


def _mm_wcast_kernel(a_ref, w_ref, o_ref):
    w = w_ref[...].astype(BF16)
    o_ref[...] = jnp.dot(a_ref[...], w, preferred_element_type=F32).astype(o_ref.dtype)


def _resident_rows(tm, k):
    return pl.BlockSpec((tm, k), lambda i, j: (i, 0), pipeline_mode=pl.Buffered(1))


def _matmul_wcast(a, w, n_cols, out_dtype):
    m, k = a.shape
    tm, tn = _tile(m, WCAST_TILE_M), _tile(n_cols, WCAST_TILE_N)
    return pl.pallas_call(
        _mm_wcast_kernel,
        grid=(m // tm, n_cols // tn),
        in_specs=[_resident_rows(tm, k), pl.BlockSpec((k, tn), lambda i, j: (0, j))],
        out_specs=pl.BlockSpec((tm, tn), lambda i, j: (i, j)),
        out_shape=jax.ShapeDtypeStruct((m, n_cols), out_dtype),
        compiler_params=_params("parallel", "arbitrary"),
        name="matmul_wcast",
    )(a, w)


def _mm_wcast_conv_kernel(a_ref, w_ref, cw_ref, cb_ref, o_ref, buf_ref, carry_ref, *, halo, tiles_per_seq):
    i, j = pl.program_id(0), pl.program_id(1)
    tm = a_ref.shape[0]
    kw = cw_ref.shape[0]

    @pl.when((i == 0) & (j == 0))
    def _():
        carry_ref[...] = jnp.zeros_like(carry_ref)

    acc = jnp.dot(a_ref[...], w_ref[...].astype(BF16), preferred_element_type=F32)
    buf_ref[0:halo, :] = jnp.where(lax.rem(i, tiles_per_seq) == 0, 0.0, carry_ref[j])
    buf_ref[halo:halo + tm, :] = acc
    carry_ref[j] = acc[tm - halo:, :]
    out = cb_ref[...] + acc * cw_ref[kw - 1:kw, :]
    for tap in range(kw - 1):
        off = halo - (kw - 1) + tap
        out = out + buf_ref[off:off + tm, :] * cw_ref[tap:tap + 1, :]
    o_ref[...] = _silu(out).astype(o_ref.dtype)


def _matmul_wcast_conv(a, w, col0, conv_w, conv_b, seqlen):
    m, k = a.shape
    c, kw = conv_w.shape
    halo = V7X_BF16_SUBLANES
    tm, tn = _tile(seqlen, WCAST_TILE_M, halo), _tile(c, WCAST_TILE_N)
    assert col0 % tn == 0 and seqlen % tm == 0 and kw - 1 <= halo
    return pl.pallas_call(
        functools.partial(_mm_wcast_conv_kernel, halo=halo, tiles_per_seq=seqlen // tm),
        grid=(m // tm, c // tn),
        in_specs=[_resident_rows(tm, k),
                  pl.BlockSpec((k, tn), lambda i, j: (0, col0 // tn + j)),
                  pl.BlockSpec((kw, tn), lambda i, j: (0, j)),
                  pl.BlockSpec((1, tn), lambda i, j: (0, j))],
        out_specs=pl.BlockSpec((tm, tn), lambda i, j: (i, j)),
        out_shape=jax.ShapeDtypeStruct((m, c), BF16),
        scratch_shapes=[pltpu.VMEM((tm + halo, tn), F32), pltpu.VMEM((c // tn, halo, tn), F32)],
        compiler_params=_params("arbitrary", "arbitrary"),
        name="matmul_wcast_conv",
    )(a, w, conv_w.T.astype(F32), conv_b.reshape(1, c).astype(F32))


def _mm_res_kernel(a_ref, w_ref, r_ref, o_ref):
    o_ref[...] = r_ref[...] + jnp.dot(a_ref[...], w_ref[...], preferred_element_type=F32)


def _res_tiles(m, k, n):
    best = None
    for tm in (2048, 1024, 512):
        for tn in (1024, 512, 256):
            if m % tm or n % tn:
                continue
            need = k * tn * 2 + 2 * tm * k * 2 + 5 * tm * tn * 4
            if need <= MM_RES_VMEM_BUDGET and (best is None or tm * tn > best[0] * best[1]):
                best = (tm, tn)
    return best if best is not None else (_tile(m, 256, 8), _tile(n, V7X_LANES))


def _matmul_residual(a, w, res):
    m, k = a.shape
    n = w.shape[1]
    tm, tn = _res_tiles(m, k, n)
    return pl.pallas_call(
        _mm_res_kernel,
        grid=(n // tn, m // tm),
        in_specs=[
            pl.BlockSpec((tm, k), lambda j, i: (i, 0)),
            pl.BlockSpec((k, tn), lambda j, i: (0, j), pipeline_mode=pl.Buffered(1)),
            pl.BlockSpec((tm, tn), lambda j, i: (i, j)),
        ],
        out_specs=pl.BlockSpec((tm, tn), lambda j, i: (i, j)),
        out_shape=jax.ShapeDtypeStruct((m, n), F32),
        compiler_params=_params("parallel", "arbitrary"),
        name="matmul_residual",
    )(a, w, res)


def _gateup_kernel(h_ref, wg_ref, wu_ref, o_ref):
    h = h_ref[...]
    g = jnp.dot(h, wg_ref[...].astype(BF16), preferred_element_type=F32)
    u = jnp.dot(h, wu_ref[...].astype(BF16), preferred_element_type=F32)
    o_ref[...] = (_silu(g) * u).astype(o_ref.dtype)


def _ffn_gateup(h, wg, wu):
    m, k = h.shape
    n = wg.shape[1]
    tm, tn = _tile(m, WCAST_TILE_M), _tile(n, GATEUP_TILE_N)
    wspec = pl.BlockSpec((k, tn), lambda i, j: (0, j))
    return pl.pallas_call(
        _gateup_kernel,
        grid=(m // tm, n // tn),
        in_specs=[_resident_rows(tm, k), wspec, wspec],
        out_specs=pl.BlockSpec((tm, tn), lambda i, j: (i, j)),
        out_shape=jax.ShapeDtypeStruct((m, n), BF16),
        compiler_params=_params("parallel", "arbitrary"),
        name="ffn_gateup",
    )(h, wg, wu)


def _ffn(x, norm_w, w_gate, w_up, w_down):
    h = _rmsnorm(x, norm_w, NORM_EPS, BF16)
    return _matmul_residual(_ffn_gateup(h, w_gate, w_up), w_down.astype(BF16), x)


def _ssd_norm_dt_kernel(x_ref, nw_ref, w_ref, bias_ref, alog_ref, h_ref, dt_ref, cs_ref, cst_ref, *,
                        n_groups, eps):
    h = _rms_normalize(x_ref[...], nw_ref[...], eps).astype(BF16)
    h_ref[...] = h
    lc = h.shape[0]
    nh = bias_ref.shape[1]
    r = nh // n_groups
    raw = jnp.dot(h, w_ref[...].astype(BF16), preferred_element_type=F32)[:, :nh]
    dt = _softplus(raw + bias_ref[...])
    la = dt * (-LOG2_E * jnp.exp(alog_ref[...]))
    row = lax.broadcasted_iota(jnp.int32, (lc, lc), 0)
    col = lax.broadcasted_iota(jnp.int32, (lc, lc), 1)
    tril3 = jnp.concatenate([(row >= col).astype(BF16)] * 3, axis=1)
    cs = jnp.dot(tril3, _split3_rows(la), preferred_element_type=F32)
    cst = cs.T
    for g in range(n_groups):
        dt_ref[0, g] = dt[:, g * r:(g + 1) * r]
        cs_ref[0, g] = cs[:, g * r:(g + 1) * r]
        cst_ref[0, g] = cst[g * r:(g + 1) * r, :]


def _ssd_norm_dt(x, norm_w, in_proj, col0, dt_bias, a_log, n_groups):
    t, d = x.shape
    nh = dt_bias.shape[0]
    r = nh // n_groups
    lc = SSD_CHUNK
    nblk = t // lc
    assert col0 % V7X_LANES == 0 and nh <= V7X_LANES
    rows = pl.BlockSpec((lc, d), lambda i: (i, 0))
    vec = pl.BlockSpec((1, nh), lambda i: (0, 0))
    tok = pl.BlockSpec((1, n_groups, lc, r), lambda i: (i, 0, 0, 0))
    head = pl.BlockSpec((1, n_groups, r, lc), lambda i: (i, 0, 0, 0))
    return pl.pallas_call(
        functools.partial(_ssd_norm_dt_kernel, n_groups=n_groups, eps=NORM_EPS),
        grid=(nblk,),
        in_specs=[rows, pl.BlockSpec((1, d), lambda i: (0, 0)),
                  pl.BlockSpec((d, V7X_LANES), lambda i: (0, col0 // V7X_LANES)), vec, vec],
        out_specs=[rows, tok, tok, head],
        out_shape=[jax.ShapeDtypeStruct((t, d), BF16),
                   jax.ShapeDtypeStruct((nblk, n_groups, lc, r), F32),
                   jax.ShapeDtypeStruct((nblk, n_groups, lc, r), F32),
                   jax.ShapeDtypeStruct((nblk, n_groups, r, lc), F32)],
        compiler_params=_params("parallel"),
        name="ssd_norm_dt",
    )(x, norm_w.reshape(1, d).astype(F32), in_proj,
      dt_bias.reshape(1, nh).astype(F32), a_log.reshape(1, nh).astype(F32))


def _ssd_scan_kernel(x_ref, b_ref, c_ref, z_ref, dt_ref, cs_ref, cst_ref, dskip_ref, nw_ref,
                     o_ref, state_ref, *, heads, head_dim, eps):
    @pl.when(pl.program_id(2) == 0)
    def _():
        state_ref[...] = jnp.zeros_like(state_ref)

    lc = x_ref.shape[0]
    half = lc // 2
    gw = heads * head_dim
    pair = 2 * head_dim
    x = x_ref[...].astype(F32)
    bm = b_ref[...]
    cm = c_ref[...]
    dt = dt_ref[0, 0]
    cs = cs_ref[0, 0]
    cst = cst_ref[0, 0]

    e_row = lax.broadcasted_iota(jnp.int32, (3 * heads, gw), 0) % heads
    e_col = lax.broadcasted_iota(jnp.int32, (3 * heads, gw), 1) // head_dim
    expand_m = (e_row == e_col).astype(BF16)
    expand = lambda a: jnp.dot(_split3_cols(a), expand_m, preferred_element_type=F32)

    cs_end = cs[lc - 1:lc, :]
    dt_e = expand(dt)
    ecs_e = expand(jnp.exp2(cs))
    dtd_e = expand(dt * jnp.exp2(cs_end - cs))

    xdt = (x * dt_e).astype(BF16)
    cb = lax.dot_general(cm, bm, (((1,), (1,)), ((), ())), preferred_element_type=F32)
    cb_tl, cb_bl, cb_br = cb[:half, :half], cb[half:, :half], cb[half:, half:]
    tri = lax.broadcasted_iota(jnp.int32, (half, half), 0) >= lax.broadcasted_iota(jnp.int32, (half, half), 1)
    first_half = lax.broadcasted_iota(jnp.int32, (lc, pair), 1) < head_dim
    zero_tr = jnp.zeros((half, half), F32)

    def masked_cb(r):
        col, row = cs[:, r:r + 1], cst[r:r + 1, :]
        m_tl = cb_tl * jnp.exp2(jnp.where(tri, col[:half] - row[:, :half], -jnp.inf))
        m_bl = cb_bl * jnp.exp2(col[half:] - row[:, :half])
        m_br = cb_br * jnp.exp2(jnp.where(tri, col[half:] - row[:, half:], -jnp.inf))
        top = jnp.concatenate([m_tl, zero_tr], axis=1)
        return jnp.concatenate([top, jnp.concatenate([m_bl, m_br], axis=1)], axis=0).astype(BF16)

    ys = []
    for p in range(heads // 2):
        xp = xdt[:, p * pair:(p + 1) * pair]
        xa = jnp.where(first_half, xp, jnp.zeros_like(xp))
        xb = jnp.where(first_half, jnp.zeros_like(xp), xp)
        ys.append(jnp.dot(masked_cb(2 * p), xa, preferred_element_type=F32)
                  + jnp.dot(masked_cb(2 * p + 1), xb, preferred_element_type=F32))
    y = jnp.concatenate(ys, axis=1) if len(ys) > 1 else ys[0]

    state = state_ref[...]
    y = y + jnp.dot(cm, state.astype(BF16), preferred_element_type=F32) * ecs_e
    state_ref[...] = state * ecs_e[lc - 1:lc, :] + lax.dot_general(
        bm, (x * dtd_e).astype(BF16), (((0,), (0,)), ((), ())), preferred_element_type=F32)

    y = y + x * dskip_ref[...]
    y = y * _silu(z_ref[...].astype(F32))
    ms = jnp.mean(y * y, axis=-1, keepdims=True)
    o_ref[...] = (y * lax.rsqrt(ms + eps) * nw_ref[...]).astype(o_ref.dtype)


def _ssd_scan(z, xbc, dt, cs, cst, d_skip, norm_w, batch, d_inner, n_groups, heads):
    t = z.shape[0]
    lc = SSD_CHUNK
    nc = t // batch // lc
    n = SSD_D_STATE
    gw = d_inner // n_groups
    head_dim = gw // heads
    assert heads % 2 == 0 and 2 * head_dim == V7X_LANES
    row = lambda b, g, c: b * nc + c
    small = lambda shape: pl.BlockSpec(shape, lambda b, g, c: (row(b, g, c), g, 0, 0))
    return pl.pallas_call(
        functools.partial(_ssd_scan_kernel, heads=heads, head_dim=head_dim, eps=INNER_NORM_EPS),
        grid=(batch, n_groups, nc),
        in_specs=[
            pl.BlockSpec((lc, gw), lambda b, g, c: (row(b, g, c), g)),
            pl.BlockSpec((lc, n), lambda b, g, c: (row(b, g, c), d_inner // n + g)),
            pl.BlockSpec((lc, n), lambda b, g, c: (row(b, g, c), d_inner // n + n_groups + g)),
            pl.BlockSpec((lc, gw), lambda b, g, c: (row(b, g, c), g)),
            small((1, 1, lc, heads)), small((1, 1, lc, heads)), small((1, 1, heads, lc)),
            pl.BlockSpec((1, gw), lambda b, g, c: (0, g)),
            pl.BlockSpec((1, gw), lambda b, g, c: (0, g)),
        ],
        out_specs=pl.BlockSpec((lc, gw), lambda b, g, c: (row(b, g, c), g)),
        out_shape=jax.ShapeDtypeStruct((t, d_inner), BF16),
        scratch_shapes=[pltpu.VMEM((n, gw), F32)],
        compiler_params=_params("parallel", "parallel", "arbitrary"),
        name="ssd_scan",
    )(xbc, xbc, xbc, z, dt, cs, cst,
      jnp.repeat(d_skip.astype(F32), head_dim).reshape(1, d_inner),
      norm_w.reshape(1, d_inner).astype(F32))


def _ssd_mixer(x, batch, norm_w, in_proj, conv_w, conv_b, dt_bias, a_log, d_skip, inner_norm_w, out_proj):
    d_inner = out_proj.shape[0]
    conv_dim = conv_w.shape[0]
    n_heads = dt_bias.shape[0]
    n_groups = (conv_dim - d_inner) // 2 // SSD_D_STATE
    main = d_inner + conv_dim
    h, dt, cs, cst = _ssd_norm_dt(x, norm_w, in_proj, main, dt_bias, a_log, n_groups)
    z = _matmul_wcast(h, in_proj, d_inner, BF16)
    xbc = _matmul_wcast_conv(h, in_proj, d_inner, conv_w, conv_b, x.shape[0] // batch)
    y = _ssd_scan(z, xbc, dt, cs, cst, d_skip, inner_norm_w, batch, d_inner, n_groups,
                  n_heads // n_groups)
    return _matmul_residual(y, out_proj.astype(BF16), x)


def _gla_norm_gate_kernel(x_ref, nw_ref, wlow_ref, up_ref, bias_ref, h_ref, o_ref, *, eps):
    h = _rms_normalize(x_ref[...], nw_ref[...], eps).astype(BF16)
    h_ref[...] = h
    rank = up_ref.shape[0]
    low = jnp.dot(h, wlow_ref[...].astype(BF16), preferred_element_type=F32)[:, :rank]
    l_hi, l_mid, l_lo = _pieces3(low)
    u_hi, u_mid, u_lo = _pieces3(up_ref[...])
    lhs = jnp.concatenate([l_hi, l_mid, l_lo, l_hi, l_mid, l_hi], axis=1).astype(BF16)
    rhs = jnp.concatenate([u_hi, u_hi, u_hi, u_mid, u_mid, u_lo], axis=0).astype(BF16)
    x = jnp.dot(lhs, rhs, preferred_element_type=F32) + bias_ref[...]
    o_ref[...] = (jnp.minimum(x, 0.0) - jnp.log1p(jnp.exp(-jnp.abs(x)))) * (1.0 / GLA_GATE_NORMALIZER)


def _gla_norm_gate(x, norm_w, w_low, gk_up, gk_bias):
    t, d = x.shape
    rank, kd = gk_up.shape
    pad = (-rank) % V7X_LANES
    w_low = jnp.pad(w_low.astype(F32), ((0, 0), (0, pad)))
    gk_up = gk_up.astype(F32)
    tr = _tile(t, 2 * ROW_TILE, 8)
    rows = pl.BlockSpec((tr, d), lambda i: (i, 0))
    return pl.pallas_call(
        functools.partial(_gla_norm_gate_kernel, eps=NORM_EPS),
        grid=(t // tr,),
        in_specs=[rows, pl.BlockSpec((1, d), lambda i: (0, 0)),
                  pl.BlockSpec((d, rank + pad), lambda i: (0, 0)),
                  pl.BlockSpec((rank, kd), lambda i: (0, 0)),
                  pl.BlockSpec((1, kd), lambda i: (0, 0))],
        out_specs=[rows, pl.BlockSpec((tr, kd), lambda i: (i, 0))],
        out_shape=[jax.ShapeDtypeStruct((t, d), BF16), jax.ShapeDtypeStruct((t, kd), F32)],
        compiler_params=_params("parallel"),
        name="gla_norm_gate",
    )(x, norm_w.reshape(1, d).astype(F32), w_low, gk_up, gk_bias.reshape(1, kd).astype(F32))


def _gla_scan_kernel(q_ref, k_ref, v_ref, g_ref, go_ref, nw_ref, o_ref, state_ref, *,
                     heads, chunk, scale, eps):
    @pl.when(pl.program_id(2) == 0)
    def _():
        state_ref[...] = jnp.zeros_like(state_ref)

    rows = q_ref.shape[0]
    nsub = rows // chunk
    hk = q_ref.shape[1] // heads
    hv = v_ref.shape[1] // heads
    shift = chunk.bit_length() - 1
    ri = lax.broadcasted_iota(jnp.int32, (rows, rows), 0)
    ci = lax.broadcasted_iota(jnp.int32, (rows, rows), 1)
    causal = ri >= ci
    in_chunk_causal = causal & (lax.shift_right_logical(ri, shift) == lax.shift_right_logical(ci, shift))
    cumsum_m = jnp.concatenate([in_chunk_causal.astype(BF16)] * 3, axis=1)
    blk = lambda a, s: a[s * chunk:(s + 1) * chunk]
    q = q_ref[...].astype(F32) * scale
    k = k_ref[...].astype(F32)
    cs = jnp.dot(cumsum_m, _split3_rows(g_ref[...]), preferred_element_type=F32)
    tot = [cs[(s + 1) * chunk - 1:(s + 1) * chunk, :] for s in range(nsub)]
    before = [jnp.zeros_like(tot[0])]
    for s in range(nsub):
        before.append(before[s] + tot[s])
    total = before[nsub]
    q_dec = q * jnp.exp(cs)
    q_dec_bf = q_dec.astype(BF16)
    k_inv = (k * jnp.exp(-cs)).astype(BF16)
    k_end = [blk(k, s) * jnp.exp(tot[s] - blk(cs, s)) for s in range(nsub)]
    q_in = jnp.concatenate([blk(q_dec, s) * jnp.exp(before[s]) for s in range(nsub)], axis=0).astype(BF16)
    k_out = jnp.concatenate([k_end[s] * jnp.exp(total - before[s + 1]) for s in range(nsub)],
                            axis=0).astype(BF16)
    keys = []
    for s in range(nsub):
        rows_s = [(k_end[j] * jnp.exp(before[s] - before[j + 1])).astype(BF16) for j in range(s)]
        rows_s.append(blk(k_inv, s))
        rows_s += [jnp.zeros((chunk, heads * hk), BF16)] * (nsub - 1 - s)
        keys.append(jnp.concatenate(rows_s, axis=0))
    total_col = jnp.exp(jnp.broadcast_to(total, (V7X_LANES, heads * hk)).T[:, 0:1])
    for hh in range(heads):
        kc = slice(hh * hk, (hh + 1) * hk)
        vc = slice(hh * hv, (hh + 1) * hv)
        v = v_ref[:, vc]
        score_rows = [lax.dot_general(blk(q_dec_bf, s)[:, kc], keys[s][:, kc], (((1,), (1,)), ((), ())),
                                      preferred_element_type=F32) for s in range(nsub)]
        scores = jnp.where(causal, jnp.concatenate(score_rows, axis=0), 0.0).astype(BF16)
        state = state_ref[hh]
        o = (jnp.dot(scores, v, preferred_element_type=F32)
             + jnp.dot(q_in[:, kc], state.astype(BF16), preferred_element_type=F32))
        state_ref[hh] = state * total_col[kc] + lax.dot_general(
            k_out[:, kc], v, (((0,), (0,)), ((), ())), preferred_element_type=F32)
        ms = jnp.mean(o * o, axis=-1, keepdims=True)
        o = o * lax.rsqrt(ms + eps) * nw_ref[...]
        o_ref[:, vc] = (o * _silu(go_ref[:, vc].astype(F32))).astype(o_ref.dtype)


def _gla_scan(qkvg, log_g, norm_w, batch, key_dim, value_dim, n_heads):
    t = qkvg.shape[0]
    hk, hv = key_dim // n_heads, value_dim // n_heads
    seqlen = t // batch
    rb = _tile(seqlen, GLA_ROW_BLOCK, GLA_CHUNK)
    nb = seqlen // rb
    hpb = GLA_HEADS_PER_STEP if n_heads % GLA_HEADS_PER_STEP == 0 else 1
    kw, vw = hpb * hk, hpb * hv
    assert rb % GLA_CHUNK == 0 and (2 * key_dim) % vw == 0 and value_dim % vw == 0
    row = lambda b, h, i: b * nb + i
    k0 = key_dim // kw
    v0 = 2 * key_dim // vw
    g0 = (2 * key_dim + value_dim) // vw
    return pl.pallas_call(
        functools.partial(_gla_scan_kernel, heads=hpb, chunk=GLA_CHUNK, scale=hk ** -0.5,
                          eps=INNER_NORM_EPS),
        grid=(batch, n_heads // hpb, nb),
        in_specs=[
            pl.BlockSpec((rb, kw), lambda b, h, i: (row(b, h, i), h)),
            pl.BlockSpec((rb, kw), lambda b, h, i: (row(b, h, i), k0 + h)),
            pl.BlockSpec((rb, vw), lambda b, h, i: (row(b, h, i), v0 + h)),
            pl.BlockSpec((rb, kw), lambda b, h, i: (row(b, h, i), h)),
            pl.BlockSpec((rb, vw), lambda b, h, i: (row(b, h, i), g0 + h)),
            pl.BlockSpec((1, hv), lambda b, h, i: (0, 0)),
        ],
        out_specs=pl.BlockSpec((rb, vw), lambda b, h, i: (row(b, h, i), h)),
        out_shape=jax.ShapeDtypeStruct((t, value_dim), BF16),
        scratch_shapes=[pltpu.VMEM((hpb, hk, hv), F32)],
        compiler_params=_params("parallel", "parallel", "arbitrary"),
        name="gla_scan",
    )(qkvg, qkvg, qkvg, log_g, qkvg, norm_w.reshape(1, hv).astype(F32))


def _gla_mixer(x, batch, norm_w, in_proj, gk_up, gk_bias, inner_norm_w, out_proj):
    key_dim = gk_up.shape[1]
    value_dim = out_proj.shape[0]
    n_heads = value_dim // inner_norm_w.shape[0]
    main = 2 * key_dim + 2 * value_dim
    h, log_g = _gla_norm_gate(x, norm_w, in_proj[:, main:], gk_up, gk_bias)
    qkvg = _matmul_wcast(h, in_proj[:, :main], main, BF16)
    o = _gla_scan(qkvg, log_g, inner_norm_w, batch, key_dim, value_dim, n_heads)
    return _matmul_residual(o, out_proj.astype(BF16), x)


def kernel(x, l0_mixer_norm, l0_ssd_in_proj, l0_ssd_conv_w, l0_ssd_conv_b, l0_ssd_dt_bias, l0_ssd_a_log, l0_ssd_d, l0_ssd_norm_w, l0_ssd_out_proj, l0_ffn_norm, l0_ffn_w_gate, l0_ffn_w_up, l0_ffn_w_down, l1_mixer_norm, l1_gla_in_proj, l1_gla_gk_up, l1_gla_gk_bias, l1_gla_norm_w, l1_gla_out_proj, l1_ffn_norm, l1_ffn_w_gate, l1_ffn_w_up, l1_ffn_w_down, final_norm):
    batch, seqlen, d = x.shape
    xf = x.reshape(batch * seqlen, d).astype(F32)
    xf = _ssd_mixer(xf, batch, l0_mixer_norm, l0_ssd_in_proj, l0_ssd_conv_w, l0_ssd_conv_b,
                    l0_ssd_dt_bias, l0_ssd_a_log, l0_ssd_d, l0_ssd_norm_w, l0_ssd_out_proj)
    xf = _ffn(xf, l0_ffn_norm, l0_ffn_w_gate, l0_ffn_w_up, l0_ffn_w_down)
    xf = _gla_mixer(xf, batch, l1_mixer_norm, l1_gla_in_proj, l1_gla_gk_up, l1_gla_gk_bias,
                    l1_gla_norm_w, l1_gla_out_proj)
    xf = _ffn(xf, l1_ffn_norm, l1_ffn_w_gate, l1_ffn_w_up, l1_ffn_w_down)
    return _rmsnorm(xf, final_norm, NORM_EPS, x.dtype).reshape(batch, seqlen, d)
```

```python
import functools

import jax
import jax.numpy as jnp
from jax import lax
from jax.experimental import pallas as pl
from jax.experimental.pallas import tpu as pltpu

F32 = jnp.float32
BF16 = jnp.bfloat16

NORM_EPS = 1e-6
INNER_NORM_EPS = 1e-5
SSD_D_STATE = 128
SSD_CHUNK = 256
GLA_CHUNK = 64
GLA_GATE_NORMALIZER = 16.0
LOG2_E = 1.4426950408889634

V7X_LANES = 128
V7X_BF16_SUBLANES = 16
V7X_VMEM_BYTES = 64 * 1024 * 1024
VMEM_LIMIT_BYTES = V7X_VMEM_BYTES - 8 * 1024 * 1024

MM_RES_VMEM_BUDGET = 46 * 1024 * 1024
WCAST_TILE_M = 2048
WCAST_TILE_N = 512
GATEUP_TILE_N = 256
ROW_TILE = 256
GLA_ROW_BLOCK = 256
GLA_HEADS_PER_STEP = 8


def _params(*semantics):
    return pltpu.CompilerParams(dimension_semantics=semantics, vmem_limit_bytes=VMEM_LIMIT_BYTES)


def _tile(dim, pref, align=V7X_LANES):
    if dim <= pref:
        return dim
    t = (pref // align) * align
    while t >= align:
        if dim % t == 0:
            return t
        t -= align
    return dim


def _silu(x):
    return x * jax.nn.sigmoid(x)


def _softplus(x):
    return jnp.maximum(x, 0.0) + jnp.log1p(jnp.exp(-jnp.abs(x)))


def _pieces3(a):
    hi = a.astype(BF16).astype(F32)
    mid = (a - hi).astype(BF16).astype(F32)
    return hi, mid, a - hi - mid


def _split3_cols(a):
    return jnp.concatenate(_pieces3(a), axis=1).astype(BF16)


def _split3_rows(a):
    return jnp.concatenate(_pieces3(a), axis=0).astype(BF16)


def _rms_normalize(x, w, eps):
    ms = jnp.mean(x * x, axis=-1, keepdims=True)
    return x * lax.rsqrt(ms + eps) * w


def _rmsnorm_kernel(x_ref, w_ref, o_ref, *, eps):
    o_ref[...] = _rms_normalize(x_ref[...], w_ref[...], eps).astype(o_ref.dtype)


def _rmsnorm(x, w, eps, out_dtype):
    t, d = x.shape
    tr = _tile(t, 2 * ROW_TILE, 8)
    return pl.pallas_call(
        functools.partial(_rmsnorm_kernel, eps=eps),
        grid=(t // tr,),
        in_specs=[pl.BlockSpec((tr, d), lambda i: (i, 0)), pl.BlockSpec((1, d), lambda i: (0, 0))],
        out_specs=pl.BlockSpec((tr, d), lambda i: (i, 0)),
        out_shape=jax.ShapeDtypeStruct((t, d), out_dtype),
        compiler_params=_params("parallel"),
        name="rmsnorm",
    )(x, w.reshape(1, d).astype(F32))


def _mm_wcast_kernel(a_ref, w_ref, o_ref):
    w = w_ref[...].astype(BF16)
    o_ref[...] = jnp.dot(a_ref[...], w, preferred_element_type=F32).astype(o_ref.dtype)


def _resident_rows(tm, k):
    return pl.BlockSpec((tm, k), lambda i, j: (i, 0), pipeline_mode=pl.Buffered(1))


def _matmul_wcast(a, w, n_cols, out_dtype):
    m, k = a.shape
    tm, tn = _tile(m, WCAST_TILE_M), _tile(n_cols, WCAST_TILE_N)
    return pl.pallas_call(
        _mm_wcast_kernel,
        grid=(m // tm, n_cols // tn),
        in_specs=[_resident_rows(tm, k), pl.BlockSpec((k, tn), lambda i, j: (0, j))],
        out_specs=pl.BlockSpec((tm, tn), lambda i, j: (i, j)),
        out_shape=jax.ShapeDtypeStruct((m, n_cols), out_dtype),
        compiler_params=_params("parallel", "arbitrary"),
        name="matmul_wcast",
    )(a, w)


def _mm_wcast_conv_kernel(a_ref, w_ref, cw_ref, cb_ref, o_ref, buf_ref, carry_ref, *, halo, tiles_per_seq):
    i, j = pl.program_id(0), pl.program_id(1)
    tm = a_ref.shape[0]
    kw = cw_ref.shape[0]

    @pl.when((i == 0) & (j == 0))
    def _():
        carry_ref[...] = jnp.zeros_like(carry_ref)

    acc = jnp.dot(a_ref[...], w_ref[...].astype(BF16), preferred_element_type=F32)
    buf_ref[0:halo, :] = jnp.where(lax.rem(i, tiles_per_seq) == 0, 0.0, carry_ref[j])
    buf_ref[halo:halo + tm, :] = acc
    carry_ref[j] = acc[tm - halo:, :]
    out = cb_ref[...] + acc * cw_ref[kw - 1:kw, :]
    for tap in range(kw - 1):
        off = halo - (kw - 1) + tap
        out = out + buf_ref[off:off + tm, :] * cw_ref[tap:tap + 1, :]
    o_ref[...] = _silu(out).astype(o_ref.dtype)


def _matmul_wcast_conv(a, w, col0, conv_w, conv_b, seqlen):
    m, k = a.shape
    c, kw = conv_w.shape
    halo = V7X_BF16_SUBLANES
    tm, tn = _tile(seqlen, WCAST_TILE_M, halo), _tile(c, WCAST_TILE_N)
    assert col0 % tn == 0 and seqlen % tm == 0 and kw - 1 <= halo
    return pl.pallas_call(
        functools.partial(_mm_wcast_conv_kernel, halo=halo, tiles_per_seq=seqlen // tm),
        grid=(m // tm, c // tn),
        in_specs=[_resident_rows(tm, k),
                  pl.BlockSpec((k, tn), lambda i, j: (0, col0 // tn + j)),
                  pl.BlockSpec((kw, tn), lambda i, j: (0, j)),
                  pl.BlockSpec((1, tn), lambda i, j: (0, j))],
        out_specs=pl.BlockSpec((tm, tn), lambda i, j: (i, j)),
        out_shape=jax.ShapeDtypeStruct((m, c), BF16),
        scratch_shapes=[pltpu.VMEM((tm + halo, tn), F32), pltpu.VMEM((c // tn, halo, tn), F32)],
        compiler_params=_params("arbitrary", "arbitrary"),
        name="matmul_wcast_conv",
    )(a, w, conv_w.T.astype(F32), conv_b.reshape(1, c).astype(F32))


def _mm_res_kernel(a_ref, w_ref, r_ref, o_ref):
    o_ref[...] = r_ref[...] + jnp.dot(a_ref[...], w_ref[...], preferred_element_type=F32)


def _res_tiles(m, k, n):
    best = None
    for tm in (2048, 1024, 512):
        for tn in (1024, 512, 256):
            if m % tm or n % tn:
                continue
            need = k * tn * 2 + 2 * tm * k * 2 + 5 * tm * tn * 4
            if need <= MM_RES_VMEM_BUDGET and (best is None or tm * tn > best[0] * best[1]):
                best = (tm, tn)
    return best if best is not None else (_tile(m, 256, 8), _tile(n, V7X_LANES))


def _matmul_residual(a, w, res):
    m, k = a.shape
    n = w.shape[1]
    tm, tn = _res_tiles(m, k, n)
    return pl.pallas_call(
        _mm_res_kernel,
        grid=(n // tn, m // tm),
        in_specs=[
            pl.BlockSpec((tm, k), lambda j, i: (i, 0)),
            pl.BlockSpec((k, tn), lambda j, i: (0, j), pipeline_mode=pl.Buffered(1)),
            pl.BlockSpec((tm, tn), lambda j, i: (i, j)),
        ],
        out_specs=pl.BlockSpec((tm, tn), lambda j, i: (i, j)),
        out_shape=jax.ShapeDtypeStruct((m, n), F32),
        compiler_params=_params("parallel", "arbitrary"),
        name="matmul_residual",
    )(a, w, res)


def _gateup_kernel(h_ref, wg_ref, wu_ref, o_ref):
    h = h_ref[...]
    g = jnp.dot(h, wg_ref[...].astype(BF16), preferred_element_type=F32)
    u = jnp.dot(h, wu_ref[...].astype(BF16), preferred_element_type=F32)
    o_ref[...] = (_silu(g) * u).astype(o_ref.dtype)


def _ffn_gateup(h, wg, wu):
    m, k = h.shape
    n = wg.shape[1]
    tm, tn = _tile(m, WCAST_TILE_M), _tile(n, GATEUP_TILE_N)
    wspec = pl.BlockSpec((k, tn), lambda i, j: (0, j))
    return pl.pallas_call(
        _gateup_kernel,
        grid=(m // tm, n // tn),
        in_specs=[_resident_rows(tm, k), wspec, wspec],
        out_specs=pl.BlockSpec((tm, tn), lambda i, j: (i, j)),
        out_shape=jax.ShapeDtypeStruct((m, n), BF16),
        compiler_params=_params("parallel", "arbitrary"),
        name="ffn_gateup",
    )(h, wg, wu)


def _ffn(x, norm_w, w_gate, w_up, w_down):
    h = _rmsnorm(x, norm_w, NORM_EPS, BF16)
    return _matmul_residual(_ffn_gateup(h, w_gate, w_up), w_down.astype(BF16), x)


def _ssd_norm_dt_kernel(x_ref, nw_ref, w_ref, bias_ref, alog_ref, h_ref, dt_ref, cs_ref, cst_ref, *,
                        n_groups, eps):
    h = _rms_normalize(x_ref[...], nw_ref[...], eps).astype(BF16)
    h_ref[...] = h
    lc = h.shape[0]
    nh = bias_ref.shape[1]
    r = nh // n_groups
    raw = jnp.dot(h, w_ref[...].astype(BF16), preferred_element_type=F32)[:, :nh]
    dt = _softplus(raw + bias_ref[...])
    la = dt * (-LOG2_E * jnp.exp(alog_ref[...]))
    row = lax.broadcasted_iota(jnp.int32, (lc, lc), 0)
    col = lax.broadcasted_iota(jnp.int32, (lc, lc), 1)
    tril3 = jnp.concatenate([(row >= col).astype(BF16)] * 3, axis=1)
    cs = jnp.dot(tril3, _split3_rows(la), preferred_element_type=F32)
    cst = cs.T
    for g in range(n_groups):
        dt_ref[0, g] = dt[:, g * r:(g + 1) * r]
        cs_ref[0, g] = cs[:, g * r:(g + 1) * r]
        cst_ref[0, g] = cst[g * r:(g + 1) * r, :]


def _ssd_norm_dt(x, norm_w, in_proj, col0, dt_bias, a_log, n_groups):
    t, d = x.shape
    nh = dt_bias.shape[0]
    r = nh // n_groups
    lc = SSD_CHUNK
    nblk = t // lc
    assert col0 % V7X_LANES == 0 and nh <= V7X_LANES
    rows = pl.BlockSpec((lc, d), lambda i: (i, 0))
    vec = pl.BlockSpec((1, nh), lambda i: (0, 0))
    tok = pl.BlockSpec((1, n_groups, lc, r), lambda i: (i, 0, 0, 0))
    head = pl.BlockSpec((1, n_groups, r, lc), lambda i: (i, 0, 0, 0))
    return pl.pallas_call(
        functools.partial(_ssd_norm_dt_kernel, n_groups=n_groups, eps=NORM_EPS),
        grid=(nblk,),
        in_specs=[rows, pl.BlockSpec((1, d), lambda i: (0, 0)),
                  pl.BlockSpec((d, V7X_LANES), lambda i: (0, col0 // V7X_LANES)), vec, vec],
        out_specs=[rows, tok, tok, head],
        out_shape=[jax.ShapeDtypeStruct((t, d), BF16),
                   jax.ShapeDtypeStruct((nblk, n_groups, lc, r), F32),
                   jax.ShapeDtypeStruct((nblk, n_groups, lc, r), F32),
                   jax.ShapeDtypeStruct((nblk, n_groups, r, lc), F32)],
        compiler_params=_params("parallel"),
        name="ssd_norm_dt",
    )(x, norm_w.reshape(1, d).astype(F32), in_proj,
      dt_bias.reshape(1, nh).astype(F32), a_log.reshape(1, nh).astype(F32))


def _ssd_scan_kernel(x_ref, b_ref, c_ref, z_ref, dt_ref, cs_ref, cst_ref, dskip_ref, nw_ref,
                     o_ref, state_ref, *, heads, head_dim, eps):
    @pl.when(pl.program_id(2) == 0)
    def _():
        state_ref[...] = jnp.zeros_like(state_ref)

    lc = x_ref.shape[0]
    half = lc // 2
    gw = heads * head_dim
    pair = 2 * head_dim
    x = x_ref[...].astype(F32)
    bm = b_ref[...]
    cm = c_ref[...]
    dt = dt_ref[0, 0]
    cs = cs_ref[0, 0]
    cst = cst_ref[0, 0]

    e_row = lax.broadcasted_iota(jnp.int32, (3 * heads, gw), 0) % heads
    e_col = lax.broadcasted_iota(jnp.int32, (3 * heads, gw), 1) // head_dim
    expand_m = (e_row == e_col).astype(BF16)
    expand = lambda a: jnp.dot(_split3_cols(a), expand_m, preferred_element_type=F32)

    cs_end = cs[lc - 1:lc, :]
    dt_e = expand(dt)
    ecs_e = expand(jnp.exp2(cs))
    dtd_e = expand(dt * jnp.exp2(cs_end - cs))

    xdt = (x * dt_e).astype(BF16)
    cb = lax.dot_general(cm, bm, (((1,), (1,)), ((), ())), preferred_element_type=F32)
    cb_tl, cb_bl, cb_br = cb[:half, :half], cb[half:, :half], cb[half:, half:]
    tri = lax.broadcasted_iota(jnp.int32, (half, half), 0) >= lax.broadcasted_iota(jnp.int32, (half, half), 1)
    first_half = lax.broadcasted_iota(jnp.int32, (lc, pair), 1) < head_dim
    zero_tr = jnp.zeros((half, half), F32)

    def masked_cb(r):
        col, row = cs[:, r:r + 1], cst[r:r + 1, :]
        m_tl = cb_tl * jnp.exp2(jnp.where(tri, col[:half] - row[:, :half], -jnp.inf))
        m_bl = cb_bl * jnp.exp2(col[half:] - row[:, :half])
        m_br = cb_br * jnp.exp2(jnp.where(tri, col[half:] - row[:, half:], -jnp.inf))
        top = jnp.concatenate([m_tl, zero_tr], axis=1)
        return jnp.concatenate([top, jnp.concatenate([m_bl, m_br], axis=1)], axis=0).astype(BF16)

    ys = []
    for p in range(heads // 2):
        xp = xdt[:, p * pair:(p + 1) * pair]
        xa = jnp.where(first_half, xp, jnp.zeros_like(xp))
        xb = jnp.where(first_half, jnp.zeros_like(xp), xp)
        ys.append(jnp.dot(masked_cb(2 * p), xa, preferred_element_type=F32)
                  + jnp.dot(masked_cb(2 * p + 1), xb, preferred_element_type=F32))
    y = jnp.concatenate(ys, axis=1) if len(ys) > 1 else ys[0]

    state = state_ref[...]
    y = y + jnp.dot(cm, state.astype(BF16), preferred_element_type=F32) * ecs_e
    state_ref[...] = state * ecs_e[lc - 1:lc, :] + lax.dot_general(
        bm, (x * dtd_e).astype(BF16), (((0,), (0,)), ((), ())), preferred_element_type=F32)

    y = y + x * dskip_ref[...]
    y = y * _silu(z_ref[...].astype(F32))
    ms = jnp.mean(y * y, axis=-1, keepdims=True)
    o_ref[...] = (y * lax.rsqrt(ms + eps) * nw_ref[...]).astype(o_ref.dtype)


def _ssd_scan(z, xbc, dt, cs, cst, d_skip, norm_w, batch, d_inner, n_groups, heads):
    t = z.shape[0]
    lc = SSD_CHUNK
    nc = t // batch // lc
    n = SSD_D_STATE
    gw = d_inner // n_groups
    head_dim = gw // heads
    assert heads % 2 == 0 and 2 * head_dim == V7X_LANES
    row = lambda b, g, c: b * nc + c
    small = lambda shape: pl.BlockSpec(shape, lambda b, g, c: (row(b, g, c), g, 0, 0))
    return pl.pallas_call(
        functools.partial(_ssd_scan_kernel, heads=heads, head_dim=head_dim, eps=INNER_NORM_EPS),
        grid=(batch, n_groups, nc),
        in_specs=[
            pl.BlockSpec((lc, gw), lambda b, g, c: (row(b, g, c), g)),
            pl.BlockSpec((lc, n), lambda b, g, c: (row(b, g, c), d_inner // n + g)),
            pl.BlockSpec((lc, n), lambda b, g, c: (row(b, g, c), d_inner // n + n_groups + g)),
            pl.BlockSpec((lc, gw), lambda b, g, c: (row(b, g, c), g)),
            small((1, 1, lc, heads)), small((1, 1, lc, heads)), small((1, 1, heads, lc)),
            pl.BlockSpec((1, gw), lambda b, g, c: (0, g)),
            pl.BlockSpec((1, gw), lambda b, g, c: (0, g)),
        ],
        out_specs=pl.BlockSpec((lc, gw), lambda b, g, c: (row(b, g, c), g)),
        out_shape=jax.ShapeDtypeStruct((t, d_inner), BF16),
        scratch_shapes=[pltpu.VMEM((n, gw), F32)],
        compiler_params=_params("parallel", "parallel", "arbitrary"),
        name="ssd_scan",
    )(xbc, xbc, xbc, z, dt, cs, cst,
      jnp.repeat(d_skip.astype(F32), head_dim).reshape(1, d_inner),
      norm_w.reshape(1, d_inner).astype(F32))


def _ssd_mixer(x, batch, norm_w, in_proj, conv_w, conv_b, dt_bias, a_log, d_skip, inner_norm_w, out_proj):
    d_inner = out_proj.shape[0]
    conv_dim = conv_w.shape[0]
    n_heads = dt_bias.shape[0]
    n_groups = (conv_dim - d_inner) // 2 // SSD_D_STATE
    main = d_inner + conv_dim
    h, dt, cs, cst = _ssd_norm_dt(x, norm_w, in_proj, main, dt_bias, a_log, n_groups)
    z = _matmul_wcast(h, in_proj, d_inner, BF16)
    xbc = _matmul_wcast_conv(h, in_proj, d_inner, conv_w, conv_b, x.shape[0] // batch)
    y = _ssd_scan(z, xbc, dt, cs, cst, d_skip, inner_norm_w, batch, d_inner, n_groups,
                  n_heads // n_groups)
    return _matmul_residual(y, out_proj.astype(BF16), x)


def _gla_norm_gate_kernel(x_ref, nw_ref, wlow_ref, up_ref, bias_ref, h_ref, o_ref, *, eps):
    h = _rms_normalize(x_ref[...], nw_ref[...], eps).astype(BF16)
    h_ref[...] = h
    rank = up_ref.shape[0]
    low = jnp.dot(h, wlow_ref[...].astype(BF16), preferred_element_type=F32)[:, :rank]
    l_hi, l_mid, l_lo = _pieces3(low)
    u_hi, u_mid, u_lo = _pieces3(up_ref[...])
    lhs = jnp.concatenate([l_hi, l_mid, l_lo, l_hi, l_mid, l_hi], axis=1).astype(BF16)
    rhs = jnp.concatenate([u_hi, u_hi, u_hi, u_mid, u_mid, u_lo], axis=0).astype(BF16)
    x = jnp.dot(lhs, rhs, preferred_element_type=F32) + bias_ref[...]
    o_ref[...] = (jnp.minimum(x, 0.0) - jnp.log1p(jnp.exp(-jnp.abs(x)))) * (1.0 / GLA_GATE_NORMALIZER)


def _gla_norm_gate(x, norm_w, w_low, gk_up, gk_bias):
    t, d = x.shape
    rank, kd = gk_up.shape
    pad = (-rank) % V7X_LANES
    w_low = jnp.pad(w_low.astype(F32), ((0, 0), (0, pad)))
    gk_up = gk_up.astype(F32)
    tr = _tile(t, 2 * ROW_TILE, 8)
    rows = pl.BlockSpec((tr, d), lambda i: (i, 0))
    return pl.pallas_call(
        functools.partial(_gla_norm_gate_kernel, eps=NORM_EPS),
        grid=(t // tr,),
        in_specs=[rows, pl.BlockSpec((1, d), lambda i: (0, 0)),
                  pl.BlockSpec((d, rank + pad), lambda i: (0, 0)),
                  pl.BlockSpec((rank, kd), lambda i: (0, 0)),
                  pl.BlockSpec((1, kd), lambda i: (0, 0))],
        out_specs=[rows, pl.BlockSpec((tr, kd), lambda i: (i, 0))],
        out_shape=[jax.ShapeDtypeStruct((t, d), BF16), jax.ShapeDtypeStruct((t, kd), F32)],
        compiler_params=_params("parallel"),
        name="gla_norm_gate",
    )(x, norm_w.reshape(1, d).astype(F32), w_low, gk_up, gk_bias.reshape(1, kd).astype(F32))


def _gla_scan_kernel(q_ref, k_ref, v_ref, g_ref, go_ref, nw_ref, o_ref, state_ref, *,
                     heads, chunk, scale, eps):
    @pl.when(pl.program_id(2) == 0)
    def _():
        state_ref[...] = jnp.zeros_like(state_ref)

    rows = q_ref.shape[0]
    nsub = rows // chunk
    hk = q_ref.shape[1] // heads
    hv = v_ref.shape[1] // heads
    shift = chunk.bit_length() - 1
    ri = lax.broadcasted_iota(jnp.int32, (rows, rows), 0)
    ci = lax.broadcasted_iota(jnp.int32, (rows, rows), 1)
    causal = ri >= ci
    in_chunk_causal = causal & (lax.shift_right_logical(ri, shift) == lax.shift_right_logical(ci, shift))
    cumsum_m = jnp.concatenate([in_chunk_causal.astype(BF16)] * 3, axis=1)
    blk = lambda a, s: a[s * chunk:(s + 1) * chunk]
    q = q_ref[...].astype(F32) * scale
    k = k_ref[...].astype(F32)
    cs = jnp.dot(cumsum_m, _split3_rows(g_ref[...]), preferred_element_type=F32)
    tot = [cs[(s + 1) * chunk - 1:(s + 1) * chunk, :] for s in range(nsub)]
    before = [jnp.zeros_like(tot[0])]
    for s in range(nsub):
        before.append(before[s] + tot[s])
    total = before[nsub]
    q_dec = q * jnp.exp(cs)
    q_dec_bf = q_dec.astype(BF16)
    k_inv = (k * jnp.exp(-cs)).astype(BF16)
    k_end = [blk(k, s) * jnp.exp(tot[s] - blk(cs, s)) for s in range(nsub)]
    q_in = jnp.concatenate([blk(q_dec, s) * jnp.exp(before[s]) for s in range(nsub)], axis=0).astype(BF16)
    k_out = jnp.concatenate([k_end[s] * jnp.exp(total - before[s + 1]) for s in range(nsub)],
                            axis=0).astype(BF16)
    keys = []
    for s in range(nsub):
        rows_s = [(k_end[j] * jnp.exp(before[s] - before[j + 1])).astype(BF16) for j in range(s)]
        rows_s.append(blk(k_inv, s))
        rows_s += [jnp.zeros((chunk, heads * hk), BF16)] * (nsub - 1 - s)
        keys.append(jnp.concatenate(rows_s, axis=0))
    total_col = jnp.exp(jnp.broadcast_to(total, (V7X_LANES, heads * hk)).T[:, 0:1])
    for hh in range(heads):
        kc = slice(hh * hk, (hh + 1) * hk)
        vc = slice(hh * hv, (hh + 1) * hv)
        v = v_ref[:, vc]
        score_rows = [lax.dot_general(blk(q_dec_bf, s)[:, kc], keys[s][:, kc], (((1,), (1,)), ((), ())),
                                      preferred_element_type=F32) for s in range(nsub)]
        scores = jnp.where(causal, jnp.concatenate(score_rows, axis=0), 0.0).astype(BF16)
        state = state_ref[hh]
        o = (jnp.dot(scores, v, preferred_element_type=F32)
             + jnp.dot(q_in[:, kc], state.astype(BF16), preferred_element_type=F32))
        state_ref[hh] = state * total_col[kc] + lax.dot_general(
            k_out[:, kc], v, (((0,), (0,)), ((), ())), preferred_element_type=F32)
        ms = jnp.mean(o * o, axis=-1, keepdims=True)
        o = o * lax.rsqrt(ms + eps) * nw_ref[...]
        o_ref[:, vc] = (o * _silu(go_ref[:, vc].astype(F32))).astype(o_ref.dtype)


def _gla_scan(qkvg, log_g, norm_w, batch, key_dim, value_dim, n_heads):
    t = qkvg.shape[0]
    hk, hv = key_dim // n_heads, value_dim // n_heads
    seqlen = t // batch
    rb = _tile(seqlen, GLA_ROW_BLOCK, GLA_CHUNK)
    nb = seqlen // rb
    hpb = GLA_HEADS_PER_STEP if n_heads % GLA_HEADS_PER_STEP == 0 else 1
    kw, vw = hpb * hk, hpb * hv
    assert rb % GLA_CHUNK == 0 and (2 * key_dim) % vw == 0 and value_dim % vw == 0
    row = lambda b, h, i: b * nb + i
    k0 = key_dim // kw
    v0 = 2 * key_dim // vw
    g0 = (2 * key_dim + value_dim) // vw
    return pl.pallas_call(
        functools.partial(_gla_scan_kernel, heads=hpb, chunk=GLA_CHUNK, scale=hk ** -0.5,
                          eps=INNER_NORM_EPS),
        grid=(batch, n_heads // hpb, nb),
        in_specs=[
            pl.BlockSpec((rb, kw), lambda b, h, i: (row(b, h, i), h)),
            pl.BlockSpec((rb, kw), lambda b, h, i: (row(b, h, i), k0 + h)),
            pl.BlockSpec((rb, vw), lambda b, h, i: (row(b, h, i), v0 + h)),
            pl.BlockSpec((rb, kw), lambda b, h, i: (row(b, h, i), h)),
            pl.BlockSpec((rb, vw), lambda b, h, i: (row(b, h, i), g0 + h)),
            pl.BlockSpec((1, hv), lambda b, h, i: (0, 0)),
        ],
        out_specs=pl.BlockSpec((rb, vw), lambda b, h, i: (row(b, h, i), h)),
        out_shape=jax.ShapeDtypeStruct((t, value_dim), BF16),
        scratch_shapes=[pltpu.VMEM((hpb, hk, hv), F32)],
        compiler_params=_params("parallel", "parallel", "arbitrary"),
        name="gla_scan",
    )(qkvg, qkvg, qkvg, log_g, qkvg, norm_w.reshape(1, hv).astype(F32))


def _gla_mixer(x, batch, norm_w, in_proj, gk_up, gk_bias, inner_norm_w, out_proj):
    key_dim = gk_up.shape[1]
    value_dim = out_proj.shape[0]
    n_heads = value_dim // inner_norm_w.shape[0]
    main = 2 * key_dim + 2 * value_dim
    h, log_g = _gla_norm_gate(x, norm_w, in_proj[:, main:], gk_up, gk_bias)
    qkvg = _matmul_wcast(h, in_proj, main, BF16)
    o = _gla_scan(qkvg, log_g, inner_norm_w, batch, key_dim, value_dim, n_heads)
    return _matmul_residual(o, out_proj.astype(BF16), x)


def kernel(x, l0_mixer_norm, l0_ssd_in_proj, l0_ssd_conv_w, l0_ssd_conv_b, l0_ssd_dt_bias, l0_ssd_a_log, l0_ssd_d, l0_ssd_norm_w, l0_ssd_out_proj, l0_ffn_norm, l0_ffn_w_gate, l0_ffn_w_up, l0_ffn_w_down, l1_mixer_norm, l1_gla_in_proj, l1_gla_gk_up, l1_gla_gk_bias, l1_gla_norm_w, l1_gla_out_proj, l1_ffn_norm, l1_ffn_w_gate, l1_ffn_w_up, l1_ffn_w_down, final_norm):
    batch, seqlen, d = x.shape
    xf = x.reshape(batch * seqlen, d).astype(F32)
    xf = _ssd_mixer(xf, batch, l0_mixer_norm, l0_ssd_in_proj, l0_ssd_conv_w, l0_ssd_conv_b,
                    l0_ssd_dt_bias, l0_ssd_a_log, l0_ssd_d, l0_ssd_norm_w, l0_ssd_out_proj)
    xf = _ffn(xf, l0_ffn_norm, l0_ffn_w_gate, l0_ffn_w_up, l0_ffn_w_down)
    xf = _gla_mixer(xf, batch, l1_mixer_norm, l1_gla_in_proj, l1_gla_gk_up, l1_gla_gk_bias,
                    l1_gla_norm_w, l1_gla_out_proj)
    xf = _ffn(xf, l1_ffn_norm, l1_ffn_w_gate, l1_ffn_w_up, l1_ffn_w_down)
    return _rmsnorm(xf, final_norm, NORM_EPS, x.dtype).reshape(batch, seqlen, d)
```

```python
import functools

import jax
import jax.numpy as jnp
from jax import lax
from jax.experimental import pallas as pl
from jax.experimental.pallas import tpu as pltpu

F32 = jnp.float32
BF16 = jnp.bfloat16

NORM_EPS = 1e-6
INNER_NORM_EPS = 1e-5
SSD_D_STATE = 128
SSD_CHUNK = 256
GLA_CHUNK = 64
GLA_GATE_NORMALIZER = 16.0
LOG2_E = 1.4426950408889634

V7X_LANES = 128
V7X_F32_SUBLANES = 8
V7X_BF16_SUBLANES = 16
V7X_VMEM_BYTES = 64 * 1024 * 1024
VMEM_LIMIT_BYTES = V7X_VMEM_BYTES - 8 * 1024 * 1024

MM_RES_VMEM_BUDGET = 46 * 1024 * 1024
RES_ROW_TILES = (2048, 1024, 512)
RES_COL_TILES = (1024, 512, 256)
WCAST_TILE_M = 2048
WCAST_TILE_N = 512
GATEUP_TILE_N = 256
ROW_TILE = 256
NORM_ROW_TILE = 512
GLA_ROW_BLOCK = 256
GLA_HEADS_PER_STEP = 8


def _params(*semantics):
    return pltpu.CompilerParams(dimension_semantics=semantics, vmem_limit_bytes=VMEM_LIMIT_BYTES)


def _tile(dim, pref, align=V7X_LANES):
    if dim <= pref:
        return dim
    t = (pref // align) * align
    while t >= align:
        if dim % t == 0:
            return t
        t -= align
    return dim


def _silu(x):
    h = 0.5 * x
    return h + h * jnp.tanh(h)


def _softplus(x):
    return jnp.maximum(x, 0.0) + jnp.log1p(jnp.exp(-jnp.abs(x)))


def _pieces3(a):
    hi = a.astype(BF16).astype(F32)
    mid = (a - hi).astype(BF16).astype(F32)
    return hi, mid, a - hi - mid


def _split3_cols(a):
    return jnp.concatenate(_pieces3(a), axis=1).astype(BF16)


def _split3_rows(a):
    return jnp.concatenate(_pieces3(a), axis=0).astype(BF16)


def _rms_normalize(x, w, eps):
    ms = jnp.mean(x * x, axis=-1, keepdims=True)
    return x * lax.rsqrt(ms + eps) * w


def _rmsnorm_kernel(x_ref, w_ref, o_ref, *, eps):
    o_ref[...] = _rms_normalize(x_ref[...], w_ref[...], eps).astype(o_ref.dtype)


def _rmsnorm(x, w, eps, out_dtype):
    t, d = x.shape
    tr = _tile(t, NORM_ROW_TILE, V7X_F32_SUBLANES)
    return pl.pallas_call(
        functools.partial(_rmsnorm_kernel, eps=eps),
        grid=(t // tr,),
        in_specs=[pl.BlockSpec((tr, d), lambda i: (i, 0)), pl.BlockSpec((1, d), lambda i: (0, 0))],
        out_specs=pl.BlockSpec((tr, d), lambda i: (i, 0)),
        out_shape=jax.ShapeDtypeStruct((t, d), out_dtype),
        compiler_params=_params("parallel"),
        name="rmsnorm",
    )(x, w.reshape(1, d).astype(F32))


def _mm_wcast_kernel(a_ref, w_ref, o_ref):
    w = w_ref[...].astype(BF16)
    o_ref[...] = jnp.dot(a_ref[...], w, preferred_element_type=F32).astype(o_ref.dtype)


def _resident_rows(tm, k):
    return pl.BlockSpec((tm, k), lambda i, j: (i, 0), pipeline_mode=pl.Buffered(1))


def _matmul_wcast(a, w, n_cols, out_dtype):
    m, k = a.shape
    tm, tn = _tile(m, WCAST_TILE_M), _tile(n_cols, WCAST_TILE_N)
    return pl.pallas_call(
        _mm_wcast_kernel,
        grid=(m // tm, n_cols // tn),
        in_specs=[_resident_rows(tm, k), pl.BlockSpec((k, tn), lambda i, j: (0, j))],
        out_specs=pl.BlockSpec((tm, tn), lambda i, j: (i, j)),
        out_shape=jax.ShapeDtypeStruct((m, n_cols), out_dtype),
        compiler_params=_params("parallel", "arbitrary"),
        name="matmul_wcast",
    )(a, w)


def _mm_wcast_conv_kernel(a_ref, w_ref, cw_ref, cb_ref, o_ref, buf_ref, carry_ref, *, halo, tiles_per_seq):
    i, j = pl.program_id(0), pl.program_id(1)
    tm = a_ref.shape[0]
    kw = cw_ref.shape[0]

    @pl.when((i == 0) & (j == 0))
    def _():
        carry_ref[...] = jnp.zeros_like(carry_ref)

    acc = jnp.dot(a_ref[...], w_ref[...].astype(BF16), preferred_element_type=F32)
    buf_ref[0:halo, :] = jnp.where(lax.rem(i, tiles_per_seq) == 0, 0.0, carry_ref[j])
    buf_ref[halo:halo + tm, :] = acc
    carry_ref[j] = acc[tm - halo:, :]
    out = cb_ref[...] + acc * cw_ref[kw - 1:kw, :]
    for tap in range(kw - 1):
        off = halo - (kw - 1) + tap
        out = out + buf_ref[off:off + tm, :] * cw_ref[tap:tap + 1, :]
    o_ref[...] = _silu(out).astype(o_ref.dtype)


def _matmul_wcast_conv(a, w, col0, conv_w, conv_b, seqlen):
    m, k = a.shape
    c, kw = conv_w.shape
    halo = V7X_BF16_SUBLANES
    tm, tn = _tile(seqlen, WCAST_TILE_M, halo), _tile(c, WCAST_TILE_N)
    assert col0 % tn == 0 and seqlen % tm == 0 and kw - 1 <= halo
    return pl.pallas_call(
        functools.partial(_mm_wcast_conv_kernel, halo=halo, tiles_per_seq=seqlen // tm),
        grid=(m // tm, c // tn),
        in_specs=[_resident_rows(tm, k),
                  pl.BlockSpec((k, tn), lambda i, j: (0, col0 // tn + j)),
                  pl.BlockSpec((kw, tn), lambda i, j: (0, j)),
                  pl.BlockSpec((1, tn), lambda i, j: (0, j))],
        out_specs=pl.BlockSpec((tm, tn), lambda i, j: (i, j)),
        out_shape=jax.ShapeDtypeStruct((m, c), BF16),
        scratch_shapes=[pltpu.VMEM((tm + halo, tn), F32), pltpu.VMEM((c // tn, halo, tn), F32)],
        compiler_params=_params("arbitrary", "arbitrary"),
        name="matmul_wcast_conv",
    )(a, w, conv_w.T.astype(F32), conv_b.reshape(1, c).astype(F32))


def _mm_res_kernel(a_ref, w_ref, r_ref, o_ref):
    o_ref[...] = r_ref[...] + jnp.dot(a_ref[...], w_ref[...], preferred_element_type=F32)


def _res_tiles(m, k, n):
    best = None
    for tm in RES_ROW_TILES:
        for tn in RES_COL_TILES:
            if m % tm or n % tn:
                continue
            need = k * tn * 2 + 2 * tm * k * 2 + 5 * tm * tn * 4
            if need <= MM_RES_VMEM_BUDGET and (best is None or tm * tn > best[0] * best[1]):
                best = (tm, tn)
    return best if best is not None else (_tile(m, ROW_TILE, V7X_F32_SUBLANES), _tile(n, V7X_LANES))


def _matmul_residual(a, w, res):
    m, k = a.shape
    n = w.shape[1]
    tm, tn = _res_tiles(m, k, n)
    return pl.pallas_call(
        _mm_res_kernel,
        grid=(n // tn, m // tm),
        in_specs=[
            pl.BlockSpec((tm, k), lambda j, i: (i, 0)),
            pl.BlockSpec((k, tn), lambda j, i: (0, j), pipeline_mode=pl.Buffered(1)),
            pl.BlockSpec((tm, tn), lambda j, i: (i, j)),
        ],
        out_specs=pl.BlockSpec((tm, tn), lambda j, i: (i, j)),
        out_shape=jax.ShapeDtypeStruct((m, n), F32),
        compiler_params=_params("parallel", "arbitrary"),
        name="matmul_residual",
    )(a, w, res)


def _gateup_kernel(h_ref, wg_ref, wu_ref, o_ref):
    h = h_ref[...]
    g = jnp.dot(h, wg_ref[...].astype(BF16), preferred_element_type=F32)
    u = jnp.dot(h, wu_ref[...].astype(BF16), preferred_element_type=F32)
    o_ref[...] = (_silu(g) * u).astype(o_ref.dtype)


def _ffn_gateup(h, wg, wu):
    m, k = h.shape
    n = wg.shape[1]
    tm, tn = _tile(m, WCAST_TILE_M), _tile(n, GATEUP_TILE_N)
    wspec = pl.BlockSpec((k, tn), lambda i, j: (0, j))
    return pl.pallas_call(
        _gateup_kernel,
        grid=(m // tm, n // tn),
        in_specs=[_resident_rows(tm, k), wspec, wspec],
        out_specs=pl.BlockSpec((tm, tn), lambda i, j: (i, j)),
        out_shape=jax.ShapeDtypeStruct((m, n), BF16),
        compiler_params=_params("parallel", "arbitrary"),
        name="ffn_gateup",
    )(h, wg, wu)


def _ffn(x, norm_w, w_gate, w_up, w_down):
    h = _rmsnorm(x, norm_w, NORM_EPS, BF16)
    return _matmul_residual(_ffn_gateup(h, w_gate, w_up), w_down.astype(BF16), x)


def _ssd_norm_dt_kernel(x_ref, nw_ref, w_ref, bias_ref, alog_ref, h_ref, dt_ref, cs_ref, cst_ref, *,
                        n_groups, eps):
    h = _rms_normalize(x_ref[...], nw_ref[...], eps).astype(BF16)
    h_ref[...] = h
    lc = h.shape[0]
    nh = bias_ref.shape[1]
    r = nh // n_groups
    raw = jnp.dot(h, w_ref[...].astype(BF16), preferred_element_type=F32)[:, :nh]
    dt = _softplus(raw + bias_ref[...])
    la = dt * (-LOG2_E * jnp.exp(alog_ref[...]))
    row = lax.broadcasted_iota(jnp.int32, (lc, lc), 0)
    col = lax.broadcasted_iota(jnp.int32, (lc, lc), 1)
    tril3 = jnp.concatenate([(row >= col).astype(BF16)] * 3, axis=1)
    cs = jnp.dot(tril3, _split3_rows(la), preferred_element_type=F32)
    cst = cs.T
    for g in range(n_groups):
        dt_ref[0, g] = dt[:, g * r:(g + 1) * r]
        cs_ref[0, g] = cs[:, g * r:(g + 1) * r]
        cst_ref[0, g] = cst[g * r:(g + 1) * r, :]


def _ssd_norm_dt(x, norm_w, in_proj, col0, dt_bias, a_log, n_groups):
    t, d = x.shape
    nh = dt_bias.shape[0]
    r = nh // n_groups
    lc = SSD_CHUNK
    nblk = t // lc
    assert col0 % V7X_LANES == 0 and nh <= V7X_LANES
    rows = pl.BlockSpec((lc, d), lambda i: (i, 0))
    vec = pl.BlockSpec((1, nh), lambda i: (0, 0))
    tok = pl.BlockSpec((1, n_groups, lc, r), lambda i: (i, 0, 0, 0))
    head = pl.BlockSpec((1, n_groups, r, lc), lambda i: (i, 0, 0, 0))
    return pl.pallas_call(
        functools.partial(_ssd_norm_dt_kernel, n_groups=n_groups, eps=NORM_EPS),
        grid=(nblk,),
        in_specs=[rows, pl.BlockSpec((1, d), lambda i: (0, 0)),
                  pl.BlockSpec((d, V7X_LANES), lambda i: (0, col0 // V7X_LANES)), vec, vec],
        out_specs=[rows, tok, tok, head],
        out_shape=[jax.ShapeDtypeStruct((t, d), BF16),
                   jax.ShapeDtypeStruct((nblk, n_groups, lc, r), F32),
                   jax.ShapeDtypeStruct((nblk, n_groups, lc, r), F32),
                   jax.ShapeDtypeStruct((nblk, n_groups, r, lc), F32)],
        compiler_params=_params("parallel"),
        name="ssd_norm_dt",
    )(x, norm_w.reshape(1, d).astype(F32), in_proj,
      dt_bias.reshape(1, nh).astype(F32), a_log.reshape(1, nh).astype(F32))


def _ssd_scan_kernel(x_ref, b_ref, c_ref, z_ref, dt_ref, cs_ref, cst_ref, dskip_ref, nw_ref,
                     o_ref, state_ref, *, heads, head_dim, eps):
    @pl.when(pl.program_id(2) == 0)
    def _():
        state_ref[...] = jnp.zeros_like(state_ref)

    lc = x_ref.shape[0]
    half = lc // 2
    gw = heads * head_dim
    pair = 2 * head_dim
    x = x_ref[...].astype(F32)
    bm = b_ref[...]
    cm = c_ref[...]
    dt = dt_ref[0, 0]
    cs = cs_ref[0, 0]
    cst = cst_ref[0, 0]

    e_row = lax.broadcasted_iota(jnp.int32, (3 * heads, gw), 0) % heads
    e_col = lax.broadcasted_iota(jnp.int32, (3 * heads, gw), 1) // head_dim
    expand_m = (e_row == e_col).astype(BF16)
    expand = lambda a: jnp.dot(_split3_cols(a), expand_m, preferred_element_type=F32)

    cs_end = cs[lc - 1:lc, :]
    dt_e = expand(dt)
    ecs_e = expand(jnp.exp2(cs))
    dtd_e = expand(dt * jnp.exp2(cs_end - cs))

    xdt = (x * dt_e).astype(BF16)
    cb = lax.dot_general(cm, bm, (((1,), (1,)), ((), ())), preferred_element_type=F32)
    cb_tl, cb_bl, cb_br = cb[:half, :half], cb[half:, :half], cb[half:, half:]
    tri = lax.broadcasted_iota(jnp.int32, (half, half), 0) >= lax.broadcasted_iota(jnp.int32, (half, half), 1)
    first_half = lax.broadcasted_iota(jnp.int32, (lc, pair), 1) < head_dim
    zero_tr = jnp.zeros((half, half), F32)

    def masked_cb(r):
        col, row = cs[:, r:r + 1], cst[r:r + 1, :]
        m_tl = cb_tl * jnp.exp2(jnp.where(tri, col[:half] - row[:, :half], -jnp.inf))
        m_bl = cb_bl * jnp.exp2(col[half:] - row[:, :half])
        m_br = cb_br * jnp.exp2(jnp.where(tri, col[half:] - row[:, half:], -jnp.inf))
        top = jnp.concatenate([m_tl, zero_tr], axis=1)
        return jnp.concatenate([top, jnp.concatenate([m_bl, m_br], axis=1)], axis=0).astype(BF16)

    ys = []
    for p in range(heads // 2):
        xp = xdt[:, p * pair:(p + 1) * pair]
        xa = jnp.where(first_half, xp, jnp.zeros_like(xp))
        xb = jnp.where(first_half, jnp.zeros_like(xp), xp)
        ys.append(jnp.dot(masked_cb(2 * p), xa, preferred_element_type=F32)
                  + jnp.dot(masked_cb(2 * p + 1), xb, preferred_element_type=F32))
    y = jnp.concatenate(ys, axis=1) if len(ys) > 1 else ys[0]

    state = state_ref[...]
    y = y + jnp.dot(cm, state.astype(BF16), preferred_element_type=F32) * ecs_e
    state_ref[...] = state * ecs_e[lc - 1:lc, :] + lax.dot_general(
        bm, (x * dtd_e).astype(BF16), (((0,), (0,)), ((), ())), preferred_element_type=F32)

    y = y + x * dskip_ref[...]
    y = y * _silu(z_ref[...].astype(F32))
    ms = jnp.mean(y * y, axis=-1, keepdims=True)
    o_ref[...] = (y * lax.rsqrt(ms + eps) * nw_ref[...]).astype(o_ref.dtype)


def _ssd_scan(z, xbc, dt, cs, cst, d_skip, norm_w, batch, d_inner, n_groups, heads):
    t = z.shape[0]
    lc = SSD_CHUNK
    nc = t // batch // lc
    n = SSD_D_STATE
    gw = d_inner // n_groups
    head_dim = gw // heads
    assert heads % 2 == 0 and 2 * head_dim == V7X_LANES
    row = lambda b, g, c: b * nc + c
    small = lambda shape: pl.BlockSpec(shape, lambda b, g, c: (row(b, g, c), g, 0, 0))
    return pl.pallas_call(
        functools.partial(_ssd_scan_kernel, heads=heads, head_dim=head_dim, eps=INNER_NORM_EPS),
        grid=(batch, n_groups, nc),
        in_specs=[
            pl.BlockSpec((lc, gw), lambda b, g, c: (row(b, g, c), g)),
            pl.BlockSpec((lc, n), lambda b, g, c: (row(b, g, c), d_inner // n + g)),
            pl.BlockSpec((lc, n), lambda b, g, c: (row(b, g, c), d_inner // n + n_groups + g)),
            pl.BlockSpec((lc, gw), lambda b, g, c: (row(b, g, c), g)),
            small((1, 1, lc, heads)), small((1, 1, lc, heads)), small((1, 1, heads, lc)),
            pl.BlockSpec((1, gw), lambda b, g, c: (0, g)),
            pl.BlockSpec((1, gw), lambda b, g, c: (0, g)),
        ],
        out_specs=pl.BlockSpec((lc, gw), lambda b, g, c: (row(b, g, c), g)),
        out_shape=jax.ShapeDtypeStruct((t, d_inner), BF16),
        scratch_shapes=[pltpu.VMEM((n, gw), F32)],
        compiler_params=_params("parallel", "parallel", "arbitrary"),
        name="ssd_scan",
    )(xbc, xbc, xbc, z, dt, cs, cst,
      jnp.repeat(d_skip.astype(F32), head_dim).reshape(1, d_inner),
      norm_w.reshape(1, d_inner).astype(F32))


def _ssd_mixer(x, batch, norm_w, in_proj, conv_w, conv_b, dt_bias, a_log, d_skip, inner_norm_w, out_proj):
    d_inner = out_proj.shape[0]
    conv_dim = conv_w.shape[0]
    n_heads = dt_bias.shape[0]
    n_groups = (conv_dim - d_inner) // 2 // SSD_D_STATE
    main = d_inner + conv_dim
    h, dt, cs, cst = _ssd_norm_dt(x, norm_w, in_proj, main, dt_bias, a_log, n_groups)
    z = _matmul_wcast(h, in_proj, d_inner, BF16)
    xbc = _matmul_wcast_conv(h, in_proj, d_inner, conv_w, conv_b, x.shape[0] // batch)
    y = _ssd_scan(z, xbc, dt, cs, cst, d_skip, inner_norm_w, batch, d_inner, n_groups,
                  n_heads // n_groups)
    return _matmul_residual(y, out_proj.astype(BF16), x)


def _gla_norm_gate_kernel(x_ref, nw_ref, wlow_ref, up_ref, bias_ref, h_ref, o_ref, *, eps):
    h = _rms_normalize(x_ref[...], nw_ref[...], eps).astype(BF16)
    h_ref[...] = h
    rank = up_ref.shape[0]
    low = jnp.dot(h, wlow_ref[...].astype(BF16), preferred_element_type=F32)[:, :rank]
    l_hi, l_mid, l_lo = _pieces3(low)
    u_hi, u_mid, u_lo = _pieces3(up_ref[...])
    lhs = jnp.concatenate([l_hi, l_mid, l_lo, l_hi, l_mid, l_hi], axis=1).astype(BF16)
    rhs = jnp.concatenate([u_hi, u_hi, u_hi, u_mid, u_mid, u_lo], axis=0).astype(BF16)
    x = jnp.dot(lhs, rhs, preferred_element_type=F32) + bias_ref[...]
    o_ref[...] = (jnp.minimum(x, 0.0) - jnp.log1p(jnp.exp(-jnp.abs(x)))) * (1.0 / GLA_GATE_NORMALIZER)


def _gla_norm_gate(x, norm_w, w_low, gk_up, gk_bias):
    t, d = x.shape
    rank, kd = gk_up.shape
    pad = (-rank) % V7X_LANES
    w_low = jnp.pad(w_low.astype(F32), ((0, 0), (0, pad)))
    gk_up = gk_up.astype(F32)
    tr = _tile(t, NORM_ROW_TILE, V7X_F32_SUBLANES)
    rows = pl.BlockSpec((tr, d), lambda i: (i, 0))
    return pl.pallas_call(
        functools.partial(_gla_norm_gate_kernel, eps=NORM_EPS),
        grid=(t // tr,),
        in_specs=[rows, pl.BlockSpec((1, d), lambda i: (0, 0)),
                  pl.BlockSpec((d, rank + pad), lambda i: (0, 0)),
                  pl.BlockSpec((rank, kd), lambda i: (0, 0)),
                  pl.BlockSpec((1, kd), lambda i: (0, 0))],
        out_specs=[rows, pl.BlockSpec((tr, kd), lambda i: (i, 0))],
        out_shape=[jax.ShapeDtypeStruct((t, d), BF16), jax.ShapeDtypeStruct((t, kd), F32)],
        compiler_params=_params("parallel"),
        name="gla_norm_gate",
    )(x, norm_w.reshape(1, d).astype(F32), w_low, gk_up, gk_bias.reshape(1, kd).astype(F32))


def _gla_scan_kernel(q_ref, k_ref, v_ref, g_ref, go_ref, nw_ref, o_ref, state_ref, *,
                     heads, chunk, scale, eps):
    @pl.when(pl.program_id(2) == 0)
    def _():
        state_ref[...] = jnp.zeros_like(state_ref)

    rows = q_ref.shape[0]
    nsub = rows // chunk
    hk = q_ref.shape[1] // heads
    hv = v_ref.shape[1] // heads
    shift = chunk.bit_length() - 1
    ri = lax.broadcasted_iota(jnp.int32, (rows, rows), 0)
    ci = lax.broadcasted_iota(jnp.int32, (rows, rows), 1)
    causal = ri >= ci
    in_chunk_causal = causal & (lax.shift_right_logical(ri, shift) == lax.shift_right_logical(ci, shift))
    cumsum_m = jnp.concatenate([in_chunk_causal.astype(BF16)] * 3, axis=1)
    blk = lambda a, s: a[s * chunk:(s + 1) * chunk]
    q = q_ref[...].astype(F32) * scale
    k = k_ref[...].astype(F32)
    cs = jnp.dot(cumsum_m, _split3_rows(g_ref[...]), preferred_element_type=F32)
    tot = [cs[(s + 1) * chunk - 1:(s + 1) * chunk, :] for s in range(nsub)]
    before = [jnp.zeros_like(tot[0])]
    for s in range(nsub):
        before.append(before[s] + tot[s])
    total = before[nsub]
    q_dec = q * jnp.exp(cs)
    q_dec_bf = q_dec.astype(BF16)
    k_inv = (k * jnp.exp(-cs)).astype(BF16)
    k_end = [blk(k, s) * jnp.exp(tot[s] - blk(cs, s)) for s in range(nsub)]
    q_in = jnp.concatenate([blk(q_dec, s) * jnp.exp(before[s]) for s in range(nsub)], axis=0).astype(BF16)
    k_out = jnp.concatenate([k_end[s] * jnp.exp(total - before[s + 1]) for s in range(nsub)],
                            axis=0).astype(BF16)
    keys = []
    for s in range(nsub):
        rows_s = [(k_end[j] * jnp.exp(before[s] - before[j + 1])).astype(BF16) for j in range(s)]
        rows_s.append(blk(k_inv, s))
        rows_s += [jnp.zeros((chunk, heads * hk), BF16)] * (nsub - 1 - s)
        keys.append(jnp.concatenate(rows_s, axis=0))
    total_col = jnp.exp(jnp.broadcast_to(total, (V7X_LANES, heads * hk)).T[:, 0:1])
    for hh in range(heads):
        kc = slice(hh * hk, (hh + 1) * hk)
        vc = slice(hh * hv, (hh + 1) * hv)
        v = v_ref[:, vc]
        score_rows = [lax.dot_general(blk(q_dec_bf, s)[:, kc], keys[s][:, kc], (((1,), (1,)), ((), ())),
                                      preferred_element_type=F32) for s in range(nsub)]
        scores = jnp.where(causal, jnp.concatenate(score_rows, axis=0), 0.0).astype(BF16)
        state = state_ref[hh]
        o = (jnp.dot(scores, v, preferred_element_type=F32)
             + jnp.dot(q_in[:, kc], state.astype(BF16), preferred_element_type=F32))
        state_ref[hh] = state * total_col[kc] + lax.dot_general(
            k_out[:, kc], v, (((0,), (0,)), ((), ())), preferred_element_type=F32)
        ms = jnp.mean(o * o, axis=-1, keepdims=True)
        o = o * lax.rsqrt(ms + eps) * nw_ref[...]
        o_ref[:, vc] = (o * _silu(go_ref[:, vc].astype(F32))).astype(o_ref.dtype)


def _gla_scan(qkvg, log_g, norm_w, batch, key_dim, value_dim, n_heads):
    t = qkvg.shape[0]
    hk, hv = key_dim // n_heads, value_dim // n_heads
    seqlen = t // batch
    rb = _tile(seqlen, GLA_ROW_BLOCK, GLA_CHUNK)
    nb = seqlen // rb
    hpb = GLA_HEADS_PER_STEP if n_heads % GLA_HEADS_PER_STEP == 0 else 1
    kw, vw = hpb * hk, hpb * hv
    assert rb % GLA_CHUNK == 0 and (2 * key_dim) % vw == 0 and value_dim % vw == 0
    row = lambda b, h, i: b * nb + i
    k0 = key_dim // kw
    v0 = 2 * key_dim // vw
    g0 = (2 * key_dim + value_dim) // vw
    return pl.pallas_call(
        functools.partial(_gla_scan_kernel, heads=hpb, chunk=GLA_CHUNK, scale=hk ** -0.5,
                          eps=INNER_NORM_EPS),
        grid=(batch, n_heads // hpb, nb),
        in_specs=[
            pl.BlockSpec((rb, kw), lambda b, h, i: (row(b, h, i), h)),
            pl.BlockSpec((rb, kw), lambda b, h, i: (row(b, h, i), k0 + h)),
            pl.BlockSpec((rb, vw), lambda b, h, i: (row(b, h, i), v0 + h)),
            pl.BlockSpec((rb, kw), lambda b, h, i: (row(b, h, i), h)),
            pl.BlockSpec((rb, vw), lambda b, h, i: (row(b, h, i), g0 + h)),
            pl.BlockSpec((1, hv), lambda b, h, i: (0, 0)),
        ],
        out_specs=pl.BlockSpec((rb, vw), lambda b, h, i: (row(b, h, i), h)),
        out_shape=jax.ShapeDtypeStruct((t, value_dim), BF16),
        scratch_shapes=[pltpu.VMEM((hpb, hk, hv), F32)],
        compiler_params=_params("parallel", "parallel", "arbitrary"),
        name="gla_scan",
    )(qkvg, qkvg, qkvg, log_g, qkvg, norm_w.reshape(1, hv).astype(F32))


def _gla_mixer(x, batch, norm_w, in_proj, gk_up, gk_bias, inner_norm_w, out_proj):
    key_dim = gk_up.shape[1]
    value_dim = out_proj.shape[0]
    n_heads = value_dim // inner_norm_w.shape[0]
    main = 2 * key_dim + 2 * value_dim
    h, log_g = _gla_norm_gate(x, norm_w, in_proj[:, main:], gk_up, gk_bias)
    qkvg = _matmul_wcast(h, in_proj, main, BF16)
    o = _gla_scan(qkvg, log_g, inner_norm_w, batch, key_dim, value_dim, n_heads)
    return _matmul_residual(o, out_proj.astype(BF16), x)


def kernel(x, l0_mixer_norm, l0_ssd_in_proj, l0_ssd_conv_w, l0_ssd_conv_b, l0_ssd_dt_bias, l0_ssd_a_log, l0_ssd_d, l0_ssd_norm_w, l0_ssd_out_proj, l0_ffn_norm, l0_ffn_w_gate, l0_ffn_w_up, l0_ffn_w_down, l1_mixer_norm, l1_gla_in_proj, l1_gla_gk_up, l1_gla_gk_bias, l1_gla_norm_w, l1_gla_out_proj, l1_ffn_norm, l1_ffn_w_gate, l1_ffn_w_up, l1_ffn_w_down, final_norm):
    batch, seqlen, d = x.shape
    xf = x.reshape(batch * seqlen, d).astype(F32)
    xf = _ssd_mixer(xf, batch, l0_mixer_norm, l0_ssd_in_proj, l0_ssd_conv_w, l0_ssd_conv_b,
                    l0_ssd_dt_bias, l0_ssd_a_log, l0_ssd_d, l0_ssd_norm_w, l0_ssd_out_proj)
    xf = _ffn(xf, l0_ffn_norm, l0_ffn_w_gate, l0_ffn_w_up, l0_ffn_w_down)
    xf = _gla_mixer(xf, batch, l1_mixer_norm, l1_gla_in_proj, l1_gla_gk_up, l1_gla_gk_bias,
                    l1_gla_norm_w, l1_gla_out_proj)
    xf = _ffn(xf, l1_ffn_norm, l1_ffn_w_gate, l1_ffn_w_up, l1_ffn_w_down)
    return _rmsnorm(xf, final_norm, NORM_EPS, x.dtype).reshape(batch, seqlen, d)
```

```python
import functools

import jax
import jax.numpy as jnp
from jax import lax
from jax.experimental import pallas as pl
from jax.experimental.pallas import tpu as pltpu

F32 = jnp.float32
BF16 = jnp.bfloat16

NORM_EPS = 1e-6
INNER_NORM_EPS = 1e-5
SSD_D_STATE = 128
SSD_CHUNK = 256
GLA_CHUNK = 64
GLA_GATE_NORMALIZER = 16.0
LOG2_E = 1.4426950408889634

V7X_LANES = 128
V7X_F32_SUBLANES = 8
V7X_BF16_SUBLANES = 16
V7X_VMEM_BYTES = 64 * 1024 * 1024
VMEM_LIMIT_BYTES = V7X_VMEM_BYTES - 8 * 1024 * 1024
VMEM_LIMIT_WIDE_BYTES = V7X_VMEM_BYTES - 1 * 1024 * 1024

MM_RES_VMEM_BUDGET = 46 * 1024 * 1024
RES_ROW_TILES = (2048, 1024, 512)
RES_COL_TILES = (1024, 512, 256)
WCAST_TILE_M = 2048
WCAST_TILE_N = 512
GATEUP_TILE_N = 256
ROW_TILE = 256
NORM_ROW_TILE = 512
GLA_ROW_BLOCK = 256
GLA_HEADS_PER_STEP = 8


def _params(*semantics, vmem_limit_bytes=VMEM_LIMIT_BYTES):
    return pltpu.CompilerParams(dimension_semantics=semantics, vmem_limit_bytes=vmem_limit_bytes)


def _tile(dim, pref, align=V7X_LANES):
    if dim <= pref:
        return dim
    t = (pref // align) * align
    while t >= align:
        if dim % t == 0:
            return t
        t -= align
    return dim


def _silu(x):
    h = 0.5 * x
    return h + h * jnp.tanh(h)


def _softplus(x):
    return jnp.maximum(x, 0.0) + jnp.log1p(jnp.exp(-jnp.abs(x)))


def _pieces3(a):
    hi = a.astype(BF16).astype(F32)
    mid = (a - hi).astype(BF16).astype(F32)
    return hi, mid, a - hi - mid


def _split3_cols(a):
    return jnp.concatenate(_pieces3(a), axis=1).astype(BF16)


def _split3_rows(a):
    return jnp.concatenate(_pieces3(a), axis=0).astype(BF16)


def _rms_normalize(x, w, eps):
    ms = jnp.mean(x * x, axis=-1, keepdims=True)
    return x * lax.rsqrt(ms + eps) * w


def _rmsnorm_kernel(x_ref, w_ref, o_ref, *, eps):
    o_ref[...] = _rms_normalize(x_ref[...], w_ref[...], eps).astype(o_ref.dtype)


def _rmsnorm(x, w, eps, out_dtype):
    t, d = x.shape
    tr = _tile(t, NORM_ROW_TILE, V7X_F32_SUBLANES)
    return pl.pallas_call(
        functools.partial(_rmsnorm_kernel, eps=eps),
        grid=(t // tr,),
        in_specs=[pl.BlockSpec((tr, d), lambda i: (i, 0)), pl.BlockSpec((1, d), lambda i: (0, 0))],
        out_specs=pl.BlockSpec((tr, d), lambda i: (i, 0)),
        out_shape=jax.ShapeDtypeStruct((t, d), out_dtype),
        compiler_params=_params("parallel"),
        name="rmsnorm",
    )(x, w.reshape(1, d).astype(F32))


def _mm_wcast_kernel(a_ref, w_ref, o_ref):
    w = w_ref[...].astype(BF16)
    o_ref[...] = jnp.dot(a_ref[...], w, preferred_element_type=F32).astype(o_ref.dtype)


def _resident_rows(tm, k):
    return pl.BlockSpec((tm, k), lambda i, j: (i, 0), pipeline_mode=pl.Buffered(1))


def _matmul_wcast(a, w, n_cols, out_dtype):
    m, k = a.shape
    tm, tn = _tile(m, WCAST_TILE_M), _tile(n_cols, WCAST_TILE_N)
    return pl.pallas_call(
        _mm_wcast_kernel,
        grid=(m // tm, n_cols // tn),
        in_specs=[pl.BlockSpec((tm, k), lambda i, j: (i, 0)), pl.BlockSpec((k, tn), lambda i, j: (0, j))],
        out_specs=pl.BlockSpec((tm, tn), lambda i, j: (i, j)),
        out_shape=jax.ShapeDtypeStruct((m, n_cols), out_dtype),
        compiler_params=_params("parallel", "arbitrary", vmem_limit_bytes=VMEM_LIMIT_WIDE_BYTES),
        name="matmul_wcast",
    )(a, w)


def _mm_wcast_conv_kernel(a_ref, w_ref, cw_ref, cb_ref, o_ref, buf_ref, carry_ref, *, halo, tiles_per_seq):
    i, j = pl.program_id(0), pl.program_id(1)
    tm = a_ref.shape[0]
    kw = cw_ref.shape[0]

    @pl.when((i == 0) & (j == 0))
    def _():
        carry_ref[...] = jnp.zeros_like(carry_ref)

    acc = jnp.dot(a_ref[...], w_ref[...].astype(BF16), preferred_element_type=F32)
    buf_ref[0:halo, :] = jnp.where(lax.rem(i, tiles_per_seq) == 0, 0.0, carry_ref[j])
    buf_ref[halo:halo + tm, :] = acc
    carry_ref[j] = acc[tm - halo:, :]
    out = cb_ref[...] + acc * cw_ref[kw - 1:kw, :]
    for tap in range(kw - 1):
        off = halo - (kw - 1) + tap
        out = out + buf_ref[off:off + tm, :] * cw_ref[tap:tap + 1, :]
    o_ref[...] = _silu(out).astype(o_ref.dtype)


def _matmul_wcast_conv(a, w, col0, conv_w, conv_b, seqlen):
    m, k = a.shape
    c, kw = conv_w.shape
    halo = V7X_BF16_SUBLANES
    tm, tn = _tile(seqlen, WCAST_TILE_M, halo), _tile(c, WCAST_TILE_N)
    assert col0 % tn == 0 and seqlen % tm == 0 and kw - 1 <= halo
    return pl.pallas_call(
        functools.partial(_mm_wcast_conv_kernel, halo=halo, tiles_per_seq=seqlen // tm),
        grid=(m // tm, c // tn),
        in_specs=[_resident_rows(tm, k),
                  pl.BlockSpec((k, tn), lambda i, j: (0, col0 // tn + j)),
                  pl.BlockSpec((kw, tn), lambda i, j: (0, j)),
                  pl.BlockSpec((1, tn), lambda i, j: (0, j))],
        out_specs=pl.BlockSpec((tm, tn), lambda i, j: (i, j)),
        out_shape=jax.ShapeDtypeStruct((m, c), BF16),
        scratch_shapes=[pltpu.VMEM((tm + halo, tn), F32), pltpu.VMEM((c // tn, halo, tn), F32)],
        compiler_params=_params("arbitrary", "arbitrary"),
        name="matmul_wcast_conv",
    )(a, w, conv_w.T.astype(F32), conv_b.reshape(1, c).astype(F32))


def _mm_res_kernel(a_ref, w_ref, r_ref, o_ref):
    o_ref[...] = r_ref[...] + jnp.dot(a_ref[...], w_ref[...], preferred_element_type=F32)


def _res_tiles(m, k, n):
    best = None
    for tm in RES_ROW_TILES:
        for tn in RES_COL_TILES:
            if m % tm or n % tn:
                continue
            need = k * tn * 2 + 2 * tm * k * 2 + 5 * tm * tn * 4
            if need <= MM_RES_VMEM_BUDGET and (best is None or tm * tn > best[0] * best[1]):
                best = (tm, tn)
    return best if best is not None else (_tile(m, ROW_TILE, V7X_F32_SUBLANES), _tile(n, V7X_LANES))


def _matmul_residual(a, w, res):
    m, k = a.shape
    n = w.shape[1]
    tm, tn = _res_tiles(m, k, n)
    return pl.pallas_call(
        _mm_res_kernel,
        grid=(n // tn, m // tm),
        in_specs=[
            pl.BlockSpec((tm, k), lambda j, i: (i, 0)),
            pl.BlockSpec((k, tn), lambda j, i: (0, j), pipeline_mode=pl.Buffered(1)),
            pl.BlockSpec((tm, tn), lambda j, i: (i, j)),
        ],
        out_specs=pl.BlockSpec((tm, tn), lambda j, i: (i, j)),
        out_shape=jax.ShapeDtypeStruct((m, n), F32),
        compiler_params=_params("parallel", "arbitrary"),
        name="matmul_residual",
    )(a, w, res)


def _gateup_kernel(h_ref, wg_ref, wu_ref, o_ref):
    h = h_ref[...]
    g = jnp.dot(h, wg_ref[...].astype(BF16), preferred_element_type=F32)
    u = jnp.dot(h, wu_ref[...].astype(BF16), preferred_element_type=F32)
    o_ref[...] = (_silu(g) * u).astype(o_ref.dtype)


def _ffn_gateup(h, wg, wu):
    m, k = h.shape
    n = wg.shape[1]
    tm, tn = _tile(m, WCAST_TILE_M), _tile(n, GATEUP_TILE_N)
    wspec = pl.BlockSpec((k, tn), lambda i, j: (0, j))
    return pl.pallas_call(
        _gateup_kernel,
        grid=(m // tm, n // tn),
        in_specs=[pl.BlockSpec((tm, k), lambda i, j: (i, 0)), wspec, wspec],
        out_specs=pl.BlockSpec((tm, tn), lambda i, j: (i, j)),
        out_shape=jax.ShapeDtypeStruct((m, n), BF16),
        compiler_params=_params("parallel", "arbitrary", vmem_limit_bytes=VMEM_LIMIT_WIDE_BYTES),
        name="ffn_gateup",
    )(h, wg, wu)


def _ffn(x, norm_w, w_gate, w_up, w_down):
    h = _rmsnorm(x, norm_w, NORM_EPS, BF16)
    return _matmul_residual(_ffn_gateup(h, w_gate, w_up), w_down.astype(BF16), x)


def _ssd_norm_dt_kernel(x_ref, nw_ref, w_ref, bias_ref, alog_ref, h_ref, dt_ref, cs_ref, cst_ref, *,
                        n_groups, eps):
    h = _rms_normalize(x_ref[...], nw_ref[...], eps).astype(BF16)
    h_ref[...] = h
    lc = h.shape[0]
    nh = bias_ref.shape[1]
    r = nh // n_groups
    raw = jnp.dot(h, w_ref[...].astype(BF16), preferred_element_type=F32)[:, :nh]
    dt = _softplus(raw + bias_ref[...])
    la = dt * (-LOG2_E * jnp.exp(alog_ref[...]))
    row = lax.broadcasted_iota(jnp.int32, (lc, lc), 0)
    col = lax.broadcasted_iota(jnp.int32, (lc, lc), 1)
    tril3 = jnp.concatenate([(row >= col).astype(BF16)] * 3, axis=1)
    cs = jnp.dot(tril3, _split3_rows(la), preferred_element_type=F32)
    cst = cs.T
    for g in range(n_groups):
        dt_ref[0, g] = dt[:, g * r:(g + 1) * r]
        cs_ref[0, g] = cs[:, g * r:(g + 1) * r]
        cst_ref[0, g] = cst[g * r:(g + 1) * r, :]


def _ssd_norm_dt(x, norm_w, in_proj, col0, dt_bias, a_log, n_groups):
    t, d = x.shape
    nh = dt_bias.shape[0]
    r = nh // n_groups
    lc = SSD_CHUNK
    nblk = t // lc
    assert col0 % V7X_LANES == 0 and nh <= V7X_LANES
    rows = pl.BlockSpec((lc, d), lambda i: (i, 0))
    vec = pl.BlockSpec((1, nh), lambda i: (0, 0))
    tok = pl.BlockSpec((1, n_groups, lc, r), lambda i: (i, 0, 0, 0))
    head = pl.BlockSpec((1, n_groups, r, lc), lambda i: (i, 0, 0, 0))
    return pl.pallas_call(
        functools.partial(_ssd_norm_dt_kernel, n_groups=n_groups, eps=NORM_EPS),
        grid=(nblk,),
        in_specs=[rows, pl.BlockSpec((1, d), lambda i: (0, 0)),
                  pl.BlockSpec((d, V7X_LANES), lambda i: (0, col0 // V7X_LANES)), vec, vec],
        out_specs=[rows, tok, tok, head],
        out_shape=[jax.ShapeDtypeStruct((t, d), BF16),
                   jax.ShapeDtypeStruct((nblk, n_groups, lc, r), F32),
                   jax.ShapeDtypeStruct((nblk, n_groups, lc, r), F32),
                   jax.ShapeDtypeStruct((nblk, n_groups, r, lc), F32)],
        compiler_params=_params("parallel"),
        name="ssd_norm_dt",
    )(x, norm_w.reshape(1, d).astype(F32), in_proj,
      dt_bias.reshape(1, nh).astype(F32), a_log.reshape(1, nh).astype(F32))


def _ssd_scan_kernel(x_ref, b_ref, c_ref, z_ref, dt_ref, cs_ref, cst_ref, dskip_ref, nw_ref,
                     o_ref, state_ref, *, heads, head_dim, eps):
    @pl.when(pl.program_id(2) == 0)
    def _():
        state_ref[...] = jnp.zeros_like(state_ref)

    lc = x_ref.shape[0]
    half = lc // 2
    gw = heads * head_dim
    pair = 2 * head_dim
    x = x_ref[...].astype(F32)
    bm = b_ref[...]
    cm = c_ref[...]
    dt = dt_ref[0, 0]
    cs = cs_ref[0, 0]
    cst = cst_ref[0, 0]

    e_row = lax.broadcasted_iota(jnp.int32, (3 * heads, gw), 0) % heads
    e_col = lax.broadcasted_iota(jnp.int32, (3 * heads, gw), 1) // head_dim
    expand_m = (e_row == e_col).astype(BF16)
    expand = lambda a: jnp.dot(_split3_cols(a), expand_m, preferred_element_type=F32)

    cs_end = cs[lc - 1:lc, :]
    dt_e = expand(dt)
    ecs_e = expand(jnp.exp2(cs))
    dtd_e = expand(dt * jnp.exp2(cs_end - cs))

    xdt = (x * dt_e).astype(BF16)
    cb = lax.dot_general(cm, bm, (((1,), (1,)), ((), ())), preferred_element_type=F32)
    cb_tl, cb_bl, cb_br = cb[:half, :half], cb[half:, :half], cb[half:, half:]
    tri = lax.broadcasted_iota(jnp.int32, (half, half), 0) >= lax.broadcasted_iota(jnp.int32, (half, half), 1)
    first_half = lax.broadcasted_iota(jnp.int32, (lc, pair), 1) < head_dim
    zero_tr = jnp.zeros((half, half), F32)

    def masked_cb(r):
        col, row = cs[:, r:r + 1], cst[r:r + 1, :]
        m_tl = cb_tl * jnp.exp2(jnp.where(tri, col[:half] - row[:, :half], -jnp.inf))
        m_bl = cb_bl * jnp.exp2(col[half:] - row[:, :half])
        m_br = cb_br * jnp.exp2(jnp.where(tri, col[half:] - row[:, half:], -jnp.inf))
        top = jnp.concatenate([m_tl, zero_tr], axis=1)
        return jnp.concatenate([top, jnp.concatenate([m_bl, m_br], axis=1)], axis=0).astype(BF16)

    ys = []
    for p in range(heads // 2):
        xp = xdt[:, p * pair:(p + 1) * pair]
        xa = jnp.where(first_half, xp, jnp.zeros_like(xp))
        xb = jnp.where(first_half, jnp.zeros_like(xp), xp)
        ys.append(jnp.dot(masked_cb(2 * p), xa, preferred_element_type=F32)
                  + jnp.dot(masked_cb(2 * p + 1), xb, preferred_element_type=F32))
    y = jnp.concatenate(ys, axis=1) if len(ys) > 1 else ys[0]

    state = state_ref[...]
    y = y + jnp.dot(cm, state.astype(BF16), preferred_element_type=F32) * ecs_e
    state_ref[...] = state * ecs_e[lc - 1:lc, :] + lax.dot_general(
        bm, (x * dtd_e).astype(BF16), (((0,), (0,)), ((), ())), preferred_element_type=F32)

    y = y + x * dskip_ref[...]
    y = y * _silu(z_ref[...].astype(F32))
    ms = jnp.mean(y * y, axis=-1, keepdims=True)
    o_ref[...] = (y * lax.rsqrt(ms + eps) * nw_ref[...]).astype(o_ref.dtype)


def _ssd_scan(z, xbc, dt, cs, cst, d_skip, norm_w, batch, d_inner, n_groups, heads):
    t = z.shape[0]
    lc = SSD_CHUNK
    nc = t // batch // lc
    n = SSD_D_STATE
    gw = d_inner // n_groups
    head_dim = gw // heads
    assert heads % 2 == 0 and 2 * head_dim == V7X_LANES
    row = lambda b, g, c: b * nc + c
    small = lambda shape: pl.BlockSpec(shape, lambda b, g, c: (row(b, g, c), g, 0, 0))
    return pl.pallas_call(
        functools.partial(_ssd_scan_kernel, heads=heads, head_dim=head_dim, eps=INNER_NORM_EPS),
        grid=(batch, n_groups, nc),
        in_specs=[
            pl.BlockSpec((lc, gw), lambda b, g, c: (row(b, g, c), g)),
            pl.BlockSpec((lc, n), lambda b, g, c: (row(b, g, c), d_inner // n + g)),
            pl.BlockSpec((lc, n), lambda b, g, c: (row(b, g, c), d_inner // n + n_groups + g)),
            pl.BlockSpec((lc, gw), lambda b, g, c: (row(b, g, c), g)),
            small((1, 1, lc, heads)), small((1, 1, lc, heads)), small((1, 1, heads, lc)),
            pl.BlockSpec((1, gw), lambda b, g, c: (0, g)),
            pl.BlockSpec((1, gw), lambda b, g, c: (0, g)),
        ],
        out_specs=pl.BlockSpec((lc, gw), lambda b, g, c: (row(b, g, c), g)),
        out_shape=jax.ShapeDtypeStruct((t, d_inner), BF16),
        scratch_shapes=[pltpu.VMEM((n, gw), F32)],
        compiler_params=_params("parallel", "parallel", "arbitrary"),
        name="ssd_scan",
    )(xbc, xbc, xbc, z, dt, cs, cst,
      jnp.repeat(d_skip.astype(F32), head_dim).reshape(1, d_inner),
      norm_w.reshape(1, d_inner).astype(F32))


def _ssd_mixer(x, batch, norm_w, in_proj, conv_w, conv_b, dt_bias, a_log, d_skip, inner_norm_w, out_proj):
    d_inner = out_proj.shape[0]
    conv_dim = conv_w.shape[0]
    n_heads = dt_bias.shape[0]
    n_groups = (conv_dim - d_inner) // 2 // SSD_D_STATE
    main = d_inner + conv_dim
    h, dt, cs, cst = _ssd_norm_dt(x, norm_w, in_proj, main, dt_bias, a_log, n_groups)
    z = _matmul_wcast(h, in_proj, d_inner, BF16)
    xbc = _matmul_wcast_conv(h, in_proj, d_inner, conv_w, conv_b, x.shape[0] // batch)
    y = _ssd_scan(z, xbc, dt, cs, cst, d_skip, inner_norm_w, batch, d_inner, n_groups,
                  n_heads // n_groups)
    return _matmul_residual(y, out_proj.astype(BF16), x)


def _gla_norm_gate_kernel(x_ref, nw_ref, wlow_ref, up_ref, bias_ref, h_ref, o_ref, *, eps):
    h = _rms_normalize(x_ref[...], nw_ref[...], eps).astype(BF16)
    h_ref[...] = h
    rank = up_ref.shape[0]
    low = jnp.dot(h, wlow_ref[...].astype(BF16), preferred_element_type=F32)[:, :rank]
    l_hi, l_mid, l_lo = _pieces3(low)
    u_hi, u_mid, u_lo = _pieces3(up_ref[...])
    lhs = jnp.concatenate([l_hi, l_mid, l_lo, l_hi, l_mid, l_hi], axis=1).astype(BF16)
    rhs = jnp.concatenate([u_hi, u_hi, u_hi, u_mid, u_mid, u_lo], axis=0).astype(BF16)
    x = jnp.dot(lhs, rhs, preferred_element_type=F32) + bias_ref[...]
    o_ref[...] = (jnp.minimum(x, 0.0) - jnp.log1p(jnp.exp(-jnp.abs(x)))) * (1.0 / GLA_GATE_NORMALIZER)


def _gla_norm_gate(x, norm_w, w_low, gk_up, gk_bias):
    t, d = x.shape
    rank, kd = gk_up.shape
    pad = (-rank) % V7X_LANES
    w_low = jnp.pad(w_low.astype(F32), ((0, 0), (0, pad)))
    gk_up = gk_up.astype(F32)
    tr = _tile(t, NORM_ROW_TILE, V7X_F32_SUBLANES)
    rows = pl.BlockSpec((tr, d), lambda i: (i, 0))
    return pl.pallas_call(
        functools.partial(_gla_norm_gate_kernel, eps=NORM_EPS),
        grid=(t // tr,),
        in_specs=[rows, pl.BlockSpec((1, d), lambda i: (0, 0)),
                  pl.BlockSpec((d, rank + pad), lambda i: (0, 0)),
                  pl.BlockSpec((rank, kd), lambda i: (0, 0)),
                  pl.BlockSpec((1, kd), lambda i: (0, 0))],
        out_specs=[rows, pl.BlockSpec((tr, kd), lambda i: (i, 0))],
        out_shape=[jax.ShapeDtypeStruct((t, d), BF16), jax.ShapeDtypeStruct((t, kd), F32)],
        compiler_params=_params("parallel"),
        name="gla_norm_gate",
    )(x, norm_w.reshape(1, d).astype(F32), w_low, gk_up, gk_bias.reshape(1, kd).astype(F32))


def _gla_scan_kernel(q_ref, k_ref, v_ref, g_ref, go_ref, nw_ref, o_ref, state_ref, *,
                     heads, chunk, scale, eps):
    @pl.when(pl.program_id(2) == 0)
    def _():
        state_ref[...] = jnp.zeros_like(state_ref)

    rows = q_ref.shape[0]
    nsub = rows // chunk
    hk = q_ref.shape[1] // heads
    hv = v_ref.shape[1] // heads
    shift = chunk.bit_length() - 1
    ri = lax.broadcasted_iota(jnp.int32, (rows, rows), 0)
    ci = lax.broadcasted_iota(jnp.int32, (rows, rows), 1)
    causal = ri >= ci
    in_chunk_causal = causal & (lax.shift_right_logical(ri, shift) == lax.shift_right_logical(ci, shift))
    cumsum_m = jnp.concatenate([in_chunk_causal.astype(BF16)] * 3, axis=1)
    blk = lambda a, s: a[s * chunk:(s + 1) * chunk]
    q = q_ref[...].astype(F32) * scale
    k = k_ref[...].astype(F32)
    cs = jnp.dot(cumsum_m, _split3_rows(g_ref[...]), preferred_element_type=F32)
    tot = [cs[(s + 1) * chunk - 1:(s + 1) * chunk, :] for s in range(nsub)]
    before = [jnp.zeros_like(tot[0])]
    for s in range(nsub):
        before.append(before[s] + tot[s])
    total = before[nsub]
    q_dec = q * jnp.exp(cs)
    q_dec_bf = q_dec.astype(BF16)
    k_inv = (k * jnp.exp(-cs)).astype(BF16)
    k_end = [blk(k, s) * jnp.exp(tot[s] - blk(cs, s)) for s in range(nsub)]
    q_in = jnp.concatenate([blk(q_dec, s) * jnp.exp(before[s]) for s in range(nsub)], axis=0).astype(BF16)
    k_out = jnp.concatenate([k_end[s] * jnp.exp(total - before[s + 1]) for s in range(nsub)],
                            axis=0).astype(BF16)
    keys = []
    for s in range(nsub):
        rows_s = [(k_end[j] * jnp.exp(before[s] - before[j + 1])).astype(BF16) for j in range(s)]
        rows_s.append(blk(k_inv, s))
        rows_s += [jnp.zeros((chunk, heads * hk), BF16)] * (nsub - 1 - s)
        keys.append(jnp.concatenate(rows_s, axis=0))
    total_col = jnp.exp(jnp.broadcast_to(total, (V7X_LANES, heads * hk)).T[:, 0:1])
    for hh in range(heads):
        kc = slice(hh * hk, (hh + 1) * hk)
        vc = slice(hh * hv, (hh + 1) * hv)
        v = v_ref[:, vc]
        score_rows = [lax.dot_general(blk(q_dec_bf, s)[:, kc], keys[s][:, kc], (((1,), (1,)), ((), ())),
                                      preferred_element_type=F32) for s in range(nsub)]
        scores = jnp.where(causal, jnp.concatenate(score_rows, axis=0), 0.0).astype(BF16)
        state = state_ref[hh]
        o = (jnp.dot(scores, v, preferred_element_type=F32)
             + jnp.dot(q_in[:, kc], state.astype(BF16), preferred_element_type=F32))
        state_ref[hh] = state * total_col[kc] + lax.dot_general(
            k_out[:, kc], v, (((0,), (0,)), ((), ())), preferred_element_type=F32)
        ms = jnp.mean(o * o, axis=-1, keepdims=True)
        o = o * lax.rsqrt(ms + eps) * nw_ref[...]
        o_ref[:, vc] = (o * _silu(go_ref[:, vc].astype(F32))).astype(o_ref.dtype)


def _gla_scan(qkvg, log_g, norm_w, batch, key_dim, value_dim, n_heads):
    t = qkvg.shape[0]
    hk, hv = key_dim // n_heads, value_dim // n_heads
    seqlen = t // batch
    rb = _tile(seqlen, GLA_ROW_BLOCK, GLA_CHUNK)
    nb = seqlen // rb
    hpb = GLA_HEADS_PER_STEP if n_heads % GLA_HEADS_PER_STEP == 0 else 1
    kw, vw = hpb * hk, hpb * hv
    assert rb % GLA_CHUNK == 0 and (2 * key_dim) % vw == 0 and value_dim % vw == 0
    row = lambda b, h, i: b * nb + i
    k0 = key_dim // kw
    v0 = 2 * key_dim // vw
    g0 = (2 * key_dim + value_dim) // vw
    return pl.pallas_call(
        functools.partial(_gla_scan_kernel, heads=hpb, chunk=GLA_CHUNK, scale=hk ** -0.5,
                          eps=INNER_NORM_EPS),
        grid=(batch, n_heads // hpb, nb),
        in_specs=[
            pl.BlockSpec((rb, kw), lambda b, h, i: (row(b, h, i), h)),
            pl.BlockSpec((rb, kw), lambda b, h, i: (row(b, h, i), k0 + h)),
            pl.BlockSpec((rb, vw), lambda b, h, i: (row(b, h, i), v0 + h)),
            pl.BlockSpec((rb, kw), lambda b, h, i: (row(b, h, i), h)),
            pl.BlockSpec((rb, vw), lambda b, h, i: (row(b, h, i), g0 + h)),
            pl.BlockSpec((1, hv), lambda b, h, i: (0, 0)),
        ],
        out_specs=pl.BlockSpec((rb, vw), lambda b, h, i: (row(b, h, i), h)),
        out_shape=jax.ShapeDtypeStruct((t, value_dim), BF16),
        scratch_shapes=[pltpu.VMEM((hpb, hk, hv), F32)],
        compiler_params=_params("parallel", "parallel", "arbitrary"),
        name="gla_scan",
    )(qkvg, qkvg, qkvg, log_g, qkvg, norm_w.reshape(1, hv).astype(F32))


def _gla_mixer(x, batch, norm_w, in_proj, gk_up, gk_bias, inner_norm_w, out_proj):
    key_dim = gk_up.shape[1]
    value_dim = out_proj.shape[0]
    n_heads = value_dim // inner_norm_w.shape[0]
    main = 2 * key_dim + 2 * value_dim
    h, log_g = _gla_norm_gate(x, norm_w, in_proj[:, main:], gk_up, gk_bias)
    qkvg = _matmul_wcast(h, in_proj, main, BF16)
    o = _gla_scan(qkvg, log_g, inner_norm_w, batch, key_dim, value_dim, n_heads)
    return _matmul_residual(o, out_proj.astype(BF16), x)


def kernel(x, l0_mixer_norm, l0_ssd_in_proj, l0_ssd_conv_w, l0_ssd_conv_b, l0_ssd_dt_bias, l0_ssd_a_log, l0_ssd_d, l0_ssd_norm_w, l0_ssd_out_proj, l0_ffn_norm, l0_ffn_w_gate, l0_ffn_w_up, l0_ffn_w_down, l1_mixer_norm, l1_gla_in_proj, l1_gla_gk_up, l1_gla_gk_bias, l1_gla_norm_w, l1_gla_out_proj, l1_ffn_norm, l1_ffn_w_gate, l1_ffn_w_up, l1_ffn_w_down, final_norm):
    batch, seqlen, d = x.shape
    xf = x.reshape(batch * seqlen, d).astype(F32)
    xf = _ssd_mixer(xf, batch, l0_mixer_norm, l0_ssd_in_proj, l0_ssd_conv_w, l0_ssd_conv_b,
                    l0_ssd_dt_bias, l0_ssd_a_log, l0_ssd_d, l0_ssd_norm_w, l0_ssd_out_proj)
    xf = _ffn(xf, l0_ffn_norm, l0_ffn_w_gate, l0_ffn_w_up, l0_ffn_w_down)
    xf = _gla_mixer(xf, batch, l1_mixer_norm, l1_gla_in_proj, l1_gla_gk_up, l1_gla_gk_bias,
                    l1_gla_norm_w, l1_gla_out_proj)
    xf = _ffn(xf, l1_ffn_norm, l1_ffn_w_gate, l1_ffn_w_up, l1_ffn_w_down)
    return _rmsnorm(xf, final_norm, NORM_EPS, x.dtype).reshape(batch, seqlen, d)
```

```python
import functools

import jax
import jax.numpy as jnp
from jax import lax
from jax.experimental import pallas as pl
from jax.experimental.pallas import tpu as pltpu

F32 = jnp.float32
BF16 = jnp.bfloat16

NORM_EPS = 1e-6
INNER_NORM_EPS = 1e-5
SSD_D_STATE = 128
SSD_CHUNK = 256
GLA_CHUNK = 64
GLA_GATE_NORMALIZER = 16.0
LOG2_E = 1.4426950408889634

V7X_LANES = 128
V7X_F32_SUBLANES = 8
V7X_BF16_SUBLANES = 16
V7X_VMEM_BYTES = 64 * 1024 * 1024
VMEM_LIMIT_BYTES = V7X_VMEM_BYTES - 8 * 1024 * 1024
VMEM_LIMIT_WIDE_BYTES = V7X_VMEM_BYTES - 1 * 1024 * 1024

MM_RES_VMEM_BUDGET = 46 * 1024 * 1024
MM_RES_DOUBLE_W_BUDGET = 56 * 1024 * 1024
RES_ROW_TILES = (2048, 1024, 512)
RES_COL_TILES = (1024, 512, 256)
WCAST_TILE_M = 2048
WCAST_TILE_N = 512
GATEUP_TILE_N = 256
ROW_TILE = 256
NORM_ROW_TILE = 512
GLA_ROW_BLOCK = 256
GLA_HEADS_PER_STEP = 8


def _params(*semantics, vmem_limit_bytes=VMEM_LIMIT_BYTES):
    return pltpu.CompilerParams(dimension_semantics=semantics, vmem_limit_bytes=vmem_limit_bytes)


def _tile(dim, pref, align=V7X_LANES):
    if dim <= pref:
        return dim
    t = (pref // align) * align
    while t >= align:
        if dim % t == 0:
            return t
        t -= align
    return dim


def _silu(x):
    h = 0.5 * x
    return h + h * jnp.tanh(h)


def _softplus(x):
    return jnp.maximum(x, 0.0) + jnp.log1p(jnp.exp(-jnp.abs(x)))


def _pieces3(a):
    hi = a.astype(BF16).astype(F32)
    mid = (a - hi).astype(BF16).astype(F32)
    return hi, mid, a - hi - mid


def _split3_cols(a):
    return jnp.concatenate(_pieces3(a), axis=1).astype(BF16)


def _split3_rows(a):
    return jnp.concatenate(_pieces3(a), axis=0).astype(BF16)


def _rms_normalize(x, w, eps):
    ms = jnp.mean(x * x, axis=-1, keepdims=True)
    return x * lax.rsqrt(ms + eps) * w


def _rmsnorm_kernel(x_ref, w_ref, o_ref, *, eps):
    o_ref[...] = _rms_normalize(x_ref[...], w_ref[...], eps).astype(o_ref.dtype)


def _rmsnorm(x, w, eps, out_dtype):
    t, d = x.shape
    tr = _tile(t, NORM_ROW_TILE, V7X_F32_SUBLANES)
    return pl.pallas_call(
        functools.partial(_rmsnorm_kernel, eps=eps),
        grid=(t // tr,),
        in_specs=[pl.BlockSpec((tr, d), lambda i: (i, 0)), pl.BlockSpec((1, d), lambda i: (0, 0))],
        out_specs=pl.BlockSpec((tr, d), lambda i: (i, 0)),
        out_shape=jax.ShapeDtypeStruct((t, d), out_dtype),
        compiler_params=_params("parallel"),
        name="rmsnorm",
    )(x, w.reshape(1, d).astype(F32))


def _mm_wcast_kernel(a_ref, w_ref, o_ref):
    w = w_ref[...].astype(BF16)
    o_ref[...] = jnp.dot(a_ref[...], w, preferred_element_type=F32).astype(o_ref.dtype)


def _matmul_wcast(a, w, n_cols, out_dtype):
    m, k = a.shape
    tm, tn = _tile(m, WCAST_TILE_M), _tile(n_cols, WCAST_TILE_N)
    return pl.pallas_call(
        _mm_wcast_kernel,
        grid=(m // tm, n_cols // tn),
        in_specs=[pl.BlockSpec((tm, k), lambda i, j: (i, 0)), pl.BlockSpec((k, tn), lambda i, j: (0, j))],
        out_specs=pl.BlockSpec((tm, tn), lambda i, j: (i, j)),
        out_shape=jax.ShapeDtypeStruct((m, n_cols), out_dtype),
        compiler_params=_params("parallel", "arbitrary", vmem_limit_bytes=VMEM_LIMIT_WIDE_BYTES),
        name="matmul_wcast",
    )(a, w)


def _mm_wcast_conv_kernel(a_ref, w_ref, cw_ref, cb_ref, o_ref, buf_ref, carry_ref, *, halo, tiles_per_seq):
    i, j = pl.program_id(0), pl.program_id(1)
    tm = a_ref.shape[0]
    kw = cw_ref.shape[0]

    @pl.when((i == 0) & (j == 0))
    def _():
        carry_ref[...] = jnp.zeros_like(carry_ref)

    acc = jnp.dot(a_ref[...], w_ref[...].astype(BF16), preferred_element_type=F32)
    buf_ref[0:halo, :] = jnp.where(lax.rem(i, tiles_per_seq) == 0, 0.0, carry_ref[j])
    buf_ref[halo:halo + tm, :] = acc
    carry_ref[j] = acc[tm - halo:, :]
    out = cb_ref[...] + acc * cw_ref[kw - 1:kw, :]
    for tap in range(kw - 1):
        off = halo - (kw - 1) + tap
        out = out + buf_ref[off:off + tm, :] * cw_ref[tap:tap + 1, :]
    o_ref[...] = _silu(out).astype(o_ref.dtype)


def _matmul_wcast_conv(a, w, col0, conv_w, conv_b, seqlen):
    m, k = a.shape
    c, kw = conv_w.shape
    halo = V7X_BF16_SUBLANES
    tm, tn = _tile(seqlen, WCAST_TILE_M, halo), _tile(c, WCAST_TILE_N)
    assert col0 % tn == 0 and seqlen % tm == 0 and kw - 1 <= halo
    return pl.pallas_call(
        functools.partial(_mm_wcast_conv_kernel, halo=halo, tiles_per_seq=seqlen // tm),
        grid=(m // tm, c // tn),
        in_specs=[pl.BlockSpec((tm, k), lambda i, j: (i, 0)),
                  pl.BlockSpec((k, tn), lambda i, j: (0, col0 // tn + j)),
                  pl.BlockSpec((kw, tn), lambda i, j: (0, j)),
                  pl.BlockSpec((1, tn), lambda i, j: (0, j))],
        out_specs=pl.BlockSpec((tm, tn), lambda i, j: (i, j)),
        out_shape=jax.ShapeDtypeStruct((m, c), BF16),
        scratch_shapes=[pltpu.VMEM((tm + halo, tn), F32), pltpu.VMEM((c // tn, halo, tn), F32)],
        compiler_params=_params("arbitrary", "arbitrary", vmem_limit_bytes=V7X_VMEM_BYTES),
        name="matmul_wcast_conv",
    )(a, w, conv_w.T.astype(F32), conv_b.reshape(1, c).astype(F32))


def _mm_res_kernel(a_ref, w_ref, r_ref, o_ref):
    o_ref[...] = r_ref[...] + jnp.dot(a_ref[...], w_ref[...], preferred_element_type=F32)


def _res_tiles(m, k, n):
    best = None
    for tm in RES_ROW_TILES:
        for tn in RES_COL_TILES:
            if m % tm or n % tn:
                continue
            need = k * tn * 2 + 2 * tm * k * 2 + 5 * tm * tn * 4
            if need <= MM_RES_VMEM_BUDGET and (best is None or tm * tn > best[0] * best[1]):
                best = (tm, tn)
    return best if best is not None else (_tile(m, ROW_TILE, V7X_F32_SUBLANES), _tile(n, V7X_LANES))


def _matmul_residual(a, w, res):
    m, k = a.shape
    n = w.shape[1]
    tm, tn = _res_tiles(m, k, n)
    double_w = 2 * k * tn * 2 + 2 * tm * k * 2 + 5 * tm * tn * 4 <= MM_RES_DOUBLE_W_BUDGET
    w_spec = pl.BlockSpec((k, tn), lambda j, i: (0, j), pipeline_mode=pl.Buffered(2 if double_w else 1))
    return pl.pallas_call(
        _mm_res_kernel,
        grid=(n // tn, m // tm),
        in_specs=[pl.BlockSpec((tm, k), lambda j, i: (i, 0)), w_spec,
                  pl.BlockSpec((tm, tn), lambda j, i: (i, j))],
        out_specs=pl.BlockSpec((tm, tn), lambda j, i: (i, j)),
        out_shape=jax.ShapeDtypeStruct((m, n), F32),
        compiler_params=_params("parallel", "arbitrary", vmem_limit_bytes=VMEM_LIMIT_WIDE_BYTES),
        name="matmul_residual",
    )(a, w, res)


def _gateup_kernel(h_ref, wg_ref, wu_ref, o_ref):
    h = h_ref[...]
    g = jnp.dot(h, wg_ref[...].astype(BF16), preferred_element_type=F32)
    u = jnp.dot(h, wu_ref[...].astype(BF16), preferred_element_type=F32)
    o_ref[...] = (_silu(g) * u).astype(o_ref.dtype)


def _ffn_gateup(h, wg, wu):
    m, k = h.shape
    n = wg.shape[1]
    tm, tn = _tile(m, WCAST_TILE_M), _tile(n, GATEUP_TILE_N)
    wspec = pl.BlockSpec((k, tn), lambda i, j: (0, j))
    return pl.pallas_call(
        _gateup_kernel,
        grid=(m // tm, n // tn),
        in_specs=[pl.BlockSpec((tm, k), lambda i, j: (i, 0)), wspec, wspec],
        out_specs=pl.BlockSpec((tm, tn), lambda i, j: (i, j)),
        out_shape=jax.ShapeDtypeStruct((m, n), BF16),
        compiler_params=_params("parallel", "arbitrary", vmem_limit_bytes=VMEM_LIMIT_WIDE_BYTES),
        name="ffn_gateup",
    )(h, wg, wu)


def _ffn(x, norm_w, w_gate, w_up, w_down):
    h = _rmsnorm(x, norm_w, NORM_EPS, BF16)
    return _matmul_residual(_ffn_gateup(h, w_gate, w_up), w_down.astype(BF16), x)


def _ssd_norm_dt_kernel(x_ref, nw_ref, w_ref, bias_ref, alog_ref, h_ref, dt_ref, cs_ref, cst_ref, *,
                        n_groups, eps):
    h = _rms_normalize(x_ref[...], nw_ref[...], eps).astype(BF16)
    h_ref[...] = h
    lc = h.shape[0]
    nh = bias_ref.shape[1]
    r = nh // n_groups
    raw = jnp.dot(h, w_ref[...].astype(BF16), preferred_element_type=F32)[:, :nh]
    dt = _softplus(raw + bias_ref[...])
    la = dt * (-LOG2_E * jnp.exp(alog_ref[...]))
    row = lax.broadcasted_iota(jnp.int32, (lc, lc), 0)
    col = lax.broadcasted_iota(jnp.int32, (lc, lc), 1)
    tril3 = jnp.concatenate([(row >= col).astype(BF16)] * 3, axis=1)
    cs = jnp.dot(tril3, _split3_rows(la), preferred_element_type=F32)
    cst = cs.T
    for g in range(n_groups):
        dt_ref[0, g] = dt[:, g * r:(g + 1) * r]
        cs_ref[0, g] = cs[:, g * r:(g + 1) * r]
        cst_ref[0, g] = cst[g * r:(g + 1) * r, :]


def _ssd_norm_dt(x, norm_w, in_proj, col0, dt_bias, a_log, n_groups):
    t, d = x.shape
    nh = dt_bias.shape[0]
    r = nh // n_groups
    lc = SSD_CHUNK
    nblk = t // lc
    assert col0 % V7X_LANES == 0 and nh <= V7X_LANES
    rows = pl.BlockSpec((lc, d), lambda i: (i, 0))
    vec = pl.BlockSpec((1, nh), lambda i: (0, 0))
    tok = pl.BlockSpec((1, n_groups, lc, r), lambda i: (i, 0, 0, 0))
    head = pl.BlockSpec((1, n_groups, r, lc), lambda i: (i, 0, 0, 0))
    return pl.pallas_call(
        functools.partial(_ssd_norm_dt_kernel, n_groups=n_groups, eps=NORM_EPS),
        grid=(nblk,),
        in_specs=[rows, pl.BlockSpec((1, d), lambda i: (0, 0)),
                  pl.BlockSpec((d, V7X_LANES), lambda i: (0, col0 // V7X_LANES)), vec, vec],
        out_specs=[rows, tok, tok, head],
        out_shape=[jax.ShapeDtypeStruct((t, d), BF16),
                   jax.ShapeDtypeStruct((nblk, n_groups, lc, r), F32),
                   jax.ShapeDtypeStruct((nblk, n_groups, lc, r), F32),
                   jax.ShapeDtypeStruct((nblk, n_groups, r, lc), F32)],
        compiler_params=_params("parallel"),
        name="ssd_norm_dt",
    )(x, norm_w.reshape(1, d).astype(F32), in_proj,
      dt_bias.reshape(1, nh).astype(F32), a_log.reshape(1, nh).astype(F32))


def _ssd_scan_kernel(x_ref, b_ref, c_ref, z_ref, dt_ref, cs_ref, cst_ref, dskip_ref, nw_ref,
                     o_ref, state_ref, *, heads, head_dim, eps):
    @pl.when(pl.program_id(2) == 0)
    def _():
        state_ref[...] = jnp.zeros_like(state_ref)

    lc = x_ref.shape[0]
    half = lc // 2
    gw = heads * head_dim
    pair = 2 * head_dim
    x = x_ref[...].astype(F32)
    bm = b_ref[...]
    cm = c_ref[...]
    dt = dt_ref[0, 0]
    cs = cs_ref[0, 0]
    cst = cst_ref[0, 0]

    e_row = lax.broadcasted_iota(jnp.int32, (3 * heads, gw), 0) % heads
    e_col = lax.broadcasted_iota(jnp.int32, (3 * heads, gw), 1) // head_dim
    expand_m = (e_row == e_col).astype(BF16)
    expand = lambda a: jnp.dot(_split3_cols(a), expand_m, preferred_element_type=F32)

    cs_end = cs[lc - 1:lc, :]
    dt_e = expand(dt)
    ecs_e = expand(jnp.exp2(cs))
    dtd_e = expand(dt * jnp.exp2(cs_end - cs))

    xdt = (x * dt_e).astype(BF16)
    cb = lax.dot_general(cm, bm, (((1,), (1,)), ((), ())), preferred_element_type=F32)
    cb_tl, cb_bl, cb_br = cb[:half, :half], cb[half:, :half], cb[half:, half:]
    tri = lax.broadcasted_iota(jnp.int32, (half, half), 0) >= lax.broadcasted_iota(jnp.int32, (half, half), 1)
    first_half = lax.broadcasted_iota(jnp.int32, (lc, pair), 1) < head_dim
    zero_tr = jnp.zeros((half, half), F32)

    def masked_cb(r):
        col, row = cs[:, r:r + 1], cst[r:r + 1, :]
        m_tl = cb_tl * jnp.exp2(jnp.where(tri, col[:half] - row[:, :half], -jnp.inf))
        m_bl = cb_bl * jnp.exp2(col[half:] - row[:, :half])
        m_br = cb_br * jnp.exp2(jnp.where(tri, col[half:] - row[:, half:], -jnp.inf))
        top = jnp.concatenate([m_tl, zero_tr], axis=1)
        return jnp.concatenate([top, jnp.concatenate([m_bl, m_br], axis=1)], axis=0).astype(BF16)

    ys = []
    for p in range(heads // 2):
        xp = xdt[:, p * pair:(p + 1) * pair]
        xa = jnp.where(first_half, xp, jnp.zeros_like(xp))
        xb = jnp.where(first_half, jnp.zeros_like(xp), xp)
        ys.append(jnp.dot(masked_cb(2 * p), xa, preferred_element_type=F32)
                  + jnp.dot(masked_cb(2 * p + 1), xb, preferred_element_type=F32))
    y = jnp.concatenate(ys, axis=1) if len(ys) > 1 else ys[0]

    state = state_ref[...]
    y = y + jnp.dot(cm, state.astype(BF16), preferred_element_type=F32) * ecs_e
    state_ref[...] = state * ecs_e[lc - 1:lc, :] + lax.dot_general(
        bm, (x * dtd_e).astype(BF16), (((0,), (0,)), ((), ())), preferred_element_type=F32)

    y = y + x * dskip_ref[...]
    y = y * _silu(z_ref[...].astype(F32))
    ms = jnp.mean(y * y, axis=-1, keepdims=True)
    o_ref[...] = (y * lax.rsqrt(ms + eps) * nw_ref[...]).astype(o_ref.dtype)


def _ssd_scan(z, xbc, dt, cs, cst, d_skip, norm_w, batch, d_inner, n_groups, heads):
    t = z.shape[0]
    lc = SSD_CHUNK
    nc = t // batch // lc
    n = SSD_D_STATE
    gw = d_inner // n_groups
    head_dim = gw // heads
    assert heads % 2 == 0 and 2 * head_dim == V7X_LANES
    row = lambda b, g, c: b * nc + c
    small = lambda shape: pl.BlockSpec(shape, lambda b, g, c: (row(b, g, c), g, 0, 0))
    return pl.pallas_call(
        functools.partial(_ssd_scan_kernel, heads=heads, head_dim=head_dim, eps=INNER_NORM_EPS),
        grid=(batch, n_groups, nc),
        in_specs=[
            pl.BlockSpec((lc, gw), lambda b, g, c: (row(b, g, c), g)),
            pl.BlockSpec((lc, n), lambda b, g, c: (row(b, g, c), d_inner // n + g)),
            pl.BlockSpec((lc, n), lambda b, g, c: (row(b, g, c), d_inner // n + n_groups + g)),
            pl.BlockSpec((lc, gw), lambda b, g, c: (row(b, g, c), g)),
            small((1, 1, lc, heads)), small((1, 1, lc, heads)), small((1, 1, heads, lc)),
            pl.BlockSpec((1, gw), lambda b, g, c: (0, g)),
            pl.BlockSpec((1, gw), lambda b, g, c: (0, g)),
        ],
        out_specs=pl.BlockSpec((lc, gw), lambda b, g, c: (row(b, g, c), g)),
        out_shape=jax.ShapeDtypeStruct((t, d_inner), BF16),
        scratch_shapes=[pltpu.VMEM((n, gw), F32)],
        compiler_params=_params("parallel", "parallel", "arbitrary"),
        name="ssd_scan",
    )(xbc, xbc, xbc, z, dt, cs, cst,
      jnp.repeat(d_skip.astype(F32), head_dim).reshape(1, d_inner),
      norm_w.reshape(1, d_inner).astype(F32))


def _ssd_mixer(x, batch, norm_w, in_proj, conv_w, conv_b, dt_bias, a_log, d_skip, inner_norm_w, out_proj):
    d_inner = out_proj.shape[0]
    conv_dim = conv_w.shape[0]
    n_heads = dt_bias.shape[0]
    n_groups = (conv_dim - d_inner) // 2 // SSD_D_STATE
    main = d_inner + conv_dim
    h, dt, cs, cst = _ssd_norm_dt(x, norm_w, in_proj, main, dt_bias, a_log, n_groups)
    z = _matmul_wcast(h, in_proj, d_inner, BF16)
    xbc = _matmul_wcast_conv(h, in_proj, d_inner, conv_w, conv_b, x.shape[0] // batch)
    y = _ssd_scan(z, xbc, dt, cs, cst, d_skip, inner_norm_w, batch, d_inner, n_groups,
                  n_heads // n_groups)
    return _matmul_residual(y, out_proj.astype(BF16), x)


def _gla_norm_gate_kernel(x_ref, nw_ref, wlow_ref, up_ref, bias_ref, h_ref, o_ref, *, eps):
    h = _rms_normalize(x_ref[...], nw_ref[...], eps).astype(BF16)
    h_ref[...] = h
    rank = up_ref.shape[0]
    low = jnp.dot(h, wlow_ref[...].astype(BF16), preferred_element_type=F32)[:, :rank]
    l_hi, l_mid, l_lo = _pieces3(low)
    u_hi, u_mid, u_lo = _pieces3(up_ref[...])
    lhs = jnp.concatenate([l_hi, l_mid, l_lo, l_hi, l_mid, l_hi], axis=1).astype(BF16)
    rhs = jnp.concatenate([u_hi, u_hi, u_hi, u_mid, u_mid, u_lo], axis=0).astype(BF16)
    x = jnp.dot(lhs, rhs, preferred_element_type=F32) + bias_ref[...]
    o_ref[...] = (jnp.minimum(x, 0.0) - jnp.log1p(jnp.exp(-jnp.abs(x)))) * (1.0 / GLA_GATE_NORMALIZER)


def _gla_norm_gate(x, norm_w, w_low, gk_up, gk_bias):
    t, d = x.shape
    rank, kd = gk_up.shape
    pad = (-rank) % V7X_LANES
    w_low = jnp.pad(w_low.astype(F32), ((0, 0), (0, pad)))
    gk_up = gk_up.astype(F32)
    tr = _tile(t, NORM_ROW_TILE, V7X_F32_SUBLANES)
    rows = pl.BlockSpec((tr, d), lambda i: (i, 0))
    return pl.pallas_call(
        functools.partial(_gla_norm_gate_kernel, eps=NORM_EPS),
        grid=(t // tr,),
        in_specs=[rows, pl.BlockSpec((1, d), lambda i: (0, 0)),
                  pl.BlockSpec((d, rank + pad), lambda i: (0, 0)),
                  pl.BlockSpec((rank, kd), lambda i: (0, 0)),
                  pl.BlockSpec((1, kd), lambda i: (0, 0))],
        out_specs=[rows, pl.BlockSpec((tr, kd), lambda i: (i, 0))],
        out_shape=[jax.ShapeDtypeStruct((t, d), BF16), jax.ShapeDtypeStruct((t, kd), F32)],
        compiler_params=_params("parallel"),
        name="gla_norm_gate",
    )(x, norm_w.reshape(1, d).astype(F32), w_low, gk_up, gk_bias.reshape(1, kd).astype(F32))


def _gla_scan_kernel(q_ref, k_ref, v_ref, g_ref, go_ref, nw_ref, o_ref, state_ref, *,
                     heads, chunk, scale, eps):
    @pl.when(pl.program_id(2) == 0)
    def _():
        state_ref[...] = jnp.zeros_like(state_ref)

    rows = q_ref.shape[0]
    nsub = rows // chunk
    hk = q_ref.shape[1] // heads
    hv = v_ref.shape[1] // heads
    shift = chunk.bit_length() - 1
    ri = lax.broadcasted_iota(jnp.int32, (rows, rows), 0)
    ci = lax.broadcasted_iota(jnp.int32, (rows, rows), 1)
    causal = ri >= ci
    in_chunk_causal = causal & (lax.shift_right_logical(ri, shift) == lax.shift_right_logical(ci, shift))
    cumsum_m = jnp.concatenate([in_chunk_causal.astype(BF16)] * 3, axis=1)
    blk = lambda a, s: a[s * chunk:(s + 1) * chunk]
    q = q_ref[...].astype(F32) * scale
    k = k_ref[...].astype(F32)
    cs = jnp.dot(cumsum_m, _split3_rows(g_ref[...]), preferred_element_type=F32)
    tot = [cs[(s + 1) * chunk - 1:(s + 1) * chunk, :] for s in range(nsub)]
    before = [jnp.zeros_like(tot[0])]
    for s in range(nsub):
        before.append(before[s] + tot[s])
    total = before[nsub]
    q_dec = q * jnp.exp(cs)
    q_dec_bf = q_dec.astype(BF16)
    k_inv = (k * jnp.exp(-cs)).astype(BF16)
    k_end = [blk(k, s) * jnp.exp(tot[s] - blk(cs, s)) for s in range(nsub)]
    q_in = jnp.concatenate([blk(q_dec, s) * jnp.exp(before[s]) for s in range(nsub)], axis=0).astype(BF16)
    k_out = jnp.concatenate([k_end[s] * jnp.exp(total - before[s + 1]) for s in range(nsub)],
                            axis=0).astype(BF16)
    keys = []
    for s in range(nsub):
        rows_s = [(k_end[j] * jnp.exp(before[s] - before[j + 1])).astype(BF16) for j in range(s)]
        rows_s.append(blk(k_inv, s))
        rows_s += [jnp.zeros((chunk, heads * hk), BF16)] * (nsub - 1 - s)
        keys.append(jnp.concatenate(rows_s, axis=0))
    total_col = jnp.exp(jnp.broadcast_to(total, (V7X_LANES, heads * hk)).T[:, 0:1])
    for hh in range(heads):
        kc = slice(hh * hk, (hh + 1) * hk)
        vc = slice(hh * hv, (hh + 1) * hv)
        v = v_ref[:, vc]
        score_rows = [lax.dot_general(blk(q_dec_bf, s)[:, kc], keys[s][:, kc], (((1,), (1,)), ((), ())),
                                      preferred_element_type=F32) for s in range(nsub)]
        scores = jnp.where(causal, jnp.concatenate(score_rows, axis=0), 0.0).astype(BF16)
        state = state_ref[hh]
        o = (jnp.dot(scores, v, preferred_element_type=F32)
             + jnp.dot(q_in[:, kc], state.astype(BF16), preferred_element_type=F32))
        state_ref[hh] = state * total_col[kc] + lax.dot_general(
            k_out[:, kc], v, (((0,), (0,)), ((), ())), preferred_element_type=F32)
        ms = jnp.mean(o * o, axis=-1, keepdims=True)
        o = o * lax.rsqrt(ms + eps) * nw_ref[...]
        o_ref[:, vc] = (o * _silu(go_ref[:, vc].astype(F32))).astype(o_ref.dtype)


def _gla_scan(qkvg, log_g, norm_w, batch, key_dim, value_dim, n_heads):
    t = qkvg.shape[0]
    hk, hv = key_dim // n_heads, value_dim // n_heads
    seqlen = t // batch
    rb = _tile(seqlen, GLA_ROW_BLOCK, GLA_CHUNK)
    nb = seqlen // rb
    hpb = GLA_HEADS_PER_STEP if n_heads % GLA_HEADS_PER_STEP == 0 else 1
    kw, vw = hpb * hk, hpb * hv
    assert rb % GLA_CHUNK == 0 and (2 * key_dim) % vw == 0 and value_dim % vw == 0
    row = lambda b, h, i: b * nb + i
    k0 = key_dim // kw
    v0 = 2 * key_dim // vw
    g0 = (2 * key_dim + value_dim) // vw
    return pl.pallas_call(
        functools.partial(_gla_scan_kernel, heads=hpb, chunk=GLA_CHUNK, scale=hk ** -0.5,
                          eps=INNER_NORM_EPS),
        grid=(batch, n_heads // hpb, nb),
        in_specs=[
            pl.BlockSpec((rb, kw), lambda b, h, i: (row(b, h, i), h)),
            pl.BlockSpec((rb, kw), lambda b, h, i: (row(b, h, i), k0 + h)),
            pl.BlockSpec((rb, vw), lambda b, h, i: (row(b, h, i), v0 + h)),
            pl.BlockSpec((rb, kw), lambda b, h, i: (row(b, h, i), h)),
            pl.BlockSpec((rb, vw), lambda b, h, i: (row(b, h, i), g0 + h)),
            pl.BlockSpec((1, hv), lambda b, h, i: (0, 0)),
        ],
        out_specs=pl.BlockSpec((rb, vw), lambda b, h, i: (row(b, h, i), h)),
        out_shape=jax.ShapeDtypeStruct((t, value_dim), BF16),
        scratch_shapes=[pltpu.VMEM((hpb, hk, hv), F32)],
        compiler_params=_params("parallel", "parallel", "arbitrary"),
        name="gla_scan",
    )(qkvg, qkvg, qkvg, log_g, qkvg, norm_w.reshape(1, hv).astype(F32))


def _gla_mixer(x, batch, norm_w, in_proj, gk_up, gk_bias, inner_norm_w, out_proj):
    key_dim = gk_up.shape[1]
    value_dim = out_proj.shape[0]
    n_heads = value_dim // inner_norm_w.shape[0]
    main = 2 * key_dim + 2 * value_dim
    h, log_g = _gla_norm_gate(x, norm_w, in_proj[:, main:], gk_up, gk_bias)
    qkvg = _matmul_wcast(h, in_proj, main, BF16)
    o = _gla_scan(qkvg, log_g, inner_norm_w, batch, key_dim, value_dim, n_heads)
    return _matmul_residual(o, out_proj.astype(BF16), x)


def kernel(x, l0_mixer_norm, l0_ssd_in_proj, l0_ssd_conv_w, l0_ssd_conv_b, l0_ssd_dt_bias, l0_ssd_a_log, l0_ssd_d, l0_ssd_norm_w, l0_ssd_out_proj, l0_ffn_norm, l0_ffn_w_gate, l0_ffn_w_up, l0_ffn_w_down, l1_mixer_norm, l1_gla_in_proj, l1_gla_gk_up, l1_gla_gk_bias, l1_gla_norm_w, l1_gla_out_proj, l1_ffn_norm, l1_ffn_w_gate, l1_ffn_w_up, l1_ffn_w_down, final_norm):
    batch, seqlen, d = x.shape
    xf = x.reshape(batch * seqlen, d).astype(F32)
    xf = _ssd_mixer(xf, batch, l0_mixer_norm, l0_ssd_in_proj, l0_ssd_conv_w, l0_ssd_conv_b,
                    l0_ssd_dt_bias, l0_ssd_a_log, l0_ssd_d, l0_ssd_norm_w, l0_ssd_out_proj)
    xf = _ffn(xf, l0_ffn_norm, l0_ffn_w_gate, l0_ffn_w_up, l0_ffn_w_down)
    xf = _gla_mixer(xf, batch, l1_mixer_norm, l1_gla_in_proj, l1_gla_gk_up, l1_gla_gk_bias,
                    l1_gla_norm_w, l1_gla_out_proj)
    xf = _ffn(xf, l1_ffn_norm, l1_ffn_w_gate, l1_ffn_w_up, l1_ffn_w_down)
    return _rmsnorm(xf, final_norm, NORM_EPS, x.dtype).reshape(batch, seqlen, d)
```

```python
import functools

import jax
import jax.numpy as jnp
from jax import lax
from jax.experimental import pallas as pl
from jax.experimental.pallas import tpu as pltpu

F32 = jnp.float32
BF16 = jnp.bfloat16

NORM_EPS = 1e-6
INNER_NORM_EPS = 1e-5
SSD_D_STATE = 128
SSD_CHUNK = 256
GLA_CHUNK = 64
GLA_GATE_NORMALIZER = 16.0
LOG2_E = 1.4426950408889634

V7X_LANES = 128
V7X_F32_SUBLANES = 8
V7X_BF16_SUBLANES = 16
V7X_VMEM_BYTES = 64 * 1024 * 1024
VMEM_LIMIT_BYTES = V7X_VMEM_BYTES - 8 * 1024 * 1024
VMEM_LIMIT_WIDE_BYTES = V7X_VMEM_BYTES - 1 * 1024 * 1024

MM_RES_VMEM_BUDGET = 46 * 1024 * 1024
MM_RES_DOUBLE_W_BUDGET = 56 * 1024 * 1024
RES_ROW_TILES = (2048, 1024, 512)
RES_COL_TILES = (1024, 512, 256)
WCAST_TILE_M = 2048
WCAST_TILE_N = 512
GATEUP_TILE_N = 256
ROW_TILE = 256
NORM_ROW_TILE = 512
GLA_ROW_BLOCK = 256
GLA_HEADS_PER_STEP = 8


def _params(*semantics, vmem_limit_bytes=VMEM_LIMIT_BYTES):
    return pltpu.CompilerParams(dimension_semantics=semantics, vmem_limit_bytes=vmem_limit_bytes)


def _tile(dim, pref, align=V7X_LANES):
    if dim <= pref:
        return dim
    t = (pref // align) * align
    while t >= align:
        if dim % t == 0:
            return t
        t -= align
    return dim


def _silu(x):
    h = 0.5 * x
    return h + h * jnp.tanh(h)


def _softplus(x):
    return jnp.maximum(x, 0.0) + jnp.log1p(jnp.exp(-jnp.abs(x)))


def _pieces3(a):
    hi = a.astype(BF16).astype(F32)
    mid = (a - hi).astype(BF16).astype(F32)
    return hi, mid, a - hi - mid


def _split3_cols(a):
    return jnp.concatenate(_pieces3(a), axis=1).astype(BF16)


def _split3_rows(a):
    return jnp.concatenate(_pieces3(a), axis=0).astype(BF16)


def _rms_normalize(x, w, eps):
    ms = jnp.mean(x * x, axis=-1, keepdims=True)
    return x * lax.rsqrt(ms + eps) * w


def _rmsnorm_kernel(x_ref, w_ref, o_ref, *, eps):
    o_ref[...] = _rms_normalize(x_ref[...], w_ref[...], eps).astype(o_ref.dtype)


def _rmsnorm(x, w, eps, out_dtype):
    t, d = x.shape
    tr = _tile(t, NORM_ROW_TILE, V7X_F32_SUBLANES)
    return pl.pallas_call(
        functools.partial(_rmsnorm_kernel, eps=eps),
        grid=(t // tr,),
        in_specs=[pl.BlockSpec((tr, d), lambda i: (i, 0)), pl.BlockSpec((1, d), lambda i: (0, 0))],
        out_specs=pl.BlockSpec((tr, d), lambda i: (i, 0)),
        out_shape=jax.ShapeDtypeStruct((t, d), out_dtype),
        compiler_params=_params("parallel"),
        name="rmsnorm",
    )(x, w.reshape(1, d).astype(F32))


def _mm_wcast_kernel(a_ref, w_ref, o_ref):
    w = w_ref[...].astype(BF16)
    o_ref[...] = jnp.dot(a_ref[...], w, preferred_element_type=F32).astype(o_ref.dtype)


def _matmul_wcast(a, w, n_cols, out_dtype):
    m, k = a.shape
    tm, tn = _tile(m, WCAST_TILE_M), _tile(n_cols, WCAST_TILE_N)
    return pl.pallas_call(
        _mm_wcast_kernel,
        grid=(m // tm, n_cols // tn),
        in_specs=[pl.BlockSpec((tm, k), lambda i, j: (i, 0)), pl.BlockSpec((k, tn), lambda i, j: (0, j))],
        out_specs=pl.BlockSpec((tm, tn), lambda i, j: (i, j)),
        out_shape=jax.ShapeDtypeStruct((m, n_cols), out_dtype),
        compiler_params=_params("parallel", "arbitrary", vmem_limit_bytes=VMEM_LIMIT_WIDE_BYTES),
        name="matmul_wcast",
    )(a, w)


def _mm_wcast_conv_kernel(a_ref, w_ref, cw_ref, cb_ref, o_ref, buf_ref, carry_ref, *, halo, tiles_per_seq):
    i, j = pl.program_id(0), pl.program_id(1)
    tm = a_ref.shape[0]
    kw = cw_ref.shape[0]

    @pl.when((i == 0) & (j == 0))
    def _():
        carry_ref[...] = jnp.zeros_like(carry_ref)

    acc = jnp.dot(a_ref[...], w_ref[...].astype(BF16), preferred_element_type=F32)
    buf_ref[0:halo, :] = jnp.where(lax.rem(i, tiles_per_seq) == 0, 0.0, carry_ref[j])
    buf_ref[halo:halo + tm, :] = acc
    carry_ref[j] = acc[tm - halo:, :]
    out = cb_ref[...] + acc * cw_ref[kw - 1:kw, :]
    for tap in range(kw - 1):
        off = halo - (kw - 1) + tap
        out = out + buf_ref[off:off + tm, :] * cw_ref[tap:tap + 1, :]
    o_ref[...] = _silu(out).astype(o_ref.dtype)


def _matmul_wcast_conv(a, w, col0, conv_w, conv_b, seqlen):
    m, k = a.shape
    c, kw = conv_w.shape
    halo = V7X_BF16_SUBLANES
    tm, tn = _tile(seqlen, WCAST_TILE_M, halo), _tile(c, WCAST_TILE_N)
    assert col0 % tn == 0 and seqlen % tm == 0 and kw - 1 <= halo
    return pl.pallas_call(
        functools.partial(_mm_wcast_conv_kernel, halo=halo, tiles_per_seq=seqlen // tm),
        grid=(m // tm, c // tn),
        in_specs=[pl.BlockSpec((tm, k), lambda i, j: (i, 0)),
                  pl.BlockSpec((k, tn), lambda i, j: (0, col0 // tn + j)),
                  pl.BlockSpec((kw, tn), lambda i, j: (0, j)),
                  pl.BlockSpec((1, tn), lambda i, j: (0, j))],
        out_specs=pl.BlockSpec((tm, tn), lambda i, j: (i, j)),
        out_shape=jax.ShapeDtypeStruct((m, c), BF16),
        scratch_shapes=[pltpu.VMEM((tm + halo, tn), F32), pltpu.VMEM((c // tn, halo, tn), F32)],
        compiler_params=_params("arbitrary", "arbitrary", vmem_limit_bytes=V7X_VMEM_BYTES),
        name="matmul_wcast_conv",
    )(a, w, conv_w.T.astype(F32), conv_b.reshape(1, c).astype(F32))


def _mm_res_kernel(a_ref, w_ref, r_ref, o_ref):
    o_ref[...] = r_ref[...] + jnp.dot(a_ref[...], w_ref[...], preferred_element_type=F32)


def _res_tiles(m, k, n):
    best = None
    for tm in RES_ROW_TILES:
        for tn in RES_COL_TILES:
            if m % tm or n % tn:
                continue
            need = k * tn * 2 + 2 * tm * k * 2 + 5 * tm * tn * 4
            if need <= MM_RES_VMEM_BUDGET and (best is None or tm * tn > best[0] * best[1]):
                best = (tm, tn)
    return best if best is not None else (_tile(m, ROW_TILE, V7X_F32_SUBLANES), _tile(n, V7X_LANES))


def _matmul_residual(a, w, res):
    m, k = a.shape
    n = w.shape[1]
    tm, tn = _res_tiles(m, k, n)
    double_w = 2 * k * tn * 2 + 2 * tm * k * 2 + 5 * tm * tn * 4 <= MM_RES_DOUBLE_W_BUDGET
    w_spec = pl.BlockSpec((k, tn), lambda j, i: (0, j), pipeline_mode=pl.Buffered(2 if double_w else 1))
    return pl.pallas_call(
        _mm_res_kernel,
        grid=(n // tn, m // tm),
        in_specs=[pl.BlockSpec((tm, k), lambda j, i: (i, 0)), w_spec,
                  pl.BlockSpec((tm, tn), lambda j, i: (i, j))],
        out_specs=pl.BlockSpec((tm, tn), lambda j, i: (i, j)),
        out_shape=jax.ShapeDtypeStruct((m, n), F32),
        compiler_params=_params("parallel", "arbitrary", vmem_limit_bytes=VMEM_LIMIT_WIDE_BYTES),
        name="matmul_residual",
    )(a, w, res)


def _gateup_kernel(h_ref, wg_ref, wu_ref, o_ref):
    h = h_ref[...]
    g = jnp.dot(h, wg_ref[...].astype(BF16), preferred_element_type=F32)
    u = jnp.dot(h, wu_ref[...].astype(BF16), preferred_element_type=F32)
    o_ref[...] = (_silu(g) * u).astype(o_ref.dtype)


def _ffn_gateup(h, wg, wu):
    m, k = h.shape
    n = wg.shape[1]
    tm, tn = _tile(m, WCAST_TILE_M), _tile(n, GATEUP_TILE_N)
    wspec = pl.BlockSpec((k, tn), lambda i, j: (0, j))
    return pl.pallas_call(
        _gateup_kernel,
        grid=(m // tm, n // tn),
        in_specs=[pl.BlockSpec((tm, k), lambda i, j: (i, 0)), wspec, wspec],
        out_specs=pl.BlockSpec((tm, tn), lambda i, j: (i, j)),
        out_shape=jax.ShapeDtypeStruct((m, n), BF16),
        compiler_params=_params("parallel", "arbitrary", vmem_limit_bytes=VMEM_LIMIT_WIDE_BYTES),
        name="ffn_gateup",
    )(h, wg, wu)


def _ffn(x, norm_w, w_gate, w_up, w_down):
    h = _rmsnorm(x, norm_w, NORM_EPS, BF16)
    return _matmul_residual(_ffn_gateup(h, w_gate, w_up), w_down.astype(BF16), x)


def _ssd_norm_dt_kernel(x_ref, nw_ref, w_ref, bias_ref, alog_ref, h_ref, dt_ref, cs_ref, cst_ref, *,
                        n_groups, eps):
    h = _rms_normalize(x_ref[...], nw_ref[...], eps).astype(BF16)
    h_ref[...] = h
    chunks, _, lc, r = dt_ref.shape
    nh = bias_ref.shape[1]
    raw = jnp.dot(h, w_ref[...].astype(BF16), preferred_element_type=F32)[:, :nh]
    dt = _softplus(raw + bias_ref[...])
    la = dt * (-LOG2_E * jnp.exp(alog_ref[...]))
    row = lax.broadcasted_iota(jnp.int32, (lc, lc), 0)
    col = lax.broadcasted_iota(jnp.int32, (lc, lc), 1)
    tril3 = jnp.concatenate([(row >= col).astype(BF16)] * 3, axis=1)
    for c in range(chunks):
        rows = slice(c * lc, (c + 1) * lc)
        cs = jnp.dot(tril3, _split3_rows(la[rows]), preferred_element_type=F32)
        cst = cs.T
        for g in range(n_groups):
            dt_ref[c, g] = dt[rows, g * r:(g + 1) * r]
            cs_ref[c, g] = cs[:, g * r:(g + 1) * r]
            cst_ref[c, g] = cst[g * r:(g + 1) * r, :]


def _ssd_norm_dt(x, norm_w, in_proj, col0, dt_bias, a_log, n_groups):
    t, d = x.shape
    nh = dt_bias.shape[0]
    r = nh // n_groups
    lc = SSD_CHUNK
    nblk = t // lc
    assert col0 % V7X_LANES == 0 and nh <= V7X_LANES
    cpb = NORM_ROW_TILE // lc if nblk % (NORM_ROW_TILE // lc) == 0 else 1
    rows = pl.BlockSpec((cpb * lc, d), lambda i: (i, 0))
    vec = pl.BlockSpec((1, nh), lambda i: (0, 0))
    tok = pl.BlockSpec((cpb, n_groups, lc, r), lambda i: (i, 0, 0, 0))
    head = pl.BlockSpec((cpb, n_groups, r, lc), lambda i: (i, 0, 0, 0))
    return pl.pallas_call(
        functools.partial(_ssd_norm_dt_kernel, n_groups=n_groups, eps=NORM_EPS),
        grid=(nblk // cpb,),
        in_specs=[rows, pl.BlockSpec((1, d), lambda i: (0, 0)),
                  pl.BlockSpec((d, V7X_LANES), lambda i: (0, col0 // V7X_LANES)), vec, vec],
        out_specs=[rows, tok, tok, head],
        out_shape=[jax.ShapeDtypeStruct((t, d), BF16),
                   jax.ShapeDtypeStruct((nblk, n_groups, lc, r), F32),
                   jax.ShapeDtypeStruct((nblk, n_groups, lc, r), F32),
                   jax.ShapeDtypeStruct((nblk, n_groups, r, lc), F32)],
        compiler_params=_params("parallel"),
        name="ssd_norm_dt",
    )(x, norm_w.reshape(1, d).astype(F32), in_proj,
      dt_bias.reshape(1, nh).astype(F32), a_log.reshape(1, nh).astype(F32))


def _ssd_scan_kernel(x_ref, b_ref, c_ref, z_ref, dt_ref, cs_ref, cst_ref, dskip_ref, nw_ref,
                     o_ref, state_ref, *, heads, head_dim, eps):
    @pl.when(pl.program_id(2) == 0)
    def _():
        state_ref[...] = jnp.zeros_like(state_ref)

    lc = x_ref.shape[0]
    half = lc // 2
    gw = heads * head_dim
    pair = 2 * head_dim
    x = x_ref[...].astype(F32)
    bm = b_ref[...]
    cm = c_ref[...]
    dt = dt_ref[0, 0]
    cs = cs_ref[0, 0]
    cst = cst_ref[0, 0]

    e_row = lax.broadcasted_iota(jnp.int32, (3 * heads, gw), 0) % heads
    e_col = lax.broadcasted_iota(jnp.int32, (3 * heads, gw), 1) // head_dim
    expand_m = (e_row == e_col).astype(BF16)
    expand = lambda a: jnp.dot(_split3_cols(a), expand_m, preferred_element_type=F32)

    cs_end = cs[lc - 1:lc, :]
    dt_e = expand(dt)
    ecs_e = expand(jnp.exp2(cs))
    dtd_e = expand(dt * jnp.exp2(cs_end - cs))

    xdt = (x * dt_e).astype(BF16)
    cb = lax.dot_general(cm, bm, (((1,), (1,)), ((), ())), preferred_element_type=F32)
    cb_tl, cb_bl, cb_br = cb[:half, :half], cb[half:, :half], cb[half:, half:]
    tri = lax.broadcasted_iota(jnp.int32, (half, half), 0) >= lax.broadcasted_iota(jnp.int32, (half, half), 1)
    first_half = lax.broadcasted_iota(jnp.int32, (lc, pair), 1) < head_dim
    zero_tr = jnp.zeros((half, half), F32)

    def masked_cb(r):
        col, row = cs[:, r:r + 1], cst[r:r + 1, :]
        m_tl = cb_tl * jnp.exp2(jnp.where(tri, col[:half] - row[:, :half], -jnp.inf))
        m_bl = cb_bl * jnp.exp2(col[half:] - row[:, :half])
        m_br = cb_br * jnp.exp2(jnp.where(tri, col[half:] - row[:, half:], -jnp.inf))
        top = jnp.concatenate([m_tl, zero_tr], axis=1)
        return jnp.concatenate([top, jnp.concatenate([m_bl, m_br], axis=1)], axis=0).astype(BF16)

    ys = []
    for p in range(heads // 2):
        xp = xdt[:, p * pair:(p + 1) * pair]
        xa = jnp.where(first_half, xp, jnp.zeros_like(xp))
        xb = jnp.where(first_half, jnp.zeros_like(xp), xp)
        ys.append(jnp.dot(masked_cb(2 * p), xa, preferred_element_type=F32)
                  + jnp.dot(masked_cb(2 * p + 1), xb, preferred_element_type=F32))
    y = jnp.concatenate(ys, axis=1) if len(ys) > 1 else ys[0]

    state = state_ref[...]
    y = y + jnp.dot(cm, state.astype(BF16), preferred_element_type=F32) * ecs_e
    state_ref[...] = state * ecs_e[lc - 1:lc, :] + lax.dot_general(
        bm, (x * dtd_e).astype(BF16), (((0,), (0,)), ((), ())), preferred_element_type=F32)

    y = y + x * dskip_ref[...]
    y = y * _silu(z_ref[...].astype(F32))
    ms = jnp.mean(y * y, axis=-1, keepdims=True)
    o_ref[...] = (y * lax.rsqrt(ms + eps) * nw_ref[...]).astype(o_ref.dtype)


def _ssd_scan(z, xbc, dt, cs, cst, d_skip, norm_w, batch, d_inner, n_groups, heads):
    t = z.shape[0]
    lc = SSD_CHUNK
    nc = t // batch // lc
    n = SSD_D_STATE
    gw = d_inner // n_groups
    head_dim = gw // heads
    assert heads % 2 == 0 and 2 * head_dim == V7X_LANES
    row = lambda b, g, c: b * nc + c
    small = lambda shape: pl.BlockSpec(shape, lambda b, g, c: (row(b, g, c), g, 0, 0))
    return pl.pallas_call(
        functools.partial(_ssd_scan_kernel, heads=heads, head_dim=head_dim, eps=INNER_NORM_EPS),
        grid=(batch, n_groups, nc),
        in_specs=[
            pl.BlockSpec((lc, gw), lambda b, g, c: (row(b, g, c), g)),
            pl.BlockSpec((lc, n), lambda b, g, c: (row(b, g, c), d_inner // n + g)),
            pl.BlockSpec((lc, n), lambda b, g, c: (row(b, g, c), d_inner // n + n_groups + g)),
            pl.BlockSpec((lc, gw), lambda b, g, c: (row(b, g, c), g)),
            small((1, 1, lc, heads)), small((1, 1, lc, heads)), small((1, 1, heads, lc)),
            pl.BlockSpec((1, gw), lambda b, g, c: (0, g)),
            pl.BlockSpec((1, gw), lambda b, g, c: (0, g)),
        ],
        out_specs=pl.BlockSpec((lc, gw), lambda b, g, c: (row(b, g, c), g)),
        out_shape=jax.ShapeDtypeStruct((t, d_inner), BF16),
        scratch_shapes=[pltpu.VMEM((n, gw), F32)],
        compiler_params=_params("parallel", "parallel", "arbitrary"),
        name="ssd_scan",
    )(xbc, xbc, xbc, z, dt, cs, cst,
      jnp.repeat(d_skip.astype(F32), head_dim).reshape(1, d_inner),
      norm_w.reshape(1, d_inner).astype(F32))


def _ssd_mixer(x, batch, norm_w, in_proj, conv_w, conv_b, dt_bias, a_log, d_skip, inner_norm_w, out_proj):
    d_inner = out_proj.shape[0]
    conv_dim = conv_w.shape[0]
    n_heads = dt_bias.shape[0]
    n_groups = (conv_dim - d_inner) // 2 // SSD_D_STATE
    main = d_inner + conv_dim
    h, dt, cs, cst = _ssd_norm_dt(x, norm_w, in_proj, main, dt_bias, a_log, n_groups)
    z = _matmul_wcast(h, in_proj, d_inner, BF16)
    xbc = _matmul_wcast_conv(h, in_proj, d_inner, conv_w, conv_b, x.shape[0] // batch)
    y = _ssd_scan(z, xbc, dt, cs, cst, d_skip, inner_norm_w, batch, d_inner, n_groups,
                  n_heads // n_groups)
    return _matmul_residual(y, out_proj.astype(BF16), x)


def _gla_norm_gate_kernel(x_ref, nw_ref, wlow_ref, up_ref, bias_ref, h_ref, o_ref, *, eps):
    h = _rms_normalize(x_ref[...], nw_ref[...], eps).astype(BF16)
    h_ref[...] = h
    rank = up_ref.shape[0]
    low = jnp.dot(h, wlow_ref[...].astype(BF16), preferred_element_type=F32)[:, :rank]
    l_hi, l_mid, l_lo = _pieces3(low)
    u_hi, u_mid, u_lo = _pieces3(up_ref[...])
    lhs = jnp.concatenate([l_hi, l_mid, l_lo, l_hi, l_mid, l_hi], axis=1).astype(BF16)
    rhs = jnp.concatenate([u_hi, u_hi, u_hi, u_mid, u_mid, u_lo], axis=0).astype(BF16)
    x = jnp.dot(lhs, rhs, preferred_element_type=F32) + bias_ref[...]
    o_ref[...] = (jnp.minimum(x, 0.0) - jnp.log1p(jnp.exp(-jnp.abs(x)))) * (1.0 / GLA_GATE_NORMALIZER)


def _gla_norm_gate(x, norm_w, w_low, gk_up, gk_bias):
    t, d = x.shape
    rank, kd = gk_up.shape
    pad = (-rank) % V7X_LANES
    w_low = jnp.pad(w_low.astype(F32), ((0, 0), (0, pad)))
    gk_up = gk_up.astype(F32)
    tr = _tile(t, NORM_ROW_TILE, V7X_F32_SUBLANES)
    rows = pl.BlockSpec((tr, d), lambda i: (i, 0))
    return pl.pallas_call(
        functools.partial(_gla_norm_gate_kernel, eps=NORM_EPS),
        grid=(t // tr,),
        in_specs=[rows, pl.BlockSpec((1, d), lambda i: (0, 0)),
                  pl.BlockSpec((d, rank + pad), lambda i: (0, 0)),
                  pl.BlockSpec((rank, kd), lambda i: (0, 0)),
                  pl.BlockSpec((1, kd), lambda i: (0, 0))],
        out_specs=[rows, pl.BlockSpec((tr, kd), lambda i: (i, 0))],
        out_shape=[jax.ShapeDtypeStruct((t, d), BF16), jax.ShapeDtypeStruct((t, kd), F32)],
        compiler_params=_params("parallel"),
        name="gla_norm_gate",
    )(x, norm_w.reshape(1, d).astype(F32), w_low, gk_up, gk_bias.reshape(1, kd).astype(F32))


def _gla_scan_kernel(q_ref, k_ref, v_ref, g_ref, go_ref, nw_ref, o_ref, state_ref, *,
                     heads, chunk, scale, eps):
    @pl.when(pl.program_id(2) == 0)
    def _():
        state_ref[...] = jnp.zeros_like(state_ref)

    rows = q_ref.shape[0]
    nsub = rows // chunk
    hk = q_ref.shape[1] // heads
    hv = v_ref.shape[1] // heads
    shift = chunk.bit_length() - 1
    ri = lax.broadcasted_iota(jnp.int32, (rows, rows), 0)
    ci = lax.broadcasted_iota(jnp.int32, (rows, rows), 1)
    causal = ri >= ci
    in_chunk_causal = causal & (lax.shift_right_logical(ri, shift) == lax.shift_right_logical(ci, shift))
    cumsum_m = jnp.concatenate([in_chunk_causal.astype(BF16)] * 3, axis=1)
    blk = lambda a, s: a[s * chunk:(s + 1) * chunk]
    q = q_ref[...].astype(F32) * scale
    k = k_ref[...].astype(F32)
    cs = jnp.dot(cumsum_m, _split3_rows(g_ref[...]), preferred_element_type=F32)
    tot = [cs[(s + 1) * chunk - 1:(s + 1) * chunk, :] for s in range(nsub)]
    before = [jnp.zeros_like(tot[0])]
    for s in range(nsub):
        before.append(before[s] + tot[s])
    total = before[nsub]
    q_dec = q * jnp.exp(cs)
    q_dec_bf = q_dec.astype(BF16)
    k_inv = (k * jnp.exp(-cs)).astype(BF16)
    k_end = [blk(k, s) * jnp.exp(tot[s] - blk(cs, s)) for s in range(nsub)]
    q_in = jnp.concatenate([blk(q_dec, s) * jnp.exp(before[s]) for s in range(nsub)], axis=0).astype(BF16)
    k_out = jnp.concatenate([k_end[s] * jnp.exp(total - before[s + 1]) for s in range(nsub)],
                            axis=0).astype(BF16)
    keys = []
    for s in range(nsub):
        rows_s = [(k_end[j] * jnp.exp(before[s] - before[j + 1])).astype(BF16) for j in range(s)]
        rows_s.append(blk(k_inv, s))
        rows_s += [jnp.zeros((chunk, heads * hk), BF16)] * (nsub - 1 - s)
        keys.append(jnp.concatenate(rows_s, axis=0))
    total_col = jnp.exp(jnp.broadcast_to(total, (V7X_LANES, heads * hk)).T[:, 0:1])
    for hh in range(heads):
        kc = slice(hh * hk, (hh + 1) * hk)
        vc = slice(hh * hv, (hh + 1) * hv)
        v = v_ref[:, vc]
        score_rows = [lax.dot_general(blk(q_dec_bf, s)[:, kc], keys[s][:, kc], (((1,), (1,)), ((), ())),
                                      preferred_element_type=F32) for s in range(nsub)]
        scores = jnp.where(causal, jnp.concatenate(score_rows, axis=0), 0.0).astype(BF16)
        state = state_ref[hh]
        o = (jnp.dot(scores, v, preferred_element_type=F32)
             + jnp.dot(q_in[:, kc], state.astype(BF16), preferred_element_type=F32))
        state_ref[hh] = state * total_col[kc] + lax.dot_general(
            k_out[:, kc], v, (((0,), (0,)), ((), ())), preferred_element_type=F32)
        ms = jnp.mean(o * o, axis=-1, keepdims=True)
        o = o * lax.rsqrt(ms + eps) * nw_ref[...]
        o_ref[:, vc] = (o * _silu(go_ref[:, vc].astype(F32))).astype(o_ref.dtype)


def _gla_scan(qkvg, log_g, norm_w, batch, key_dim, value_dim, n_heads):
    t = qkvg.shape[0]
    hk, hv = key_dim // n_heads, value_dim // n_heads
    seqlen = t // batch
    rb = _tile(seqlen, GLA_ROW_BLOCK, GLA_CHUNK)
    nb = seqlen // rb
    hpb = GLA_HEADS_PER_STEP if n_heads % GLA_HEADS_PER_STEP == 0 else 1
    kw, vw = hpb * hk, hpb * hv
    assert rb % GLA_CHUNK == 0 and (2 * key_dim) % vw == 0 and value_dim % vw == 0
    row = lambda b, h, i: b * nb + i
    k0 = key_dim // kw
    v0 = 2 * key_dim // vw
    g0 = (2 * key_dim + value_dim) // vw
    return pl.pallas_call(
        functools.partial(_gla_scan_kernel, heads=hpb, chunk=GLA_CHUNK, scale=hk ** -0.5,
                          eps=INNER_NORM_EPS),
        grid=(batch, n_heads // hpb, nb),
        in_specs=[
            pl.BlockSpec((rb, kw), lambda b, h, i: (row(b, h, i), h)),
            pl.BlockSpec((rb, kw), lambda b, h, i: (row(b, h, i), k0 + h)),
            pl.BlockSpec((rb, vw), lambda b, h, i: (row(b, h, i), v0 + h)),
            pl.BlockSpec((rb, kw), lambda b, h, i: (row(b, h, i), h)),
            pl.BlockSpec((rb, vw), lambda b, h, i: (row(b, h, i), g0 + h)),
            pl.BlockSpec((1, hv), lambda b, h, i: (0, 0)),
        ],
        out_specs=pl.BlockSpec((rb, vw), lambda b, h, i: (row(b, h, i), h)),
        out_shape=jax.ShapeDtypeStruct((t, value_dim), BF16),
        scratch_shapes=[pltpu.VMEM((hpb, hk, hv), F32)],
        compiler_params=_params("parallel", "parallel", "arbitrary"),
        name="gla_scan",
    )(qkvg, qkvg, qkvg, log_g, qkvg, norm_w.reshape(1, hv).astype(F32))


def _gla_mixer(x, batch, norm_w, in_proj, gk_up, gk_bias, inner_norm_w, out_proj):
    key_dim = gk_up.shape[1]
    value_dim = out_proj.shape[0]
    n_heads = value_dim // inner_norm_w.shape[0]
    main = 2 * key_dim + 2 * value_dim
    h, log_g = _gla_norm_gate(x, norm_w, in_proj[:, main:], gk_up, gk_bias)
    qkvg = _matmul_wcast(h, in_proj, main, BF16)
    o = _gla_scan(qkvg, log_g, inner_norm_w, batch, key_dim, value_dim, n_heads)
    return _matmul_residual(o, out_proj.astype(BF16), x)


def kernel(x, l0_mixer_norm, l0_ssd_in_proj, l0_ssd_conv_w, l0_ssd_conv_b, l0_ssd_dt_bias, l0_ssd_a_log, l0_ssd_d, l0_ssd_norm_w, l0_ssd_out_proj, l0_ffn_norm, l0_ffn_w_gate, l0_ffn_w_up, l0_ffn_w_down, l1_mixer_norm, l1_gla_in_proj, l1_gla_gk_up, l1_gla_gk_bias, l1_gla_norm_w, l1_gla_out_proj, l1_ffn_norm, l1_ffn_w_gate, l1_ffn_w_up, l1_ffn_w_down, final_norm):
    batch, seqlen, d = x.shape
    xf = x.reshape(batch * seqlen, d).astype(F32)
    xf = _ssd_mixer(xf, batch, l0_mixer_norm, l0_ssd_in_proj, l0_ssd_conv_w, l0_ssd_conv_b,
                    l0_ssd_dt_bias, l0_ssd_a_log, l0_ssd_d, l0_ssd_norm_w, l0_ssd_out_proj)
    xf = _ffn(xf, l0_ffn_norm, l0_ffn_w_gate, l0_ffn_w_up, l0_ffn_w_down)
    xf = _gla_mixer(xf, batch, l1_mixer_norm, l1_gla_in_proj, l1_gla_gk_up, l1_gla_gk_bias,
                    l1_gla_norm_w, l1_gla_out_proj)
    xf = _ffn(xf, l1_ffn_norm, l1_ffn_w_gate, l1_ffn_w_up, l1_ffn_w_down)
    return _rmsnorm(xf, final_norm, NORM_EPS, x.dtype).reshape(batch, seqlen, d)
```

```python
import functools

import jax
import jax.numpy as jnp
from jax import lax
from jax.experimental import pallas as pl
from jax.experimental.pallas import tpu as pltpu

F32 = jnp.float32
BF16 = jnp.bfloat16

NORM_EPS = 1e-6
INNER_NORM_EPS = 1e-5
SSD_D_STATE = 128
SSD_CHUNK = 256
GLA_CHUNK = 64
GLA_GATE_NORMALIZER = 16.0
LOG2_E = 1.4426950408889634

V7X_LANES = 128
V7X_F32_SUBLANES = 8
V7X_BF16_SUBLANES = 16
V7X_VMEM_BYTES = 64 * 1024 * 1024
VMEM_LIMIT_BYTES = V7X_VMEM_BYTES - 8 * 1024 * 1024
VMEM_LIMIT_WIDE_BYTES = V7X_VMEM_BYTES - 1 * 1024 * 1024

MM_RES_VMEM_BUDGET = 46 * 1024 * 1024
MM_RES_DOUBLE_W_BUDGET = 56 * 1024 * 1024
RES_ROW_TILES = (2048, 1024, 512)
RES_COL_TILES = (1024, 512, 256)
WCAST_TILE_M = 2048
WCAST_TILE_N = 512
GATEUP_TILE_N = 256
ROW_TILE = 256
NORM_ROW_TILE = 512
GLA_ROW_BLOCK = 256
GLA_HEADS_PER_STEP = 8


def _params(*semantics, vmem_limit_bytes=VMEM_LIMIT_BYTES):
    return pltpu.CompilerParams(dimension_semantics=semantics, vmem_limit_bytes=vmem_limit_bytes)


def _tile(dim, pref, align=V7X_LANES):
    if dim <= pref:
        return dim
    t = (pref // align) * align
    while t >= align:
        if dim % t == 0:
            return t
        t -= align
    return dim


def _silu(x):
    h = 0.5 * x
    return h + h * jnp.tanh(h)


def _softplus(x):
    return jnp.maximum(x, 0.0) + jnp.log1p(jnp.exp(-jnp.abs(x)))


def _pieces3(a):
    hi = a.astype(BF16).astype(F32)
    mid = (a - hi).astype(BF16).astype(F32)
    return hi, mid, a - hi - mid


def _split3_cols(a):
    return jnp.concatenate(_pieces3(a), axis=1).astype(BF16)


def _split3_rows(a):
    return jnp.concatenate(_pieces3(a), axis=0).astype(BF16)


def _rms_normalize(x, w, eps):
    ms = jnp.mean(x * x, axis=-1, keepdims=True)
    return x * lax.rsqrt(ms + eps) * w


def _rmsnorm_kernel(x_ref, w_ref, o_ref, *, eps):
    o_ref[...] = _rms_normalize(x_ref[...], w_ref[...], eps).astype(o_ref.dtype)


def _rmsnorm(x, w, eps, out_dtype):
    t, d = x.shape
    tr = _tile(t, NORM_ROW_TILE, V7X_F32_SUBLANES)
    return pl.pallas_call(
        functools.partial(_rmsnorm_kernel, eps=eps),
        grid=(t // tr,),
        in_specs=[pl.BlockSpec((tr, d), lambda i: (i, 0)), pl.BlockSpec((1, d), lambda i: (0, 0))],
        out_specs=pl.BlockSpec((tr, d), lambda i: (i, 0)),
        out_shape=jax.ShapeDtypeStruct((t, d), out_dtype),
        compiler_params=_params("parallel"),
        name="rmsnorm",
    )(x, w.reshape(1, d).astype(F32))


def _mm_wcast_kernel(a_ref, w_ref, o_ref):
    w = w_ref[...].astype(BF16)
    o_ref[...] = jnp.dot(a_ref[...], w, preferred_element_type=F32).astype(o_ref.dtype)


def _matmul_wcast(a, w, n_cols, out_dtype):
    m, k = a.shape
    tm, tn = _tile(m, WCAST_TILE_M), _tile(n_cols, WCAST_TILE_N)
    return pl.pallas_call(
        _mm_wcast_kernel,
        grid=(m // tm, n_cols // tn),
        in_specs=[pl.BlockSpec((tm, k), lambda i, j: (i, 0)), pl.BlockSpec((k, tn), lambda i, j: (0, j))],
        out_specs=pl.BlockSpec((tm, tn), lambda i, j: (i, j)),
        out_shape=jax.ShapeDtypeStruct((m, n_cols), out_dtype),
        compiler_params=_params("parallel", "arbitrary", vmem_limit_bytes=VMEM_LIMIT_WIDE_BYTES),
        name="matmul_wcast",
    )(a, w)


def _mm_wcast_conv_kernel(a_ref, w_ref, cw_ref, cb_ref, o_ref, buf_ref, carry_ref, *, halo, tiles_per_seq):
    i, j = pl.program_id(0), pl.program_id(1)
    tm = a_ref.shape[0]
    kw = cw_ref.shape[0]

    @pl.when((i == 0) & (j == 0))
    def _():
        carry_ref[...] = jnp.zeros_like(carry_ref)

    acc = jnp.dot(a_ref[...], w_ref[...].astype(BF16), preferred_element_type=F32)
    buf_ref[0:halo, :] = jnp.where(lax.rem(i, tiles_per_seq) == 0, 0.0, carry_ref[j])
    buf_ref[halo:halo + tm, :] = acc
    carry_ref[j] = acc[tm - halo:, :]
    out = cb_ref[...] + acc * cw_ref[kw - 1:kw, :]
    for tap in range(kw - 1):
        off = halo - (kw - 1) + tap
        out = out + buf_ref[off:off + tm, :] * cw_ref[tap:tap + 1, :]
    o_ref[...] = _silu(out).astype(o_ref.dtype)


def _matmul_wcast_conv(a, w, col0, conv_w, conv_b, seqlen):
    m, k = a.shape
    c, kw = conv_w.shape
    halo = V7X_BF16_SUBLANES
    tm, tn = _tile(seqlen, WCAST_TILE_M, halo), _tile(c, WCAST_TILE_N)
    assert col0 % tn == 0 and seqlen % tm == 0 and kw - 1 <= halo
    return pl.pallas_call(
        functools.partial(_mm_wcast_conv_kernel, halo=halo, tiles_per_seq=seqlen // tm),
        grid=(m // tm, c // tn),
        in_specs=[pl.BlockSpec((tm, k), lambda i, j: (i, 0)),
                  pl.BlockSpec((k, tn), lambda i, j: (0, col0 // tn + j)),
                  pl.BlockSpec((kw, tn), lambda i, j: (0, j)),
                  pl.BlockSpec((1, tn), lambda i, j: (0, j))],
        out_specs=pl.BlockSpec((tm, tn), lambda i, j: (i, j)),
        out_shape=jax.ShapeDtypeStruct((m, c), BF16),
        scratch_shapes=[pltpu.VMEM((tm + halo, tn), F32), pltpu.VMEM((c // tn, halo, tn), F32)],
        compiler_params=_params("arbitrary", "arbitrary", vmem_limit_bytes=V7X_VMEM_BYTES),
        name="matmul_wcast_conv",
    )(a, w, conv_w.T.astype(F32), conv_b.reshape(1, c).astype(F32))


def _mm_res_kernel(a_ref, w_ref, r_ref, o_ref):
    o_ref[...] = r_ref[...] + jnp.dot(a_ref[...], w_ref[...], preferred_element_type=F32)


def _res_tiles(m, k, n):
    best = None
    for tm in RES_ROW_TILES:
        for tn in RES_COL_TILES:
            if m % tm or n % tn:
                continue
            need = k * tn * 2 + 2 * tm * k * 2 + 5 * tm * tn * 4
            if need <= MM_RES_VMEM_BUDGET and (best is None or tm * tn > best[0] * best[1]):
                best = (tm, tn)
    return best if best is not None else (_tile(m, ROW_TILE, V7X_F32_SUBLANES), _tile(n, V7X_LANES))


def _matmul_residual(a, w, res):
    m, k = a.shape
    n = w.shape[1]
    tm, tn = _res_tiles(m, k, n)
    double_w = 2 * k * tn * 2 + 2 * tm * k * 2 + 5 * tm * tn * 4 <= MM_RES_DOUBLE_W_BUDGET
    w_spec = pl.BlockSpec((k, tn), lambda j, i: (0, j), pipeline_mode=pl.Buffered(2 if double_w else 1))
    return pl.pallas_call(
        _mm_res_kernel,
        grid=(n // tn, m // tm),
        in_specs=[pl.BlockSpec((tm, k), lambda j, i: (i, 0)), w_spec,
                  pl.BlockSpec((tm, tn), lambda j, i: (i, j))],
        out_specs=pl.BlockSpec((tm, tn), lambda j, i: (i, j)),
        out_shape=jax.ShapeDtypeStruct((m, n), F32),
        compiler_params=_params("parallel", "arbitrary", vmem_limit_bytes=VMEM_LIMIT_WIDE_BYTES),
        name="matmul_residual",
    )(a, w, res)


def _gateup_kernel(h_ref, wg_ref, wu_ref, o_ref):
    h = h_ref[...]
    g = jnp.dot(h, wg_ref[...].astype(BF16), preferred_element_type=F32)
    u = jnp.dot(h, wu_ref[...].astype(BF16), preferred_element_type=F32)
    o_ref[...] = (_silu(g) * u).astype(o_ref.dtype)


def _ffn_gateup(h, wg, wu):
    m, k = h.shape
    n = wg.shape[1]
    tm, tn = _tile(m, WCAST_TILE_M), _tile(n, GATEUP_TILE_N)
    wspec = pl.BlockSpec((k, tn), lambda i, j: (0, j))
    return pl.pallas_call(
        _gateup_kernel,
        grid=(m // tm, n // tn),
        in_specs=[pl.BlockSpec((tm, k), lambda i, j: (i, 0)), wspec, wspec],
        out_specs=pl.BlockSpec((tm, tn), lambda i, j: (i, j)),
        out_shape=jax.ShapeDtypeStruct((m, n), BF16),
        compiler_params=_params("parallel", "arbitrary", vmem_limit_bytes=VMEM_LIMIT_WIDE_BYTES),
        name="ffn_gateup",
    )(h, wg, wu)


def _ffn(x, norm_w, w_gate, w_up, w_down):
    h = _rmsnorm(x, norm_w, NORM_EPS, BF16)
    return _matmul_residual(_ffn_gateup(h, w_gate, w_up), w_down.astype(BF16), x)


def _ssd_norm_dt_kernel(x_ref, nw_ref, w_ref, bias_ref, alog_ref, h_ref, dt_ref, cs_ref, cst_ref, *,
                        n_groups, eps):
    h = _rms_normalize(x_ref[...], nw_ref[...], eps).astype(BF16)
    h_ref[...] = h
    chunks, _, lc, r = dt_ref.shape
    nh = bias_ref.shape[1]
    raw = jnp.dot(h, w_ref[...].astype(BF16), preferred_element_type=F32)[:, :nh]
    dt = _softplus(raw + bias_ref[...])
    la = dt * (-LOG2_E * jnp.exp(alog_ref[...]))
    row = lax.broadcasted_iota(jnp.int32, (lc, lc), 0)
    col = lax.broadcasted_iota(jnp.int32, (lc, lc), 1)
    tril3 = jnp.concatenate([(row >= col).astype(BF16)] * 3, axis=1)
    for c in range(chunks):
        rows = slice(c * lc, (c + 1) * lc)
        cs = jnp.dot(tril3, _split3_rows(la[rows]), preferred_element_type=F32)
        cst = cs.T
        for g in range(n_groups):
            dt_ref[c, g] = dt[rows, g * r:(g + 1) * r]
            cs_ref[c, g] = cs[:, g * r:(g + 1) * r]
            cst_ref[c, g] = cst[g * r:(g + 1) * r, :]


def _ssd_norm_dt(x, norm_w, in_proj, col0, dt_bias, a_log, n_groups):
    t, d = x.shape
    nh = dt_bias.shape[0]
    r = nh // n_groups
    lc = SSD_CHUNK
    nblk = t // lc
    assert col0 % V7X_LANES == 0 and nh <= V7X_LANES
    cpb = NORM_ROW_TILE // lc if nblk % (NORM_ROW_TILE // lc) == 0 else 1
    rows = pl.BlockSpec((cpb * lc, d), lambda i: (i, 0))
    vec = pl.BlockSpec((1, nh), lambda i: (0, 0))
    tok = pl.BlockSpec((cpb, n_groups, lc, r), lambda i: (i, 0, 0, 0))
    head = pl.BlockSpec((cpb, n_groups, r, lc), lambda i: (i, 0, 0, 0))
    return pl.pallas_call(
        functools.partial(_ssd_norm_dt_kernel, n_groups=n_groups, eps=NORM_EPS),
        grid=(nblk // cpb,),
        in_specs=[rows, pl.BlockSpec((1, d), lambda i: (0, 0)),
                  pl.BlockSpec((d, V7X_LANES), lambda i: (0, col0 // V7X_LANES)), vec, vec],
        out_specs=[rows, tok, tok, head],
        out_shape=[jax.ShapeDtypeStruct((t, d), BF16),
                   jax.ShapeDtypeStruct((nblk, n_groups, lc, r), F32),
                   jax.ShapeDtypeStruct((nblk, n_groups, lc, r), F32),
                   jax.ShapeDtypeStruct((nblk, n_groups, r, lc), F32)],
        compiler_params=_params("parallel"),
        name="ssd_norm_dt",
    )(x, norm_w.reshape(1, d).astype(F32), in_proj,
      dt_bias.reshape(1, nh).astype(F32), a_log.reshape(1, nh).astype(F32))


def _ssd_scan_kernel(x_ref, b_ref, c_ref, z_ref, dt_ref, cs_ref, cst_ref, dskip_ref, nw_ref,
                     o_ref, state_ref, *, heads, head_dim, eps):
    @pl.when(pl.program_id(2) == 0)
    def _():
        state_ref[...] = jnp.zeros_like(state_ref)

    lc = x_ref.shape[0]
    half = lc // 2
    gw = heads * head_dim
    pair = 2 * head_dim
    x = x_ref[...].astype(F32)
    bm = b_ref[...]
    cm = c_ref[...]
    dt = dt_ref[0, 0]
    cs = cs_ref[0, 0]
    cst = cst_ref[0, 0]

    e_row = lax.broadcasted_iota(jnp.int32, (3 * heads, gw), 0) % heads
    e_col = lax.broadcasted_iota(jnp.int32, (3 * heads, gw), 1) // head_dim
    expand_m = (e_row == e_col).astype(BF16)
    expand = lambda a: jnp.dot(_split3_cols(a), expand_m, preferred_element_type=F32)

    cs_end = cs[lc - 1:lc, :]
    dt_e = expand(dt)
    ecs_e = expand(jnp.exp2(cs))
    dtd_e = expand(dt * jnp.exp2(cs_end - cs))

    xdt = (x * dt_e).astype(BF16)
    cb = lax.dot_general(cm, bm, (((1,), (1,)), ((), ())), preferred_element_type=F32)
    cb_tl, cb_bl, cb_br = cb[:half, :half], cb[half:, :half], cb[half:, half:]
    tri = lax.broadcasted_iota(jnp.int32, (half, half), 0) >= lax.broadcasted_iota(jnp.int32, (half, half), 1)
    first_half = lax.broadcasted_iota(jnp.int32, (lc, pair), 1) < head_dim
    zero_tr = jnp.zeros((half, half), F32)

    def masked_cb(r):
        col, row = cs[:, r:r + 1], cst[r:r + 1, :]
        m_tl = cb_tl * jnp.exp2(jnp.where(tri, col[:half] - row[:, :half], -jnp.inf))
        m_bl = cb_bl * jnp.exp2(col[half:] - row[:, :half])
        m_br = cb_br * jnp.exp2(jnp.where(tri, col[half:] - row[:, half:], -jnp.inf))
        top = jnp.concatenate([m_tl, zero_tr], axis=1)
        return jnp.concatenate([top, jnp.concatenate([m_bl, m_br], axis=1)], axis=0).astype(BF16)

    state = state_ref[...]
    y_off = jnp.dot(cm, state.astype(BF16), preferred_element_type=F32) * ecs_e
    state_ref[...] = state * ecs_e[lc - 1:lc, :] + lax.dot_general(
        bm, (x * dtd_e).astype(BF16), (((0,), (0,)), ((), ())), preferred_element_type=F32)

    ys = []
    for p in range(heads // 2):
        cols = slice(p * pair, (p + 1) * pair)
        xp = xdt[:, cols]
        xa = jnp.where(first_half, xp, jnp.zeros_like(xp))
        xb = jnp.where(first_half, jnp.zeros_like(xp), xp)
        y = (jnp.dot(masked_cb(2 * p), xa, preferred_element_type=F32)
             + jnp.dot(masked_cb(2 * p + 1), xb, preferred_element_type=F32) + y_off[:, cols])
        y = y + x_ref[:, cols].astype(F32) * dskip_ref[:, cols]
        ys.append(y * _silu(z_ref[:, cols].astype(F32)))
    y = jnp.concatenate(ys, axis=1) if len(ys) > 1 else ys[0]
    ms = jnp.mean(y * y, axis=-1, keepdims=True)
    o_ref[...] = (y * lax.rsqrt(ms + eps) * nw_ref[...]).astype(o_ref.dtype)


def _ssd_scan(z, xbc, dt, cs, cst, d_skip, norm_w, batch, d_inner, n_groups, heads):
    t = z.shape[0]
    lc = SSD_CHUNK
    nc = t // batch // lc
    n = SSD_D_STATE
    gw = d_inner // n_groups
    head_dim = gw // heads
    assert heads % 2 == 0 and 2 * head_dim == V7X_LANES
    row = lambda b, g, c: b * nc + c
    small = lambda shape: pl.BlockSpec(shape, lambda b, g, c: (row(b, g, c), g, 0, 0))
    return pl.pallas_call(
        functools.partial(_ssd_scan_kernel, heads=heads, head_dim=head_dim, eps=INNER_NORM_EPS),
        grid=(batch, n_groups, nc),
        in_specs=[
            pl.BlockSpec((lc, gw), lambda b, g, c: (row(b, g, c), g)),
            pl.BlockSpec((lc, n), lambda b, g, c: (row(b, g, c), d_inner // n + g)),
            pl.BlockSpec((lc, n), lambda b, g, c: (row(b, g, c), d_inner // n + n_groups + g)),
            pl.BlockSpec((lc, gw), lambda b, g, c: (row(b, g, c), g)),
            small((1, 1, lc, heads)), small((1, 1, lc, heads)), small((1, 1, heads, lc)),
            pl.BlockSpec((1, gw), lambda b, g, c: (0, g)),
            pl.BlockSpec((1, gw), lambda b, g, c: (0, g)),
        ],
        out_specs=pl.BlockSpec((lc, gw), lambda b, g, c: (row(b, g, c), g)),
        out_shape=jax.ShapeDtypeStruct((t, d_inner), BF16),
        scratch_shapes=[pltpu.VMEM((n, gw), F32)],
        compiler_params=_params("parallel", "parallel", "arbitrary"),
        name="ssd_scan",
    )(xbc, xbc, xbc, z, dt, cs, cst,
      jnp.repeat(d_skip.astype(F32), head_dim).reshape(1, d_inner),
      norm_w.reshape(1, d_inner).astype(F32))


def _ssd_mixer(x, batch, norm_w, in_proj, conv_w, conv_b, dt_bias, a_log, d_skip, inner_norm_w, out_proj):
    d_inner = out_proj.shape[0]
    conv_dim = conv_w.shape[0]
    n_heads = dt_bias.shape[0]
    n_groups = (conv_dim - d_inner) // 2 // SSD_D_STATE
    main = d_inner + conv_dim
    h, dt, cs, cst = _ssd_norm_dt(x, norm_w, in_proj, main, dt_bias, a_log, n_groups)
    z = _matmul_wcast(h, in_proj, d_inner, BF16)
    xbc = _matmul_wcast_conv(h, in_proj, d_inner, conv_w, conv_b, x.shape[0] // batch)
    y = _ssd_scan(z, xbc, dt, cs, cst, d_skip, inner_norm_w, batch, d_inner, n_groups,
                  n_heads // n_groups)
    return _matmul_residual(y, out_proj.astype(BF16), x)


def _gla_norm_gate_kernel(x_ref, nw_ref, wlow_ref, up_ref, bias_ref, h_ref, o_ref, *, eps):
    h = _rms_normalize(x_ref[...], nw_ref[...], eps).astype(BF16)
    h_ref[...] = h
    rank = up_ref.shape[0]
    low = jnp.dot(h, wlow_ref[...].astype(BF16), preferred_element_type=F32)[:, :rank]
    l_hi, l_mid, l_lo = _pieces3(low)
    u_hi, u_mid, u_lo = _pieces3(up_ref[...])
    lhs = jnp.concatenate([l_hi, l_mid, l_lo, l_hi, l_mid, l_hi], axis=1).astype(BF16)
    rhs = jnp.concatenate([u_hi, u_hi, u_hi, u_mid, u_mid, u_lo], axis=0).astype(BF16)
    x = jnp.dot(lhs, rhs, preferred_element_type=F32) + bias_ref[...]
    o_ref[...] = (jnp.minimum(x, 0.0) - jnp.log1p(jnp.exp(-jnp.abs(x)))) * (1.0 / GLA_GATE_NORMALIZER)


def _gla_norm_gate(x, norm_w, w_low, gk_up, gk_bias):
    t, d = x.shape
    rank, kd = gk_up.shape
    pad = (-rank) % V7X_LANES
    w_low = jnp.pad(w_low.astype(F32), ((0, 0), (0, pad)))
    gk_up = gk_up.astype(F32)
    tr = _tile(t, NORM_ROW_TILE, V7X_F32_SUBLANES)
    rows = pl.BlockSpec((tr, d), lambda i: (i, 0))
    return pl.pallas_call(
        functools.partial(_gla_norm_gate_kernel, eps=NORM_EPS),
        grid=(t // tr,),
        in_specs=[rows, pl.BlockSpec((1, d), lambda i: (0, 0)),
                  pl.BlockSpec((d, rank + pad), lambda i: (0, 0)),
                  pl.BlockSpec((rank, kd), lambda i: (0, 0)),
                  pl.BlockSpec((1, kd), lambda i: (0, 0))],
        out_specs=[rows, pl.BlockSpec((tr, kd), lambda i: (i, 0))],
        out_shape=[jax.ShapeDtypeStruct((t, d), BF16), jax.ShapeDtypeStruct((t, kd), F32)],
        compiler_params=_params("parallel"),
        name="gla_norm_gate",
    )(x, norm_w.reshape(1, d).astype(F32), w_low, gk_up, gk_bias.reshape(1, kd).astype(F32))


def _gla_scan_kernel(q_ref, k_ref, v_ref, g_ref, go_ref, nw_ref, o_ref, state_ref, *,
                     heads, chunk, scale, eps):
    @pl.when(pl.program_id(2) == 0)
    def _():
        state_ref[...] = jnp.zeros_like(state_ref)

    rows = q_ref.shape[0]
    nsub = rows // chunk
    hk = q_ref.shape[1] // heads
    hv = v_ref.shape[1] // heads
    shift = chunk.bit_length() - 1
    ri = lax.broadcasted_iota(jnp.int32, (rows, rows), 0)
    ci = lax.broadcasted_iota(jnp.int32, (rows, rows), 1)
    causal = ri >= ci
    in_chunk_causal = causal & (lax.shift_right_logical(ri, shift) == lax.shift_right_logical(ci, shift))
    cumsum_m = jnp.concatenate([in_chunk_causal.astype(BF16)] * 3, axis=1)
    blk = lambda a, s: a[s * chunk:(s + 1) * chunk]
    q = q_ref[...].astype(F32) * scale
    k = k_ref[...].astype(F32)
    cs = jnp.dot(cumsum_m, _split3_rows(g_ref[...]), preferred_element_type=F32)
    tot = [cs[(s + 1) * chunk - 1:(s + 1) * chunk, :] for s in range(nsub)]
    before = [jnp.zeros_like(tot[0])]
    for s in range(nsub):
        before.append(before[s] + tot[s])
    total = before[nsub]
    q_dec = q * jnp.exp(cs)
    q_dec_bf = q_dec.astype(BF16)
    k_inv = (k * jnp.exp(-cs)).astype(BF16)
    k_end = [blk(k, s) * jnp.exp(tot[s] - blk(cs, s)) for s in range(nsub)]
    q_in = jnp.concatenate([blk(q_dec, s) * jnp.exp(before[s]) for s in range(nsub)], axis=0).astype(BF16)
    k_out = jnp.concatenate([k_end[s] * jnp.exp(total - before[s + 1]) for s in range(nsub)],
                            axis=0).astype(BF16)
    keys = []
    for s in range(nsub):
        rows_s = [(k_end[j] * jnp.exp(before[s] - before[j + 1])).astype(BF16) for j in range(s)]
        rows_s.append(blk(k_inv, s))
        rows_s += [jnp.zeros((chunk, heads * hk), BF16)] * (nsub - 1 - s)
        keys.append(jnp.concatenate(rows_s, axis=0))
    total_col = jnp.exp(jnp.broadcast_to(total, (V7X_LANES, heads * hk)).T[:, 0:1])
    for hh in range(heads):
        kc = slice(hh * hk, (hh + 1) * hk)
        vc = slice(hh * hv, (hh + 1) * hv)
        v = v_ref[:, vc]
        score_rows = [lax.dot_general(blk(q_dec_bf, s)[:, kc], keys[s][:, kc], (((1,), (1,)), ((), ())),
                                      preferred_element_type=F32) for s in range(nsub)]
        scores = jnp.where(causal, jnp.concatenate(score_rows, axis=0), 0.0).astype(BF16)
        state = state_ref[hh]
        o = (jnp.dot(scores, v, preferred_element_type=F32)
             + jnp.dot(q_in[:, kc], state.astype(BF16), preferred_element_type=F32))
        state_ref[hh] = state * total_col[kc] + lax.dot_general(
            k_out[:, kc], v, (((0,), (0,)), ((), ())), preferred_element_type=F32)
        ms = jnp.mean(o * o, axis=-1, keepdims=True)
        o = o * lax.rsqrt(ms + eps) * nw_ref[...]
        o_ref[:, vc] = (o * _silu(go_ref[:, vc].astype(F32))).astype(o_ref.dtype)


def _gla_scan(qkvg, log_g, norm_w, batch, key_dim, value_dim, n_heads):
    t = qkvg.shape[0]
    hk, hv = key_dim // n_heads, value_dim // n_heads
    seqlen = t // batch
    rb = _tile(seqlen, GLA_ROW_BLOCK, GLA_CHUNK)
    nb = seqlen // rb
    hpb = GLA_HEADS_PER_STEP if n_heads % GLA_HEADS_PER_STEP == 0 else 1
    kw, vw = hpb * hk, hpb * hv
    assert rb % GLA_CHUNK == 0 and (2 * key_dim) % vw == 0 and value_dim % vw == 0
    row = lambda b, h, i: b * nb + i
    k0 = key_dim // kw
    v0 = 2 * key_dim // vw
    g0 = (2 * key_dim + value_dim) // vw
    return pl.pallas_call(
        functools.partial(_gla_scan_kernel, heads=hpb, chunk=GLA_CHUNK, scale=hk ** -0.5,
                          eps=INNER_NORM_EPS),
        grid=(batch, n_heads // hpb, nb),
        in_specs=[
            pl.BlockSpec((rb, kw), lambda b, h, i: (row(b, h, i), h)),
            pl.BlockSpec((rb, kw), lambda b, h, i: (row(b, h, i), k0 + h)),
            pl.BlockSpec((rb, vw), lambda b, h, i: (row(b, h, i), v0 + h)),
            pl.BlockSpec((rb, kw), lambda b, h, i: (row(b, h, i), h)),
            pl.BlockSpec((rb, vw), lambda b, h, i: (row(b, h, i), g0 + h)),
            pl.BlockSpec((1, hv), lambda b, h, i: (0, 0)),
        ],
        out_specs=pl.BlockSpec((rb, vw), lambda b, h, i: (row(b, h, i), h)),
        out_shape=jax.ShapeDtypeStruct((t, value_dim), BF16),
        scratch_shapes=[pltpu.VMEM((hpb, hk, hv), F32)],
        compiler_params=_params("parallel", "parallel", "arbitrary"),
        name="gla_scan",
    )(qkvg, qkvg, qkvg, log_g, qkvg, norm_w.reshape(1, hv).astype(F32))


def _gla_mixer(x, batch, norm_w, in_proj, gk_up, gk_bias, inner_norm_w, out_proj):
    key_dim = gk_up.shape[1]
    value_dim = out_proj.shape[0]
    n_heads = value_dim // inner_norm_w.shape[0]
    main = 2 * key_dim + 2 * value_dim
    h, log_g = _gla_norm_gate(x, norm_w, in_proj[:, main:], gk_up, gk_bias)
    qkvg = _matmul_wcast(h, in_proj, main, BF16)
    o = _gla_scan(qkvg, log_g, inner_norm_w, batch, key_dim, value_dim, n_heads)
    return _matmul_residual(o, out_proj.astype(BF16), x)


def kernel(x, l0_mixer_norm, l0_ssd_in_proj, l0_ssd_conv_w, l0_ssd_conv_b, l0_ssd_dt_bias, l0_ssd_a_log, l0_ssd_d, l0_ssd_norm_w, l0_ssd_out_proj, l0_ffn_norm, l0_ffn_w_gate, l0_ffn_w_up, l0_ffn_w_down, l1_mixer_norm, l1_gla_in_proj, l1_gla_gk_up, l1_gla_gk_bias, l1_gla_norm_w, l1_gla_out_proj, l1_ffn_norm, l1_ffn_w_gate, l1_ffn_w_up, l1_ffn_w_down, final_norm):
    batch, seqlen, d = x.shape
    xf = x.reshape(batch * seqlen, d).astype(F32)
    xf = _ssd_mixer(xf, batch, l0_mixer_norm, l0_ssd_in_proj, l0_ssd_conv_w, l0_ssd_conv_b,
                    l0_ssd_dt_bias, l0_ssd_a_log, l0_ssd_d, l0_ssd_norm_w, l0_ssd_out_proj)
    xf = _ffn(xf, l0_ffn_norm, l0_ffn_w_gate, l0_ffn_w_up, l0_ffn_w_down)
    xf = _gla_mixer(xf, batch, l1_mixer_norm, l1_gla_in_proj, l1_gla_gk_up, l1_gla_gk_bias,
                    l1_gla_norm_w, l1_gla_out_proj)
    xf = _ffn(xf, l1_ffn_norm, l1_ffn_w_gate, l1_ffn_w_up, l1_ffn_w_down)
    return _rmsnorm(xf, final_norm, NORM_EPS, x.dtype).reshape(batch, seqlen, d)
```
